```python
import math
import jax
import jax.numpy as jnp
from jax import lax
import numpy as np

D_MODEL = 1024
BATCH = 8
SEQ = 2048
DEPTH = 2

CTX_LEN = 256
GRID_W = 64
EPS = 1e-6
F32 = jnp.float32

N_AB = (DEPTH + 1) // 2
N_C = DEPTH // 2

A_HEADS = 4
A_DK = 128
A_DV = 128
A_WIDTH = A_HEADS * A_DV
CHUNK = 64
CONV_W = 4

B_WIDTH = 512
B_BLOCKS = 8
B_BLK = B_WIDTH // B_BLOCKS
RG_C = 8.0

AB_PROJ = 4 * A_WIDTH + 4 * A_HEADS + 2 * B_WIDTH
MIX_WIDTH = A_WIDTH + B_WIDTH

C_HEADS = 8
C_KV_HEADS = 2
C_HD = 128
C_QBLOCK = 128
ROPE_THETA = 10000.0
C_QKV = (C_HEADS + 2 * C_KV_HEADS) * C_HD

N_GROUPS = 4
EXP_PER_GROUP = 8
N_EXPERTS = N_GROUPS * EXP_PER_GROUP
D_EXPERT = 512
TOP_K = 2

kernel_name = "hybrid_deltanet_rglru_gqa_hmoe_dit"


def rms_norm(x, w):
    xf = x.astype(F32)
    y = xf * lax.rsqrt(jnp.mean(xf * xf, axis=-1, keepdims=True) + EPS)
    return (y * w.astype(F32)).astype(x.dtype)


def l2norm(x):
    return x * lax.rsqrt(jnp.sum(x * x, axis=-1, keepdims=True) + EPS)


def ada_mod(cond, w, b):
    m = jax.nn.silu(cond) @ w + b
    return [t[:, None, :] for t in jnp.split(m, 6, axis=-1)]


def modulate(x, gain, shift, scale):
    return rms_norm(x, gain) * (1 + scale) + shift


def dw_conv(x, w):
    ch = x.shape[-1]
    pad = (CONV_W // 2, CONV_W - 1 - CONV_W // 2)
    return lax.conv_general_dilated(x, w[:, None, :].astype(x.dtype), window_strides=(1,), padding=[pad],
                                    dimension_numbers=('NWC', 'WIO', 'NWC'), feature_group_count=ch)


def gated_delta_chunked(q, k, v, g, beta, s0):
    bsz, t, nh, _ = q.shape
    dv = v.shape[-1]
    n = t // CHUNK

    def to_chunks(a):
        return jnp.moveaxis(a.reshape(bsz, n, CHUNK, nh, *a.shape[3:]), 3, 1)

    q, k, v, g, beta = [to_chunks(a) for a in (q, k, v, g, beta)]
    gc = jnp.cumsum(g, axis=-1)
    idx = jnp.arange(CHUNK)
    lower = idx[:, None] >= idx[None, :]
    strict = idx[:, None] > idx[None, :]
    diff = gc[..., :, None] - gc[..., None, :]
    decay = jnp.where(lower, jnp.exp(jnp.where(lower, diff, 0.0)), 0.0)
    kb = k * beta[..., None]
    lmat = jnp.where(strict, jnp.einsum('bhnid,bhnjd->bhnij', kb, k) * decay, 0.0)
    eye = jnp.eye(CHUNK, dtype=F32)
    rhs = jnp.concatenate([v * beta[..., None], kb * jnp.exp(gc)[..., None]], axis=-1)
    sol = lax.linalg.triangular_solve(lmat + eye, rhs, left_side=True, lower=True, unit_diagonal=True)
    u, w = sol[..., :dv], sol[..., dv:]
    attn = jnp.einsum('bhnid,bhnjd->bhnij', q, k) * decay
    q_dec = q * jnp.exp(gc)[..., None]
    g_last = gc[..., -1]
    k_dec = k * jnp.exp(g_last[..., None] - gc)[..., None]

    def step(s, xs):
        u_i, w_i, q_i, k_i, a_i, gl_i = xs
        v_new = u_i - jnp.einsum('bhcd,bhde->bhce', w_i, s)
        o_i = jnp.einsum('bhcd,bhde->bhce', q_i, s) + jnp.einsum('bhij,bhje->bhie', a_i, v_new)
        s = s * jnp.exp(gl_i)[..., None, None] + jnp.einsum('bhcd,bhce->bhde', k_i, v_new)
        return s, o_i

    xs = [jnp.moveaxis(a, 2, 0) for a in (u, w, q_dec, k_dec, attn, g_last)]
    s_fin, o = lax.scan(step, s0, xs)
    o = jnp.transpose(o, (1, 0, 3, 2, 4)).reshape(bsz, t, nh, dv)
    return o, s_fin


def linear_scan(a, b, h0):
    def comb(left, right):
        al, bl = left
        ar, br = right
        return al * ar, ar * bl + br
    acum, bcum = lax.associative_scan(comb, (a, b), axis=1)
    h = acum * h0[:, None, :] + bcum
    return h, h[:, -1]


def bidirectional(run, ctx_f, lat_f, ctx_b, lat_b, s0):
    rev = lambda xs: [jnp.flip(a, axis=1) for a in xs]
    oc_f, sc_f = run(*ctx_f, s0)
    ol_f, _ = run(*lat_f, sc_f)
    oc_b, sc_b = run(*rev(ctx_b), s0)
    ol_b, _ = run(*rev(lat_b), sc_b)
    return oc_f + jnp.flip(oc_b, axis=1), ol_f + jnp.flip(ol_b, axis=1)


def ab_mixer(h_ctx, h_lat, w_in, conv_qkv, a_log, dt_bias, onorm, conv_x,
             rg_wr, rg_br, rg_wi, rg_bi, rg_lam, w_out):
    o1 = 3 * A_WIDTH
    o2 = 4 * A_WIDTH
    o3 = o2 + 4 * A_HEADS
    o4 = o3 + B_WIDTH

    def prep(h):
        bsz, t, _ = h.shape
        z = h @ w_in
        qkv = jax.nn.silu(dw_conv(z[..., :o1], conv_qkv)).astype(F32)
        q, k, v = [a.reshape(bsz, t, A_HEADS, A_DK) for a in jnp.split(qkv, 3, axis=-1)]
        q = l2norm(q) * A_DK ** -0.5
        k = l2norm(k)
        gates = z[..., o2:o3].astype(F32).reshape(bsz, t, 2, 2, A_HEADS)
        decay = -jnp.exp(a_log.astype(F32)) * jax.nn.softplus(gates[:, :, :, 0] + dt_bias)
        beta = jax.nn.sigmoid(gates[:, :, :, 1])
        xc = dw_conv(z[..., o3:o4], conv_x).astype(F32)
        xcb = xc.reshape(bsz, t, B_BLOCKS, B_BLK)
        r = jax.nn.sigmoid(jnp.einsum('btgk,dgkj->btdgj', xcb, rg_wr) + rg_br).reshape(bsz, t, 2, B_WIDTH)
        i = jax.nn.sigmoid(jnp.einsum('btgk,dgkj->btdgj', xcb, rg_wi) + rg_bi).reshape(bsz, t, 2, B_WIDTH)
        log_a = -RG_C * r * jax.nn.softplus(-rg_lam.astype(F32))
        a = jnp.exp(log_a)
        b = jnp.sqrt(-jnp.expm1(2.0 * log_a)) * i * xc[:, :, None, :]
        delta_dirs = [(q, k, v, decay[:, :, d], beta[:, :, d]) for d in (0, 1)]
        lru_dirs = [(a[:, :, d], b[:, :, d]) for d in (0, 1)]
        return delta_dirs, lru_dirs, z[..., o1:o2], z[..., o4:]

    dc, lc, gout_c, y_c = prep(h_ctx)
    dl, ll, gout_l, y_l = prep(h_lat)
    bsz = h_lat.shape[0]
    s0_delta = jnp.zeros((bsz, A_HEADS, A_DK, A_DV), F32)
    s0_lru = jnp.zeros((bsz, B_WIDTH), F32)
    oa_c, oa_l = bidirectional(gated_delta_chunked, dc[0], dl[0], dc[1], dl[1], s0_delta)
    ob_c, ob_l = bidirectional(linear_scan, lc[0], ll[0], lc[1], ll[1], s0_lru)

    def merge(oa, ob, gout, y):
        b_, t = oa.shape[:2]
        ya = rms_norm(oa, onorm).reshape(b_, t, A_WIDTH) * jax.nn.silu(gout.astype(F32))
        yb = ob * jax.nn.gelu(y.astype(F32))
        return jnp.concatenate([ya, yb], axis=-1).astype(h_lat.dtype) @ w_out

    return merge(oa_c, ob_c, gout_c, y_c), merge(oa_l, ob_l, gout_l, y_l)


def axial_rope(t):
    rows = t // GRID_W
    row = jnp.repeat(jnp.arange(rows, dtype=F32), GRID_W)
    col = jnp.tile(jnp.arange(GRID_W, dtype=F32), rows)
    n_freq = C_HD // 4
    inv = ROPE_THETA ** (-jnp.arange(n_freq, dtype=F32) / n_freq)
    ang = jnp.concatenate([row[:, None] * inv, col[:, None] * inv], axis=-1)
    return jnp.cos(ang), jnp.sin(ang)


def apply_rope(x, cos, sin):
    xf = x.astype(F32).reshape(*x.shape[:-1], C_HD // 2, 2)
    x1, x2 = xf[..., 0], xf[..., 1]
    c = cos[None, :, None, :]
    s = sin[None, :, None, :]
    out = jnp.stack([x1 * c - x2 * s, x1 * s + x2 * c], axis=-1).reshape(x.shape)
    return out.astype(x.dtype)


def blocked_attention(q, k, v):
    bsz, tq = q.shape[:2]
    nblk = tq // C_QBLOCK
    grp = C_HEADS // C_KV_HEADS
    qb = (q * C_HD ** -0.5).reshape(bsz, nblk, C_QBLOCK, C_KV_HEADS, grp, C_HD)
    qb = jnp.moveaxis(qb, 1, 0)

    def one_block(qi):
        s = jnp.einsum('bqhgd,bkhd->bhgqk', qi, k, preferred_element_type=F32)
        p = jax.nn.softmax(s, axis=-1).astype(v.dtype)
        return jnp.einsum('bhgqk,bkhd->bqhgd', p, v)

    o = lax.map(one_block, qb)
    return jnp.moveaxis(o, 0, 1).reshape(bsz, tq, C_HEADS * C_HD)


def attn_mixer(h_ctx, h_lat, w_qkv, q_norm, k_norm, w_out, need_ctx):
    nq = C_HEADS * C_HD
    nk = C_KV_HEADS * C_HD
    heads = lambda z, n: z.reshape(*z.shape[:2], n, C_HD)
    t = h_lat.shape[1]
    zl = h_lat @ w_qkv
    cos, sin = axial_rope(t)
    ql = apply_rope(rms_norm(heads(zl[..., :nq], C_HEADS), q_norm), cos, sin)
    kl = apply_rope(rms_norm(heads(zl[..., nq:nq + nk], C_KV_HEADS), k_norm), cos, sin)
    vl = heads(zl[..., nq + nk:], C_KV_HEADS)
    zc = h_ctx @ w_qkv[:, nq:]
    kc = rms_norm(heads(zc[..., :nk], C_KV_HEADS), k_norm)
    vc = heads(zc[..., nk:], C_KV_HEADS)
    k_all = jnp.concatenate([kl, kc], axis=1)
    v_all = jnp.concatenate([vl, vc], axis=1)
    y_lat = blocked_attention(ql, k_all, v_all) @ w_out
    y_ctx = None
    if need_ctx:
        qc = rms_norm(heads(h_ctx @ w_qkv[:, :nq], C_HEADS), q_norm)
        y_ctx = blocked_attention(qc, kc, vc) @ w_out
    return y_ctx, y_lat


def hier_moe(h, w_grp, b_grp, w_exp, b_exp, w_gate, w_up, w_down):
    bsz, t, d = h.shape
    x = h.reshape(bsz * t, d)
    g_logit = (x @ w_grp).astype(F32) + b_grp
    _, g_idx = lax.top_k(g_logit, 1)
    g_w = jnp.take_along_axis(jax.nn.softmax(g_logit, axis=-1), g_idx, axis=-1)
    e_logit = ((x @ w_exp).astype(F32) + b_exp).reshape(-1, N_GROUPS, EXP_PER_GROUP)
    e_in = jnp.take_along_axis(e_logit, g_idx[:, :, None], axis=1)[:, 0]
    e_val, e_idx = lax.top_k(e_in, TOP_K)
    w_sel = jax.nn.softmax(e_val, axis=-1) * g_w
    expert = g_idx * EXP_PER_GROUP + e_idx
    gates = jnp.sum(jax.nn.one_hot(expert, N_EXPERTS, dtype=F32) * w_sel[..., None], axis=1)
    hg = jnp.einsum('nd,edf->nef', x, w_gate).astype(F32)
    hu = jnp.einsum('nd,edf->nef', x, w_up).astype(F32)
    act = (jax.nn.silu(hg) * hu * gates[..., None]).astype(x.dtype)
    y = jnp.einsum('nef,efd->nd', act, w_down)
    return y.reshape(bsz, t, d).astype(h.dtype)


def setup_inputs(seed: int = 0) -> dict:
    key = jax.random.key(seed)
    k = jax.random.split(key, 32)
    nrm = lambda i, shape, scale: jax.random.normal(k[i], shape, F32) * scale
    gain = lambda i, shape: 1.0 + 0.05 * jax.random.normal(k[i], shape, F32)
    a_log = jnp.log(jax.random.uniform(k[10], (N_AB, 2, A_HEADS), F32, 1.0, 16.0))
    dt = jnp.exp(jax.random.uniform(k[11], (N_AB, 2, A_HEADS), F32, math.log(1e-3), math.log(0.1)))
    dt_bias = dt + jnp.log(-jnp.expm1(-dt))
    a_base = jax.random.uniform(k[18], (N_AB, 2, B_WIDTH), F32, 0.9, 0.999) ** (1.0 / RG_C)
    rg_lam = jnp.log(a_base) - jnp.log1p(-a_base)
    return {
        'x': nrm(0, (BATCH, SEQ, D_MODEL), 1.0),
        'c': nrm(1, (BATCH, D_MODEL), 1.0),
        'ctx': nrm(2, (BATCH, CTX_LEN, D_MODEL), 1.0),
        'c_ctx': nrm(3, (D_MODEL,), 1.0),
        'ada_w': nrm(4, (DEPTH, D_MODEL, 6 * D_MODEL), D_MODEL ** -0.5),
        'ada_b': nrm(5, (DEPTH, 6 * D_MODEL), 0.02),
        'norm1': gain(6, (DEPTH, D_MODEL)),
        'norm2': gain(7, (DEPTH, D_MODEL)),
        'final_norm': gain(8, (D_MODEL,)),
        'ab_w_in': nrm(9, (N_AB, D_MODEL, AB_PROJ), D_MODEL ** -0.5),
        'ab_conv_qkv': nrm(12, (N_AB, CONV_W, 3 * A_WIDTH), CONV_W ** -0.5),
        'ab_a_log': a_log,
        'ab_dt_bias': dt_bias,
        'ab_onorm': gain(13, (N_AB, A_DV)),
        'ab_conv_x': nrm(14, (N_AB, CONV_W, B_WIDTH), CONV_W ** -0.5),
        'ab_rg_wr': nrm(15, (N_AB, 2, B_BLOCKS, B_BLK, B_BLK), B_BLK ** -0.5),
        'ab_rg_br': nrm(16, (N_AB, 2, B_BLOCKS, B_BLK), 0.1),
        'ab_rg_wi': nrm(17, (N_AB, 2, B_BLOCKS, B_BLK, B_BLK), B_BLK ** -0.5),
        'ab_rg_bi': nrm(19, (N_AB, 2, B_BLOCKS, B_BLK), 0.1),
        'ab_rg_lam': rg_lam,
        'ab_w_out': nrm(20, (N_AB, MIX_WIDTH, D_MODEL), MIX_WIDTH ** -0.5),
        'at_w_qkv': nrm(21, (N_C, D_MODEL, C_QKV), D_MODEL ** -0.5),
        'at_q_norm': gain(22, (N_C, C_HD)),
        'at_k_norm': gain(23, (N_C, C_HD)),
        'at_w_out': nrm(24, (N_C, C_HEADS * C_HD, D_MODEL), (C_HEADS * C_HD) ** -0.5),
        'moe_w_grp': nrm(25, (DEPTH, D_MODEL, N_GROUPS), D_MODEL ** -0.5),
        'moe_b_grp': nrm(26, (DEPTH, N_GROUPS), 0.01),
        'moe_w_exp': nrm(27, (DEPTH, D_MODEL, N_EXPERTS), D_MODEL ** -0.5),
        'moe_b_exp': nrm(28, (DEPTH, N_EXPERTS), 0.01),
        'moe_w_gate': nrm(29, (DEPTH, N_EXPERTS, D_MODEL, D_EXPERT), D_MODEL ** -0.5),
        'moe_w_up': nrm(30, (DEPTH, N_EXPERTS, D_MODEL, D_EXPERT), D_MODEL ** -0.5),
        'moe_w_down': nrm(31, (DEPTH, N_EXPERTS, D_EXPERT, D_MODEL), D_EXPERT ** -0.5),
    }


def reference(x, c, ctx, c_ctx, ada_w, ada_b, norm1, norm2, final_norm,
              ab_w_in, ab_conv_qkv, ab_a_log, ab_dt_bias, ab_onorm, ab_conv_x,
              ab_rg_wr, ab_rg_br, ab_rg_wi, ab_rg_bi, ab_rg_lam, ab_w_out,
              at_w_qkv, at_q_norm, at_k_norm, at_w_out,
              moe_w_grp, moe_b_grp, moe_w_exp, moe_b_exp, moe_w_gate, moe_w_up, moe_w_down):
    for l in range(DEPTH):
        last = l == DEPTH - 1
        j = l // 2
        sh1, sc1, g1, sh2, sc2, g2 = ada_mod(c, ada_w[l], ada_b[l])
        csh1, csc1, cg1, csh2, csc2, cg2 = ada_mod(c_ctx[None], ada_w[l], ada_b[l])
        h_lat = modulate(x, norm1[l], sh1, sc1)
        h_ctx = modulate(ctx, norm1[l], csh1, csc1)
        if l % 2 == 0:
            y_ctx, y_lat = ab_mixer(h_ctx, h_lat, ab_w_in[j], ab_conv_qkv[j], ab_a_log[j], ab_dt_bias[j],
                                    ab_onorm[j], ab_conv_x[j], ab_rg_wr[j], ab_rg_br[j], ab_rg_wi[j],
                                    ab_rg_bi[j], ab_rg_lam[j], ab_w_out[j])
        else:
            y_ctx, y_lat = attn_mixer(h_ctx, h_lat, at_w_qkv[j], at_q_norm[j], at_k_norm[j], at_w_out[j],
                                      not last)
        x = x + g1 * y_lat
        if not last:
            ctx = ctx + cg1 * y_ctx
        moe_p = (moe_w_grp[l], moe_b_grp[l], moe_w_exp[l], moe_b_exp[l], moe_w_gate[l], moe_w_up[l], moe_w_down[l])
        x = x + g2 * hier_moe(modulate(x, norm2[l], sh2, sc2), *moe_p)
        if not last:
            ctx = ctx + cg2 * hier_moe(modulate(ctx, norm2[l], csh2, csc2), *moe_p)
    return rms_norm(x, final_norm)
```

```python
import functools
import math

import jax
import jax.numpy as jnp
from jax import lax
from jax.experimental import pallas as pl
from jax.experimental.pallas import tpu as pltpu

F32 = jnp.float32
BF16 = jnp.bfloat16
HIGHEST = lax.Precision.HIGHEST

D_MODEL = 1024
EPS = 1e-6
TM = 256
LANES = 128
SUBLANES = 8

A_HEADS = 4
A_DK = 128
A_WIDTH = A_HEADS * A_DK
CHUNK = 64
CONV_W = 4
B_WIDTH = 512
B_BLOCKS = 8
B_BLK = B_WIDTH // B_BLOCKS
RG_C = 8.0

C_HEADS = 8
C_KV_HEADS = 2
C_HD = 128
C_GRP = C_HEADS // C_KV_HEADS
ROPE_THETA = 10000.0

N_GROUPS = 4
EXP_PER_GROUP = 8
N_EXPERTS = N_GROUPS * EXP_PER_GROUP
D_EXPERT = 512

VMEM_LIMIT = 56 * 1024 * 1024

ZC_QKV = 0
ZC_XB = 3 * A_WIDTH
ZC_GOUT = ZC_XB + B_WIDTH
ZC_YB = ZC_GOUT + A_WIDTH
ZC_GATE = ZC_YB + B_WIDTH
ZC_TOTAL = ZC_GATE + LANES
ZC_CONV = ZC_GOUT
HALO = SUBLANES


def _cparams(sem):
    return pltpu.CompilerParams(dimension_semantics=sem, vmem_limit_bytes=VMEM_LIMIT)


def _sigmoid(x):
    return jax.nn.sigmoid(x)


def _silu(x):
    return x * jax.nn.sigmoid(x)


def _softplus(x):
    return jnp.maximum(x, 0.0) + jnp.log1p(jnp.exp(-jnp.abs(x)))


def _gelu_tanh(x):
    c = math.sqrt(2.0 / math.pi)
    return 0.5 * x * (1.0 + jnp.tanh(c * (x + 0.044715 * (x * x * x))))


def _modulate(x, gain, shift, scale):
    y = x * lax.rsqrt(jnp.mean(x * x, axis=-1, keepdims=True) + EPS)
    return (y * gain) * (1.0 + scale) + shift


def _ada_kernel(cond_ref, w_ref, b_ref, o_ref):
    s = _silu(cond_ref[...]).astype(BF16)
    o_ref[...] = jnp.dot(s, w_ref[...].astype(BF16), preferred_element_type=F32) + b_ref[...]


def _ada_mod(cond, ada_w, ada_b):
    depth = ada_w.shape[0]
    tn = 1536
    nn = 6 * D_MODEL // tn
    out = pl.pallas_call(
        _ada_kernel,
        grid=(depth, nn),
        in_specs=[
            pl.BlockSpec((16, D_MODEL), lambda l, n: (0, 0)),
            pl.BlockSpec((None, D_MODEL, tn), lambda l, n: (l, 0, n)),
            pl.BlockSpec((None, 1, tn), lambda l, n: (l, 0, n)),
        ],
        out_specs=pl.BlockSpec((None, 16, tn), lambda l, n: (l, 0, n)),
        out_shape=jax.ShapeDtypeStruct((depth, 16, 6 * D_MODEL), F32),
        compiler_params=_cparams(("arbitrary", "arbitrary")),
        name="ada_mod",
    )(cond, ada_w, ada_b.reshape(depth, 1, 6 * D_MODEL))
    return out.reshape(depth, 16, 6, D_MODEL)


def _inproj_ab_kernel(nt, nc, xc_ref, xp_ref, xn_ref, mod_ref, n1_ref, w_ref, cq_ref, cx_ref,
                      alog_ref, dt_ref, wbd_ref, bbd_ref, lam_ref,
                      q_ref, k_ref, v_ref, go_ref, yb_ref, g_ref, a0_ref, b0_ref, a1_ref, b1_ref,
                      zbuf, xcbuf):
    i = pl.program_id(0)
    r = i % nt
    is_ctx = r < nc
    prev_ok = jnp.logical_and(r > 0, ((r - 1) < nc) == is_ctx)
    next_ok = jnp.logical_and(r < nt - 1, ((r + 1) < nc) == is_ctx)

    gain = n1_ref[...]
    shift = mod_ref[0:1, :]
    scale = mod_ref[1:2, :]
    xall = jnp.concatenate([xp_ref[...], xc_ref[...], xn_ref[...]], axis=0)
    h = _modulate(xall, gain, shift, scale).astype(BF16)
    zbuf[...] = jnp.dot(h, w_ref[...], preferred_element_type=F32)
    zbuf[0:HALO, 0:ZC_CONV] = jnp.where(prev_ok, zbuf[0:HALO, 0:ZC_CONV], 0.0)
    zbuf[HALO + TM:, 0:ZC_CONV] = jnp.where(next_ok, zbuf[HALO + TM:, 0:ZC_CONV], 0.0)

    def conv(c0, w_taps_ref, wc0):
        acc = None
        for j in range(CONV_W):
            start = HALO - CONV_W // 2 + j
            term = zbuf[start:start + TM, c0:c0 + LANES] * w_taps_ref[j:j + 1, wc0:wc0 + LANES]
            acc = term if acc is None else acc + term
        return acc

    outs = (q_ref, k_ref, v_ref)
    for s in range(3 * A_HEADS):
        y = _silu(conv(s * LANES, cq_ref, s * LANES))
        if s < 2 * A_HEADS:
            y = y * lax.rsqrt(jnp.sum(y * y, axis=-1, keepdims=True) + EPS)
        if s < A_HEADS:
            y = y * (A_DK ** -0.5)
        hh = s % A_HEADS
        outs[s // A_HEADS][:, hh * LANES:(hh + 1) * LANES] = y

    for s in range(B_WIDTH // LANES):
        xcbuf[:, s * LANES:(s + 1) * LANES] = conv(ZC_XB + s * LANES, cx_ref, s * LANES)

    go_ref[...] = zbuf[HALO:HALO + TM, ZC_GOUT:ZC_GOUT + A_WIDTH]
    yb_ref[...] = zbuf[HALO:HALO + TM, ZC_YB:ZC_YB + B_WIDTH]

    zg = zbuf[HALO:HALO + TM, ZC_GATE:ZC_GATE + LANES]
    lane = lax.broadcasted_iota(jnp.int32, (TM, LANES), 1)
    dec = -jnp.exp(alog_ref[...]) * _softplus(zg + dt_ref[...])
    g_ref[...] = jnp.where(jnp.bitwise_and(lane, 7) < A_HEADS, dec, _sigmoid(zg))

    xc = xcbuf[...]
    rg = jnp.dot(xc.astype(BF16), wbd_ref[...], preferred_element_type=F32) + bbd_ref[...]
    sp = _softplus(-lam_ref[...])
    for d, (a_ref, b_ref) in enumerate(((a0_ref, b0_ref), (a1_ref, b1_ref))):
        rr = _sigmoid(rg[:, d * B_WIDTH:(d + 1) * B_WIDTH])
        ii = _sigmoid(rg[:, (2 + d) * B_WIDTH:(3 + d) * B_WIDTH])
        a = jnp.exp(-RG_C * rr * sp[:, d * B_WIDTH:(d + 1) * B_WIDTH])
        a_ref[...] = a
        b_ref[...] = jnp.sqrt(1.0 - a * a) * ii * xc


def _inproj_ab(xu, mods, norm1, w_pad, conv_qkv, conv_x, alog_vec, dt_vec, wbd, bbd, lam, nb, nt, nc):
    rows = xu.shape[0]
    ntiles = rows // TM
    hb = TM // HALO
    nhalo = rows // HALO

    def mrow(i):
        return jnp.where(i % nt < nc, nb, i // nt)

    full = lambda shape: pl.BlockSpec(shape, lambda i: (0,) * len(shape))
    row_spec = lambda w: pl.BlockSpec((TM, w), lambda i: (i, 0))
    sds = lambda w: jax.ShapeDtypeStruct((rows, w), F32)
    return pl.pallas_call(
        functools.partial(_inproj_ab_kernel, nt, nc),
        grid=(ntiles,),
        in_specs=[
            row_spec(D_MODEL),
            pl.BlockSpec((HALO, D_MODEL), lambda i: (jnp.maximum(i * hb - 1, 0), 0)),
            pl.BlockSpec((HALO, D_MODEL), lambda i: (jnp.minimum((i + 1) * hb, nhalo - 1), 0)),
            pl.BlockSpec((None, 6, D_MODEL), lambda i: (mrow(i), 0, 0)),
            full((1, D_MODEL)),
            full((D_MODEL, ZC_TOTAL)),
            full((CONV_W, 3 * A_WIDTH)),
            full((CONV_W, B_WIDTH)),
            full((1, LANES)),
            full((1, LANES)),
            full((B_WIDTH, 4 * B_WIDTH)),
            full((1, 4 * B_WIDTH)),
            full((1, 2 * B_WIDTH)),
        ],
        out_specs=[row_spec(A_WIDTH)] * 5 + [row_spec(LANES)] + [row_spec(B_WIDTH)] * 4,
        out_shape=[sds(A_WIDTH)] * 5 + [sds(LANES)] + [sds(B_WIDTH)] * 4,
        scratch_shapes=[pltpu.VMEM((TM + 2 * HALO, ZC_TOTAL), F32), pltpu.VMEM((TM, B_WIDTH), F32)],
        compiler_params=_cparams(("arbitrary",)),
        name="inproj_ab",
    )(xu, xu, xu, mods, norm1, w_pad, conv_qkv, conv_x, alog_vec, dt_vec, wbd, bbd, lam)


def _delta_kernel(qf_ref, kf_ref, vf_ref, gf_ref, qb_ref, kb_ref, vb_ref, gb_ref, of_ref, ob_ref, s_ref):
    @pl.when(pl.program_id(1) == 0)
    def _():
        s_ref[...] = jnp.zeros_like(s_ref)

    ii = lax.broadcasted_iota(jnp.int32, (CHUNK, CHUNK), 0)
    jj = lax.broadcasted_iota(jnp.int32, (CHUNK, CHUNK), 1)
    eye = (ii == jj).astype(F32)
    dirs = ((qf_ref, kf_ref, vf_ref, gf_ref, of_ref), (qb_ref, kb_ref, vb_ref, gb_ref, ob_ref))
    for d, (q_ref, k_ref, v_ref, g_ref, o_ref) in enumerate(dirs):
        lower = (ii >= jj) if d == 0 else (ii <= jj)
        strict = (ii > jj) if d == 0 else (ii < jj)
        last = CHUNK - 1 if d == 0 else 0
        gates = g_ref[...]
        gcum = jnp.dot(lower.astype(F32), gates, precision=HIGHEST, preferred_element_type=F32)
        gcum_t = gcum.T
        for h in range(A_HEADS):
            ld = d * 2 * A_HEADS + h
            lb = ld + A_HEADS
            sl = slice(h * A_DK, (h + 1) * A_DK)
            gc_col = gcum[:, ld:ld + 1]
            gc_row = gcum_t[ld:ld + 1, :]
            beta = gates[:, lb:lb + 1]
            g_last = gcum[last:last + 1, ld:ld + 1]
            decay = jnp.where(lower, jnp.exp(jnp.where(lower, gc_col - gc_row, 0.0)), 0.0)
            eg = jnp.exp(gc_col)
            q = q_ref[:, sl]
            k = k_ref[:, sl]
            v = v_ref[:, sl]
            kb = k * beta
            k16 = k.astype(BF16)
            kk = lax.dot_general(kb.astype(BF16), k16, (((1,), (1,)), ((), ())), preferred_element_type=F32)
            neg_l = jnp.where(strict, -(kk * decay), 0.0)
            tinv = eye + neg_l
            pw = neg_l
            for _ in range(int(math.log2(CHUNK)) - 1):
                pw = jnp.dot(pw, pw, precision=HIGHEST, preferred_element_type=F32)
                tinv = tinv + jnp.dot(tinv, pw, precision=HIGHEST, preferred_element_type=F32)
            rhs = jnp.concatenate([v * beta, kb * eg], axis=-1).astype(BF16)
            sol = jnp.dot(tinv.astype(BF16), rhs, preferred_element_type=F32)
            u = sol[:, :A_DK]
            w = sol[:, A_DK:]
            attn = lax.dot_general(q.astype(BF16), k16, (((1,), (1,)), ((), ())),
                                   preferred_element_type=F32) * decay
            q_dec = (q * eg).astype(BF16)
            k_dec = k * jnp.exp(g_last - gc_col)
            st = s_ref[d, h]
            st16 = st.astype(BF16)
            v_new = u - jnp.dot(w.astype(BF16), st16, preferred_element_type=F32)
            v16 = v_new.astype(BF16)
            o = jnp.dot(q_dec, st16, preferred_element_type=F32) + jnp.dot(
                attn.astype(BF16), v16, preferred_element_type=F32)
            o_ref[:, sl] = o
            s_ref[d, h] = st * jnp.exp(g_last) + jnp.dot(k_dec.T.astype(BF16), v16, preferred_element_type=F32)


def _bwd_order(s, n_ctx, n_all):
    return jnp.where(s < n_ctx, n_ctx - 1 - s, n_all - 1 - (s - n_ctx))


def _delta(q, k, v, g, nb, p_rows, c_rows):
    rows = q.shape[0]
    n_all = p_rows // CHUNK
    n_ctx = c_rows // CHUNK
    fwd = lambda w: pl.BlockSpec((CHUNK, w), lambda b, s: (b * n_all + s, 0))
    bwd = lambda w: pl.BlockSpec((CHUNK, w), lambda b, s: (b * n_all + _bwd_order(s, n_ctx, n_all), 0))
    return pl.pallas_call(
        _delta_kernel,
        grid=(nb, n_all),
        in_specs=[fwd(A_WIDTH)] * 3 + [fwd(LANES)] + [bwd(A_WIDTH)] * 3 + [bwd(LANES)],
        out_specs=[fwd(A_WIDTH), bwd(A_WIDTH)],
        out_shape=[jax.ShapeDtypeStruct((rows, A_WIDTH), F32)] * 2,
        scratch_shapes=[pltpu.VMEM((2, A_HEADS, A_DK, A_DK), F32)],
        compiler_params=_cparams(("arbitrary", "arbitrary")),
        name="delta_rule",
    )(q, k, v, g, q, k, v, g)


def _lru_kernel(a0_ref, b0_ref, a1_ref, b1_ref, hf_ref, hb_ref, carry_ref):
    @pl.when(pl.program_id(1) == 0)
    def _():
        carry_ref[...] = jnp.zeros_like(carry_ref)

    row = lax.broadcasted_iota(jnp.int32, (SUBLANES, B_WIDTH), 0)
    ngroups = TM // SUBLANES

    def scan_group(a_ref, b_ref, h_ref, r0, h_in, reverse):
        a = a_ref[pl.ds(r0, SUBLANES), :]
        b = b_ref[pl.ds(r0, SUBLANES), :]
        for sft in (1, 2, 4):
            shift = SUBLANES - sft if reverse else sft
            keep = (row < SUBLANES - sft) if reverse else (row >= sft)
            a_sh = pltpu.roll(a, shift, axis=0)
            b_sh = pltpu.roll(b, shift, axis=0)
            b = jnp.where(keep, a * b_sh + b, b)
            a = jnp.where(keep, a * a_sh, a)
        hrows = a * h_in + b
        h_ref[pl.ds(r0, SUBLANES), :] = hrows
        return hrows[0:1, :] if reverse else hrows[SUBLANES - 1:SUBLANES, :]

    def fwd_body(t, h_in):
        r0 = pl.multiple_of(t * SUBLANES, SUBLANES)
        return scan_group(a0_ref, b0_ref, hf_ref, r0, h_in, False)

    def bwd_body(t, h_in):
        r0 = pl.multiple_of((ngroups - 1 - t) * SUBLANES, SUBLANES)
        return scan_group(a1_ref, b1_ref, hb_ref, r0, h_in, True)

    carry_ref[0:1, :] = lax.fori_loop(0, ngroups, fwd_body, carry_ref[0:1, :])
    carry_ref[1:2, :] = lax.fori_loop(0, ngroups, bwd_body, carry_ref[1:2, :])


def _lru(a0, b0, a1, b1, nb, nt, nc):
    rows = a0.shape[0]
    fwd = pl.BlockSpec((TM, B_WIDTH), lambda b, s: (b * nt + s, 0))
    bwd = pl.BlockSpec((TM, B_WIDTH), lambda b, s: (b * nt + _bwd_order(s, nc, nt), 0))
    return pl.pallas_call(
        _lru_kernel,
        grid=(nb, nt),
        in_specs=[fwd, fwd, bwd, bwd],
        out_specs=[fwd, bwd],
        out_shape=[jax.ShapeDtypeStruct((rows, B_WIDTH), F32)] * 2,
        scratch_shapes=[pltpu.VMEM((SUBLANES, B_WIDTH), F32)],
        compiler_params=_cparams(("arbitrary", "arbitrary")),
        name="rg_lru_scan",
    )(a0, b0, a1, b1)


def _merge_ab_kernel(of_ref, ob_ref, hf_ref, hb_ref, go_ref, yb_ref, x_ref, mod_ref, on_ref, w_ref, xo_ref):
    parts = []
    for h in range(A_HEADS):
        sl = slice(h * A_DK, (h + 1) * A_DK)
        o = of_ref[:, sl] + ob_ref[:, sl]
        n = o * lax.rsqrt(jnp.mean(o * o, axis=-1, keepdims=True) + EPS) * on_ref[...]
        parts.append((n * _silu(go_ref[:, sl])).astype(BF16))
    parts.append(((hf_ref[...] + hb_ref[...]) * _gelu_tanh(yb_ref[...])).astype(BF16))
    cat = jnp.concatenate(parts, axis=-1)
    y = jnp.dot(cat, w_ref[...], preferred_element_type=F32)
    xo_ref[...] = x_ref[...] + mod_ref[2:3, :] * y


def _merge_ab(o_f, o_b, h_f, h_b, gout, yb, xu, mods, onorm, w_out, nb, nt, nc):
    rows = xu.shape[0]
    half = pl.BlockSpec((TM, A_WIDTH), lambda i: (i, 0))
    wide = pl.BlockSpec((TM, D_MODEL), lambda i: (i, 0))
    mrow = lambda i: jnp.where(i % nt < nc, nb, i // nt)
    return pl.pallas_call(
        _merge_ab_kernel,
        grid=(rows // TM,),
        in_specs=[half] * 6 + [
            wide,
            pl.BlockSpec((None, 6, D_MODEL), lambda i: (mrow(i), 0, 0)),
            pl.BlockSpec((1, A_DK), lambda i: (0, 0)),
            pl.BlockSpec((D_MODEL, D_MODEL), lambda i: (0, 0)),
        ],
        out_specs=wide,
        out_shape=jax.ShapeDtypeStruct((rows, D_MODEL), F32),
        compiler_params=_cparams(("arbitrary",)),
        name="merge_ab",
    )(o_f, o_b, h_f, h_b, gout, yb, xu, mods, onorm, w_out)


def _inproj_attn_kernel(x_ref, mod_ref, n1_ref, w_ref, qn_ref, kn_ref, cos_ref, sin_ref, q_ref, k_ref, v_ref):
    h = _modulate(x_ref[...], n1_ref[...], mod_ref[0:1, :], mod_ref[1:2, :]).astype(BF16)
    z = jnp.dot(h, w_ref[...], preferred_element_type=F32)
    cos = cos_ref[...]
    sin = sin_ref[...]
    for hd in range(C_HEADS + C_KV_HEADS):
        xh = z[:, hd * C_HD:(hd + 1) * C_HD]
        gain = qn_ref[...] if hd < C_HEADS else kn_ref[...]
        n = xh * lax.rsqrt(jnp.mean(xh * xh, axis=-1, keepdims=True) + EPS) * gain
        rot = n * cos + pltpu.roll(n, C_HD // 2, axis=1) * sin
        if hd < C_HEADS:
            q_ref[:, hd * C_HD:(hd + 1) * C_HD] = (rot * (C_HD ** -0.5)).astype(BF16)
        else:
            kh = hd - C_HEADS
            k_ref[:, kh * C_HD:(kh + 1) * C_HD] = rot.astype(BF16)
    v_ref[...] = z[:, (C_HEADS + C_KV_HEADS) * C_HD:].astype(BF16)


def _inproj_attn(xu, mods, norm1, w_perm, qn, kn, cos_tab, sin_tab, nb, nt, nc):
    rows = xu.shape[0]
    nqk = (C_HEADS + 2 * C_KV_HEADS) * C_HD
    mrow = lambda i: jnp.where(i % nt < nc, nb, i // nt)
    full = lambda shape: pl.BlockSpec(shape, lambda i: (0,) * len(shape))
    tab = pl.BlockSpec((TM, C_HD), lambda i: (i % nt, 0))
    return pl.pallas_call(
        _inproj_attn_kernel,
        grid=(rows // TM,),
        in_specs=[
            pl.BlockSpec((TM, D_MODEL), lambda i: (i, 0)),
            pl.BlockSpec((None, 6, D_MODEL), lambda i: (mrow(i), 0, 0)),
            full((1, D_MODEL)),
            full((D_MODEL, nqk)),
            full((1, C_HD)),
            full((1, C_HD)),
            tab, tab,
        ],
        out_specs=[
            pl.BlockSpec((TM, C_HEADS * C_HD), lambda i: (i, 0)),
            pl.BlockSpec((TM, C_KV_HEADS * C_HD), lambda i: (i, 0)),
            pl.BlockSpec((TM, C_KV_HEADS * C_HD), lambda i: (i, 0)),
        ],
        out_shape=[
            jax.ShapeDtypeStruct((rows, C_HEADS * C_HD), BF16),
            jax.ShapeDtypeStruct((rows, C_KV_HEADS * C_HD), BF16),
            jax.ShapeDtypeStruct((rows, C_KV_HEADS * C_HD), BF16),
        ],
        compiler_params=_cparams(("arbitrary",)),
        name="inproj_attn",
    )(xu, mods, norm1, w_perm, qn, kn, cos_tab, sin_tab)


def _attn_kernel(q_ref, k_ref, v_ref, o_ref):
    k = k_ref[...]
    v = v_ref[...]
    for g in range(C_GRP):
        sl = slice(g * C_HD, (g + 1) * C_HD)
        s = lax.dot_general(q_ref[:, sl], k, (((1,), (1,)), ((), ())), preferred_element_type=F32)
        p = jnp.exp(s - jnp.max(s, axis=-1, keepdims=True))
        denom = jnp.sum(p, axis=-1, keepdims=True)
        o = jnp.dot(p.astype(BF16), v, preferred_element_type=F32) / denom
        o_ref[:, sl] = o.astype(BF16)


def _attention(q, k, v, nb, nt, nc, p_rows):
    nq = nt - nc
    gw = C_GRP * C_HD
    return pl.pallas_call(
        _attn_kernel,
        grid=(nb, C_KV_HEADS, nq),
        in_specs=[
            pl.BlockSpec((TM, gw), lambda b, h, t: (b * nt + nc + t, h)),
            pl.BlockSpec((p_rows, C_HD), lambda b, h, t: (b, h)),
            pl.BlockSpec((p_rows, C_HD), lambda b, h, t: (b, h)),
        ],
        out_specs=pl.BlockSpec((TM, gw), lambda b, h, t: (b * nq + t, h)),
        out_shape=jax.ShapeDtypeStruct((nb * nq * TM, C_HEADS * C_HD), BF16),
        compiler_params=_cparams(("arbitrary", "arbitrary", "arbitrary")),
        name="gqa_attention",
    )(q, k, v)


def _outproj_kernel(a_ref, x_ref, mod_ref, w_ref, xo_ref):
    y = jnp.dot(a_ref[...], w_ref[...], preferred_element_type=F32)
    xo_ref[...] = x_ref[...] + mod_ref[2:3, :] * y


def _outproj_lat(a, xu, mods, w_out, nt, nc):
    rows = a.shape[0]
    nq = nt - nc
    return pl.pallas_call(
        _outproj_kernel,
        grid=(rows // TM,),
        in_specs=[
            pl.BlockSpec((TM, D_MODEL), lambda j: (j, 0)),
            pl.BlockSpec((TM, D_MODEL), lambda j: ((j // nq) * nt + nc + j % nq, 0)),
            pl.BlockSpec((None, 6, D_MODEL), lambda j: (j // nq, 0, 0)),
            pl.BlockSpec((D_MODEL, D_MODEL), lambda j: (0, 0)),
        ],
        out_specs=pl.BlockSpec((TM, D_MODEL), lambda j: (j, 0)),
        out_shape=jax.ShapeDtypeStruct((rows, D_MODEL), F32),
        compiler_params=_cparams(("arbitrary",)),
        name="outproj_attn",
    )(a, xu, mods, w_out)


def _route_kernel(x_ref, mod_ref, n2_ref, wr_ref, br_ref, h_ref, g_ref):
    h = _modulate(x_ref[...], n2_ref[...], mod_ref[3:4, :], mod_ref[4:5, :])
    h_ref[...] = h.astype(BF16)
    lg = jnp.dot(h, wr_ref[...], precision=HIGHEST, preferred_element_type=F32) + br_ref[...]
    lane = lax.broadcasted_iota(jnp.int32, lg.shape, 1)
    neg = jnp.float32(-jnp.inf)
    big = jnp.int32(1 << 20)
    is_grp = jnp.logical_and(lane >= N_EXPERTS, lane < N_EXPERTS + N_GROUPS)
    gl = jnp.where(is_grp, lg, neg)
    gmax = jnp.max(gl, axis=-1, keepdims=True)
    gidx = jnp.min(jnp.where(gl == gmax, lane - N_EXPERTS, big), axis=-1, keepdims=True)
    g_w = 1.0 / jnp.sum(jnp.where(is_grp, jnp.exp(gl - gmax), 0.0), axis=-1, keepdims=True)
    in_grp = jnp.logical_and(lane < N_EXPERTS, jnp.right_shift(lane, 3) == gidx)
    e1 = jnp.where(in_grp, lg, neg)
    m1 = jnp.max(e1, axis=-1, keepdims=True)
    i1 = jnp.min(jnp.where(e1 == m1, lane, big), axis=-1, keepdims=True)
    e2 = jnp.where(lane == i1, neg, e1)
    m2 = jnp.max(e2, axis=-1, keepdims=True)
    i2 = jnp.min(jnp.where(e2 == m2, lane, big), axis=-1, keepdims=True)
    t = jnp.exp(m2 - m1)
    w1 = g_w / (1.0 + t)
    w2 = g_w * t / (1.0 + t)
    g_ref[...] = jnp.where(lane == i1, w1, jnp.where(lane == i2, w2, 0.0))


def _route_call(x, mods, norm2, w_route, b_route, ntiles, xrow, mrow):
    return pl.pallas_call(
        _route_kernel,
        grid=(ntiles,),
        in_specs=[
            pl.BlockSpec((TM, D_MODEL), lambda i: (xrow(i), 0)),
            pl.BlockSpec((None, 6, D_MODEL), lambda i: (mrow(i), 0, 0)),
            pl.BlockSpec((1, D_MODEL), lambda i: (0, 0)),
            pl.BlockSpec((D_MODEL, LANES), lambda i: (0, 0)),
            pl.BlockSpec((1, LANES), lambda i: (0, 0)),
        ],
        out_specs=[
            pl.BlockSpec((TM, D_MODEL), lambda i: (i, 0)),
            pl.BlockSpec((TM, LANES), lambda i: (i, 0)),
        ],
        out_shape=[
            jax.ShapeDtypeStruct((ntiles * TM, D_MODEL), BF16),
            jax.ShapeDtypeStruct((ntiles * TM, LANES), F32),
        ],
        compiler_params=_cparams(("arbitrary",)),
        name="moe_route",
    )(x, mods, norm2, w_route, b_route)


def _moe_dense_kernel(h_ref, g_ref, wg_ref, wu_ref, wd_ref, y_ref, acc_ref):
    e = pl.program_id(1)

    @pl.when(e == 0)
    def _():
        acc_ref[...] = jnp.zeros_like(acc_ref)

    gates = g_ref[...]
    lane = lax.broadcasted_iota(jnp.int32, gates.shape, 1)
    gcol = jnp.sum(jnp.where(lane == e, gates, 0.0), axis=-1, keepdims=True)
    hh = h_ref[...]
    hg = jnp.dot(hh, wg_ref[...].astype(BF16), preferred_element_type=F32)
    hu = jnp.dot(hh, wu_ref[...].astype(BF16), preferred_element_type=F32)
    act = (_silu(hg) * hu * gcol).astype(BF16)
    acc_ref[...] += jnp.dot(act, wd_ref[...].astype(BF16), preferred_element_type=F32)

    @pl.when(e == N_EXPERTS - 1)
    def _():
        y_ref[...] = acc_ref[...]


def _moe_dense(h2, gates, w_gate, w_up, w_down, tmm):
    rows = h2.shape[0]
    return pl.pallas_call(
        _moe_dense_kernel,
        grid=(rows // tmm, N_EXPERTS),
        in_specs=[
            pl.BlockSpec((tmm, D_MODEL), lambda i, e: (i, 0)),
            pl.BlockSpec((tmm, LANES), lambda i, e: (i, 0)),
            pl.BlockSpec((None, D_MODEL, D_EXPERT), lambda i, e: (e, 0, 0)),
            pl.BlockSpec((None, D_MODEL, D_EXPERT), lambda i, e: (e, 0, 0)),
            pl.BlockSpec((None, D_EXPERT, D_MODEL), lambda i, e: (e, 0, 0)),
        ],
        out_specs=pl.BlockSpec((tmm, D_MODEL), lambda i, e: (i, 0)),
        out_shape=jax.ShapeDtypeStruct((rows, D_MODEL), F32),
        scratch_shapes=[pltpu.VMEM((tmm, D_MODEL), F32)],
        compiler_params=_cparams(("arbitrary", "arbitrary")),
        name="moe_experts_dense",
    )(h2, gates, w_gate, w_up, w_down)


def _residual_kernel(final, x_ref, y_ref, mod_ref, fn_ref, o_ref):
    x = x_ref[...] + mod_ref[5:6, :] * y_ref[...]
    if final:
        x = x * lax.rsqrt(jnp.mean(x * x, axis=-1, keepdims=True) + EPS) * fn_ref[...]
    o_ref[...] = x


def _residual(x, y, mods, final_norm, mrow, final):
    rows = x.shape[0]
    wide = pl.BlockSpec((TM, D_MODEL), lambda i: (i, 0))
    return pl.pallas_call(
        functools.partial(_residual_kernel, final),
        grid=(rows // TM,),
        in_specs=[
            wide, wide,
            pl.BlockSpec((None, 6, D_MODEL), lambda i: (mrow(i), 0, 0)),
            pl.BlockSpec((1, D_MODEL), lambda i: (0, 0)),
        ],
        out_specs=wide,
        out_shape=jax.ShapeDtypeStruct((rows, D_MODEL), F32),
        compiler_params=_cparams(("arbitrary",)),
        name="moe_residual",
    )(x, y, mods, final_norm)


def _moe_block(x, mods, norm2, w_grp, b_grp, w_exp, b_exp, w_gate, w_up, w_down, final_norm, mrow, final):
    rows = x.shape[0]
    w_route = jnp.zeros((D_MODEL, LANES), F32)
    w_route = w_route.at[:, :N_EXPERTS].set(w_exp).at[:, N_EXPERTS:N_EXPERTS + N_GROUPS].set(w_grp)
    b_route = jnp.zeros((1, LANES), F32)
    b_route = b_route.at[0, :N_EXPERTS].set(b_exp).at[0, N_EXPERTS:N_EXPERTS + N_GROUPS].set(b_grp)
    h2, gates = _route_call(x, mods, norm2, w_route, b_route, rows // TM, lambda i: i, mrow)
    tmm = 1024 if rows % 1024 == 0 else TM
    y = _moe_dense(h2, gates, w_gate, w_up, w_down, tmm)
    return _residual(x, y, mods, final_norm, mrow, final)


def _ab_params(w_in, conv_qkv, a_log, dt_bias, conv_x, rg_wr, rg_br, rg_wi, rg_bi, rg_lam):
    o1 = 3 * A_WIDTH
    o2 = 4 * A_WIDTH
    o3 = o2 + 4 * A_HEADS
    o4 = o3 + B_WIDTH
    gate_cols = jnp.zeros((D_MODEL, LANES), F32).at[:, :4 * A_HEADS].set(w_in[:, o2:o3])
    w_pad = jnp.concatenate([w_in[:, :o1], w_in[:, o3:o4], w_in[:, o1:o2], w_in[:, o4:], gate_cols],
                            axis=1).astype(BF16)
    alog_vec = jnp.zeros((1, LANES), F32)
    dt_vec = jnp.zeros((1, LANES), F32)
    for d in range(2):
        alog_vec = alog_vec.at[0, d * 8:d * 8 + A_HEADS].set(a_log[d])
        dt_vec = dt_vec.at[0, d * 8:d * 8 + A_HEADS].set(dt_bias[d])
    wbd = jnp.zeros((B_WIDTH, 4 * B_WIDTH), F32)
    for m, wsrc in enumerate((rg_wr, rg_wi)):
        for d in range(2):
            for g in range(B_BLOCKS):
                c0 = (2 * m + d) * B_WIDTH + g * B_BLK
                wbd = wbd.at[g * B_BLK:(g + 1) * B_BLK, c0:c0 + B_BLK].set(wsrc[d, g])
    bbd = jnp.concatenate([rg_br[0].reshape(-1), rg_br[1].reshape(-1),
                           rg_bi[0].reshape(-1), rg_bi[1].reshape(-1)]).reshape(1, 4 * B_WIDTH)
    lam = rg_lam.reshape(1, 2 * B_WIDTH)
    return w_pad, alog_vec, dt_vec, wbd.astype(BF16), bbd, lam


def _attn_params(w_qkv, q_norm, k_norm, t_lat, c_rows):
    half = C_HD // 2
    perm = jnp.concatenate([jnp.arange(half) * 2, jnp.arange(half) * 2 + 1])
    nrot = (C_HEADS + C_KV_HEADS) * C_HD
    cols = (jnp.arange(C_HEADS + C_KV_HEADS)[:, None] * C_HD + perm[None, :]).reshape(-1)
    cols = jnp.concatenate([cols, jnp.arange(nrot, w_qkv.shape[1])])
    w_perm = w_qkv[:, cols].astype(BF16)
    qn = q_norm[perm].reshape(1, C_HD)
    kn = k_norm[perm].reshape(1, C_HD)
    return w_perm, qn, kn


def _rope_tables(t_lat, c_rows, grid_w):
    rows = t_lat // grid_w
    row = jnp.repeat(jnp.arange(rows, dtype=F32), grid_w)
    col = jnp.tile(jnp.arange(grid_w, dtype=F32), rows)
    n_freq = C_HD // 4
    inv = ROPE_THETA ** (-jnp.arange(n_freq, dtype=F32) / n_freq)
    ang = jnp.concatenate([row[:, None] * inv, col[:, None] * inv], axis=-1)
    cos = jnp.cos(ang)
    sin = jnp.sin(ang)
    cos_tab = jnp.concatenate([jnp.ones((c_rows, C_HD), F32), jnp.concatenate([cos, cos], axis=-1)], axis=0)
    sin_tab = jnp.concatenate([jnp.zeros((c_rows, C_HD), F32), jnp.concatenate([-sin, sin], axis=-1)], axis=0)
    return cos_tab, sin_tab


GRID_W = 64


def kernel(x, c, ctx, c_ctx, ada_w, ada_b, norm1, norm2, final_norm, ab_w_in, ab_conv_qkv, ab_a_log, ab_dt_bias, ab_onorm, ab_conv_x, ab_rg_wr, ab_rg_br, ab_rg_wi, ab_rg_bi, ab_rg_lam, ab_w_out, at_w_qkv, at_q_norm, at_k_norm, at_w_out, moe_w_grp, moe_b_grp, moe_w_exp, moe_b_exp, moe_w_gate, moe_w_up, moe_w_down):
    nb, t_lat, _ = x.shape
    c_rows = ctx.shape[1]
    p_rows = c_rows + t_lat
    nt = p_rows // TM
    nc = c_rows // TM
    depth = ada_w.shape[0]
    assert depth == 2 and nb < 16 and c_rows % TM == 0 and t_lat % TM == 0

    cond = jnp.zeros((16, D_MODEL), F32).at[:nb].set(c).at[nb].set(c_ctx)
    mods = _ada_mod(cond, ada_w, ada_b)
    xu = jnp.concatenate([ctx, x], axis=1).reshape(nb * p_rows, D_MODEL)
    uni_mrow = lambda i: jnp.where(i % nt < nc, nb, i // nt)
    fnorm = final_norm.reshape(1, D_MODEL)

    w_pad, alog_vec, dt_vec, wbd, bbd, lam = _ab_params(
        ab_w_in[0], ab_conv_qkv[0], ab_a_log[0], ab_dt_bias[0], ab_conv_x[0],
        ab_rg_wr[0], ab_rg_br[0], ab_rg_wi[0], ab_rg_bi[0], ab_rg_lam[0])
    q, k, v, gout, yb, gates, a0, b0, a1, b1 = _inproj_ab(
        xu, mods[0], norm1[0].reshape(1, D_MODEL), w_pad, ab_conv_qkv[0], ab_conv_x[0],
        alog_vec, dt_vec, wbd, bbd, lam, nb, nt, nc)
    o_f, o_b = _delta(q, k, v, gates, nb, p_rows, c_rows)
    h_f, h_b = _lru(a0, b0, a1, b1, nb, nt, nc)
    xu = _merge_ab(o_f, o_b, h_f, h_b, gout, yb, xu, mods[0], ab_onorm[0].reshape(1, A_DK),
                   ab_w_out[0].astype(BF16), nb, nt, nc)
    xu = _moe_block(xu, mods[0], norm2[0].reshape(1, D_MODEL), moe_w_grp[0], moe_b_grp[0], moe_w_exp[0],
                    moe_b_exp[0], moe_w_gate[0], moe_w_up[0], moe_w_down[0], fnorm, uni_mrow, False)

    w_perm, qn, kn = _attn_params(at_w_qkv[0], at_q_norm[0], at_k_norm[0], t_lat, c_rows)
    cos_tab, sin_tab = _rope_tables(t_lat, c_rows, GRID_W)
    q, k, v = _inproj_attn(xu, mods[1], norm1[1].reshape(1, D_MODEL), w_perm, qn, kn, cos_tab, sin_tab,
                           nb, nt, nc)
    att = _attention(q, k, v, nb, nt, nc, p_rows)
    xl = _outproj_lat(att, xu, mods[1], at_w_out[0].astype(BF16), nt, nc)
    nq = nt - nc
    xl = _moe_block(xl, mods[1], norm2[1].reshape(1, D_MODEL), moe_w_grp[1], moe_b_grp[1], moe_w_exp[1],
                    moe_b_exp[1], moe_w_gate[1], moe_w_up[1], moe_w_down[1], fnorm, lambda j: j // nq, True)
    return xl.reshape(nb, t_lat, D_MODEL)
```

```python
import functools
import math

import jax
import jax.numpy as jnp
from jax import lax
from jax.experimental import pallas as pl
from jax.experimental.pallas import tpu as pltpu

F32 = jnp.float32
BF16 = jnp.bfloat16
HIGHEST = lax.Precision.HIGHEST

D_MODEL = 1024
EPS = 1e-6
TM = 256
LANES = 128
SUBLANES = 8

A_HEADS = 4
A_DK = 128
A_WIDTH = A_HEADS * A_DK
CHUNK = 64
CONV_W = 4
B_WIDTH = 512
B_BLOCKS = 8
B_BLK = B_WIDTH // B_BLOCKS
RG_C = 8.0

C_HEADS = 8
C_KV_HEADS = 2
C_HD = 128
C_GRP = C_HEADS // C_KV_HEADS
ROPE_THETA = 10000.0

N_GROUPS = 4
EXP_PER_GROUP = 8
N_EXPERTS = N_GROUPS * EXP_PER_GROUP
D_EXPERT = 512

VMEM_LIMIT = 56 * 1024 * 1024

ZC_QKV = 0
ZC_XB = 3 * A_WIDTH
ZC_GOUT = ZC_XB + B_WIDTH
ZC_YB = ZC_GOUT + A_WIDTH
ZC_GATE = ZC_YB + B_WIDTH
ZC_TOTAL = ZC_GATE + LANES
ZC_CONV = ZC_GOUT
HALO = SUBLANES


def _cparams(sem):
    return pltpu.CompilerParams(dimension_semantics=sem, vmem_limit_bytes=VMEM_LIMIT)


def _sigmoid(x):
    return jax.nn.sigmoid(x)


def _silu(x):
    return x * jax.nn.sigmoid(x)


def _softplus(x):
    return jnp.maximum(x, 0.0) + jnp.log1p(jnp.exp(-jnp.abs(x)))


def _gelu_tanh(x):
    c = math.sqrt(2.0 / math.pi)
    return 0.5 * x * (1.0 + jnp.tanh(c * (x + 0.044715 * (x * x * x))))


def _modulate(x, gain, shift, scale):
    y = x * lax.rsqrt(jnp.mean(x * x, axis=-1, keepdims=True) + EPS)
    return (y * gain) * (1.0 + scale) + shift


def _ada_kernel(cond_ref, w_ref, b_ref, o_ref):
    s = _silu(cond_ref[...]).astype(BF16)
    o_ref[...] = jnp.dot(s, w_ref[...].astype(BF16), preferred_element_type=F32) + b_ref[...]


def _ada_mod(cond, ada_w, ada_b):
    depth = ada_w.shape[0]
    tn = 1536
    nn = 6 * D_MODEL // tn
    out = pl.pallas_call(
        _ada_kernel,
        grid=(depth, nn),
        in_specs=[
            pl.BlockSpec((16, D_MODEL), lambda l, n: (0, 0)),
            pl.BlockSpec((None, D_MODEL, tn), lambda l, n: (l, 0, n)),
            pl.BlockSpec((None, 1, tn), lambda l, n: (l, 0, n)),
        ],
        out_specs=pl.BlockSpec((None, 16, tn), lambda l, n: (l, 0, n)),
        out_shape=jax.ShapeDtypeStruct((depth, 16, 6 * D_MODEL), F32),
        compiler_params=_cparams(("arbitrary", "arbitrary")),
        name="ada_mod",
    )(cond, ada_w, ada_b.reshape(depth, 1, 6 * D_MODEL))
    return out.reshape(depth, 16, 6, D_MODEL)


def _inproj_ab_kernel(nt, nc, xc_ref, xp_ref, xn_ref, mod_ref, n1_ref, w_ref, cq_ref, cx_ref,
                      alog_ref, dt_ref, wbd_ref, bbd_ref, lam_ref,
                      q_ref, k_ref, v_ref, go_ref, yb_ref, g_ref, a0_ref, b0_ref, a1_ref, b1_ref,
                      zbuf, xcbuf):
    i = pl.program_id(0)
    r = i % nt
    is_ctx = r < nc
    prev_ok = jnp.logical_and(r > 0, ((r - 1) < nc) == is_ctx)
    next_ok = jnp.logical_and(r < nt - 1, ((r + 1) < nc) == is_ctx)

    gain = n1_ref[...]
    shift = mod_ref[0:1, :]
    scale = mod_ref[1:2, :]
    xall = jnp.concatenate([xp_ref[...], xc_ref[...], xn_ref[...]], axis=0)
    h = _modulate(xall, gain, shift, scale).astype(BF16)
    zbuf[...] = jnp.dot(h, w_ref[...], preferred_element_type=F32)
    zbuf[0:HALO, 0:ZC_CONV] = jnp.where(prev_ok, zbuf[0:HALO, 0:ZC_CONV], 0.0)
    zbuf[HALO + TM:, 0:ZC_CONV] = jnp.where(next_ok, zbuf[HALO + TM:, 0:ZC_CONV], 0.0)

    def conv(c0, w_taps_ref, wc0):
        acc = None
        for j in range(CONV_W):
            start = HALO - CONV_W // 2 + j
            term = zbuf[start:start + TM, c0:c0 + LANES] * w_taps_ref[j:j + 1, wc0:wc0 + LANES]
            acc = term if acc is None else acc + term
        return acc

    outs = (q_ref, k_ref, v_ref)
    for s in range(3 * A_HEADS):
        y = _silu(conv(s * LANES, cq_ref, s * LANES))
        if s < 2 * A_HEADS:
            y = y * lax.rsqrt(jnp.sum(y * y, axis=-1, keepdims=True) + EPS)
        if s < A_HEADS:
            y = y * (A_DK ** -0.5)
        hh = s % A_HEADS
        outs[s // A_HEADS][:, hh * LANES:(hh + 1) * LANES] = y

    for s in range(B_WIDTH // LANES):
        xcbuf[:, s * LANES:(s + 1) * LANES] = conv(ZC_XB + s * LANES, cx_ref, s * LANES)

    go_ref[...] = zbuf[HALO:HALO + TM, ZC_GOUT:ZC_GOUT + A_WIDTH]
    yb_ref[...] = zbuf[HALO:HALO + TM, ZC_YB:ZC_YB + B_WIDTH]

    zg = zbuf[HALO:HALO + TM, ZC_GATE:ZC_GATE + LANES]
    lane = lax.broadcasted_iota(jnp.int32, (TM, LANES), 1)
    dec = -jnp.exp(alog_ref[...]) * _softplus(zg + dt_ref[...])
    g_ref[...] = jnp.where(jnp.bitwise_and(lane, 7) < A_HEADS, dec, _sigmoid(zg))

    xc = xcbuf[...]
    rg = jnp.dot(xc.astype(BF16), wbd_ref[...], preferred_element_type=F32) + bbd_ref[...]
    sp = _softplus(-lam_ref[...])
    for d, (a_ref, b_ref) in enumerate(((a0_ref, b0_ref), (a1_ref, b1_ref))):
        rr = _sigmoid(rg[:, d * B_WIDTH:(d + 1) * B_WIDTH])
        ii = _sigmoid(rg[:, (2 + d) * B_WIDTH:(3 + d) * B_WIDTH])
        a = jnp.exp(-RG_C * rr * sp[:, d * B_WIDTH:(d + 1) * B_WIDTH])
        a_ref[...] = a
        b_ref[...] = jnp.sqrt(1.0 - a * a) * ii * xc


def _inproj_ab(xu, mods, norm1, w_pad, conv_qkv, conv_x, alog_vec, dt_vec, wbd, bbd, lam, nb, nt, nc):
    rows = xu.shape[0]
    ntiles = rows // TM
    hb = TM // HALO
    nhalo = rows // HALO

    def mrow(i):
        return jnp.where(i % nt < nc, nb, i // nt)

    full = lambda shape: pl.BlockSpec(shape, lambda i: (0,) * len(shape))
    row_spec = lambda w: pl.BlockSpec((TM, w), lambda i: (i, 0))
    sds = lambda w: jax.ShapeDtypeStruct((rows, w), F32)
    return pl.pallas_call(
        functools.partial(_inproj_ab_kernel, nt, nc),
        grid=(ntiles,),
        in_specs=[
            row_spec(D_MODEL),
            pl.BlockSpec((HALO, D_MODEL), lambda i: (jnp.maximum(i * hb - 1, 0), 0)),
            pl.BlockSpec((HALO, D_MODEL), lambda i: (jnp.minimum((i + 1) * hb, nhalo - 1), 0)),
            pl.BlockSpec((None, 6, D_MODEL), lambda i: (mrow(i), 0, 0)),
            full((1, D_MODEL)),
            full((D_MODEL, ZC_TOTAL)),
            full((CONV_W, 3 * A_WIDTH)),
            full((CONV_W, B_WIDTH)),
            full((1, LANES)),
            full((1, LANES)),
            full((B_WIDTH, 4 * B_WIDTH)),
            full((1, 4 * B_WIDTH)),
            full((1, 2 * B_WIDTH)),
        ],
        out_specs=[row_spec(A_WIDTH)] * 5 + [row_spec(LANES)] + [row_spec(B_WIDTH)] * 4,
        out_shape=[sds(A_WIDTH)] * 5 + [sds(LANES)] + [sds(B_WIDTH)] * 4,
        scratch_shapes=[pltpu.VMEM((TM + 2 * HALO, ZC_TOTAL), F32), pltpu.VMEM((TM, B_WIDTH), F32)],
        compiler_params=_cparams(("arbitrary",)),
        name="inproj_ab",
    )(xu, xu, xu, mods, norm1, w_pad, conv_qkv, conv_x, alog_vec, dt_vec, wbd, bbd, lam)


PAIR = 2 * CHUNK
N_STREAMS = 4
PREP_CHUNKS = 4
PREP_PROBLEMS = PREP_CHUNKS * N_STREAMS
SCAN_BATCH = 2
DIAG_BLK = 8


def _delta_prep_kernel(q_ref, k_ref, v_ref, g_ref, u_ref, wq_ref, at_ref, kt_ref, aux_ref,
                       gcum_s, dec_s, kq_s, rhs_s, n_s, pwf_s, l_s):
    ii = lax.broadcasted_iota(jnp.int32, (PAIR, PAIR), 0)
    jj = lax.broadcasted_iota(jnp.int32, (PAIR, PAIR), 1)
    same = (ii < CHUNK) == (jj < CHUNK)
    ci = lax.broadcasted_iota(jnp.int32, (CHUNK, CHUNK), 0)
    cj = lax.broadcasted_iota(jnp.int32, (CHUNK, CHUNK), 1)
    aux_ref[...] = jnp.zeros_like(aux_ref)
    problems = [(c, d, p) for c in range(PREP_CHUNKS) for d in range(2) for p in range(2)]

    for c in range(PREP_CHUNKS):
        gates = g_ref[c * CHUNK:(c + 1) * CHUNK, :]
        for d in range(2):
            tri = (ci >= cj) if d == 0 else (ci <= cj)
            gcum_s[2 * c + d] = jnp.dot(tri.astype(F32), gates, precision=HIGHEST, preferred_element_type=F32)

    for n, (c, d, p) in enumerate(problems):
        x = 2 * d + p
        rows = slice(c * CHUNK, (c + 1) * CHUNK)
        heads = (2 * p, 2 * p + 1)
        lanes = [d * 2 * A_HEADS + h for h in heads]
        last = CHUNK - 1 if d == 0 else 0
        lower = jnp.logical_and(same, (ii >= jj) if d == 0 else (ii <= jj))
        stack = lambda ref: jnp.concatenate([ref[rows, h * A_DK:(h + 1) * A_DK] for h in heads], axis=0)
        gcum = gcum_s[2 * c + d]
        gates = g_ref[rows, :]
        gc_col = jnp.concatenate([gcum[:, l:l + 1] for l in lanes], axis=0)
        beta = jnp.concatenate([gates[:, l + A_HEADS:l + A_HEADS + 1] for l in lanes], axis=0)
        g_last = [gcum[last:last + 1, l:l + 1] for l in lanes]
        gl_col = jnp.concatenate([jnp.broadcast_to(g, (CHUNK, 1)) for g in g_last], axis=0)
        gc_mat = jnp.broadcast_to(gc_col, (PAIR, PAIR))
        dec_s[n] = jnp.where(lower, jnp.exp(jnp.where(lower, gc_mat - gc_mat.T, 0.0)), 0.0)
        eg = jnp.exp(gc_col)
        q = stack(q_ref)
        k = stack(k_ref)
        kb = k * beta
        kq_s[n] = lax.dot_general(jnp.concatenate([kb, q], axis=0).astype(BF16), k.astype(BF16),
                                  (((1,), (1,)), ((), ())), preferred_element_type=F32)
        rhs_s[n] = jnp.concatenate([stack(v_ref) * beta, kb * eg], axis=-1)
        wq_ref[x, (2 * c + 1) * PAIR:(2 * c + 2) * PAIR, :] = (q * eg).astype(BF16)
        kt_ref[x, c * PAIR:(c + 1) * PAIR, :] = (k * jnp.exp(gl_col - gc_col)).T.astype(BF16)
        aux_ref[c * SUBLANES + x:c * SUBLANES + x + 1, :] = jnp.concatenate(
            [jnp.broadcast_to(jnp.exp(g), (1, A_DK)) for g in g_last], axis=-1)

    blk = lambda s: jnp.right_shift(ii, s) == jnp.right_shift(jj, s)
    diag_sh = int(math.log2(DIAG_BLK))
    for n, (c, d, p) in enumerate(problems):
        strict = jnp.logical_and(same, (ii > jj) if d == 0 else (ii < jj))
        dec = dec_s[n]
        lmat = jnp.where(strict, kq_s[n, :PAIR] * dec, 0.0)
        l_s[n] = lmat
        neg_d = jnp.where(blk(diag_sh), -lmat, 0.0)
        n_s[n] = neg_d
        pwf_s[n] = neg_d
        at_ref[2 * d + p, c * PAIR:(c + 1) * PAIR, :] = (kq_s[n, PAIR:] * dec).astype(BF16)

    for _ in range(diag_sh - 1):
        for n in range(PREP_PROBLEMS):
            pw = pwf_s[n]
            pwf_s[n] = jnp.dot(pw, pw, precision=HIGHEST, preferred_element_type=F32)
        for n in range(PREP_PROBLEMS):
            nm = n_s[n]
            pw = pwf_s[n]
            n_s[n] = nm + pw + jnp.dot(nm, pw, precision=HIGHEST, preferred_element_type=F32)

    for s in range(diag_sh, int(math.log2(CHUNK))):
        off = jnp.logical_and(blk(s + 1), jnp.logical_not(blk(s)))
        for n in range(PREP_PROBLEMS):
            l_off = jnp.where(off, l_s[n], 0.0)
            pwf_s[n] = l_off + jnp.dot(l_off.astype(BF16), n_s[n].astype(BF16), preferred_element_type=F32)
        for n in range(PREP_PROBLEMS):
            nm = n_s[n]
            xm = pwf_s[n]
            n_s[n] = nm - xm - jnp.dot(nm.astype(BF16), xm.astype(BF16), preferred_element_type=F32)

    for n, (c, d, p) in enumerate(problems):
        x = 2 * d + p
        rhs = rhs_s[n]
        sol = rhs + jnp.dot(n_s[n].astype(BF16), rhs.astype(BF16), preferred_element_type=F32)
        u_ref[x, c * PAIR:(c + 1) * PAIR, :] = sol[:, :A_DK]
        wq_ref[x, 2 * c * PAIR:(2 * c + 1) * PAIR, :] = sol[:, A_DK:].astype(BF16)


def _delta_scan_kernel(*refs):
    ins = refs[:4 * N_STREAMS]
    auxf_ref, auxb_ref, of_ref, ob_ref, s_ref, r_s, vbd_s = refs[4 * N_STREAMS:]
    insts = [(bb, x) for bb in range(SCAN_BATCH) for x in range(N_STREAMS)]

    @pl.when(pl.program_id(1) == 0)
    def _():
        s_ref[...] = jnp.zeros_like(s_ref)

    zeros = jnp.zeros((CHUNK, A_DK), F32)
    for n, (bb, x) in enumerate(insts):
        wq_ref = ins[4 * x + 1]
        r_s[n] = jnp.dot(wq_ref[bb], s_ref[n].astype(BF16), preferred_element_type=F32)
    for n, (bb, x) in enumerate(insts):
        d, p = divmod(x, 2)
        u_ref, _, at_ref, _ = ins[4 * x:4 * x + 4]
        o_ref = of_ref if d == 0 else ob_ref
        vn_a = u_ref[bb, :CHUNK, :] - r_s[n, :CHUNK, :A_DK]
        vn_b = u_ref[bb, CHUNK:, :] - r_s[n, CHUNK:PAIR, A_DK:]
        av = jnp.dot(at_ref[bb], jnp.concatenate([vn_a, vn_b], axis=0).astype(BF16),
                     preferred_element_type=F32)
        o_ref[bb, :, 2 * p * A_DK:(2 * p + 1) * A_DK] = r_s[n, PAIR:PAIR + CHUNK, :A_DK] + av[:CHUNK]
        o_ref[bb, :, (2 * p + 1) * A_DK:(2 * p + 2) * A_DK] = r_s[n, PAIR + CHUNK:, A_DK:] + av[CHUNK:]
        vbd_s[n] = jnp.concatenate([jnp.concatenate([vn_a, zeros], axis=1),
                                    jnp.concatenate([zeros, vn_b], axis=1)], axis=0).astype(BF16)
    for n, (bb, x) in enumerate(insts):
        aux_ref = auxf_ref if x < 2 else auxb_ref
        kt_ref = ins[4 * x + 3]
        s_ref[n] = s_ref[n] * aux_ref[bb, x:x + 1, :] + jnp.dot(kt_ref[bb], vbd_s[n],
                                                               preferred_element_type=F32)


def _bwd_order(s, n_ctx, n_all):
    return jnp.where(s < n_ctx, n_ctx - 1 - s, n_all - 1 - (s - n_ctx))


def _delta(q, k, v, g, nb, p_rows, c_rows):
    rows = q.shape[0]
    nchunks = rows // CHUNK
    pr = PREP_CHUNKS * CHUNK
    row_spec = lambda w: pl.BlockSpec((pr, w), lambda i: (i, 0))
    stream_spec = lambda m: pl.BlockSpec((N_STREAMS, PREP_CHUNKS * m, A_DK), lambda i: (0, i, 0))
    stream_sds = lambda m, dt: jax.ShapeDtypeStruct((N_STREAMS, nchunks * m, A_DK), dt)
    u, wq, at, kt, aux = pl.pallas_call(
        _delta_prep_kernel,
        grid=(nchunks // PREP_CHUNKS,),
        in_specs=[row_spec(A_WIDTH)] * 3 + [row_spec(LANES)],
        out_specs=[stream_spec(PAIR), stream_spec(2 * PAIR), stream_spec(PAIR), stream_spec(PAIR),
                   pl.BlockSpec((PREP_CHUNKS * SUBLANES, 2 * A_DK), lambda i: (i, 0))],
        out_shape=[stream_sds(PAIR, F32), stream_sds(2 * PAIR, BF16), stream_sds(PAIR, BF16),
                   stream_sds(PAIR, BF16), jax.ShapeDtypeStruct((nchunks * SUBLANES, 2 * A_DK), F32)],
        scratch_shapes=[
            pltpu.VMEM((2 * PREP_CHUNKS, CHUNK, LANES), F32),
            pltpu.VMEM((PREP_PROBLEMS, PAIR, PAIR), F32),
            pltpu.VMEM((PREP_PROBLEMS, 2 * PAIR, PAIR), F32),
            pltpu.VMEM((PREP_PROBLEMS, PAIR, 2 * A_DK), F32),
            pltpu.VMEM((PREP_PROBLEMS, PAIR, PAIR), F32),
            pltpu.VMEM((PREP_PROBLEMS, PAIR, PAIR), F32),
            pltpu.VMEM((PREP_PROBLEMS, PAIR, PAIR), F32),
        ],
        compiler_params=_cparams(("arbitrary",)),
        name="delta_prep",
    )(q, k, v, g)

    assert nb % SCAN_BATCH == 0
    ng = nb // SCAN_BATCH
    n_all = p_rows // CHUNK
    n_ctx = c_rows // CHUNK
    order = (lambda s: s, lambda s: _bwd_order(s, n_ctx, n_all))
    view = lambda arr, m: arr.reshape(N_STREAMS, ng, SCAN_BATCH, n_all * m, A_DK)
    in_specs, args = [], []
    for x in range(N_STREAMS):
        pos = order[x // 2]
        for arr, m in ((u, PAIR), (wq, 2 * PAIR), (at, PAIR), (kt, PAIR)):
            in_specs.append(pl.BlockSpec((None, None, SCAN_BATCH, m, A_DK),
                                         lambda b, s, x=x, pos=pos: (x, b, 0, pos(s), 0)))
            args.append(view(arr, m))
    for pos in order:
        in_specs.append(pl.BlockSpec((None, SCAN_BATCH, SUBLANES, 2 * A_DK), lambda b, s, pos=pos: (b, 0, pos(s), 0)))
        args.append(aux.reshape(ng, SCAN_BATCH, n_all * SUBLANES, 2 * A_DK))
    out_spec = lambda pos: pl.BlockSpec((None, SCAN_BATCH, CHUNK, A_WIDTH), lambda b, s: (b, 0, pos(s), 0))
    n_inst = SCAN_BATCH * N_STREAMS
    o_f, o_b = pl.pallas_call(
        _delta_scan_kernel,
        grid=(ng, n_all),
        in_specs=in_specs,
        out_specs=[out_spec(order[0]), out_spec(order[1])],
        out_shape=[jax.ShapeDtypeStruct((ng, SCAN_BATCH, p_rows, A_WIDTH), F32)] * 2,
        scratch_shapes=[
            pltpu.VMEM((n_inst, A_DK, 2 * A_DK), F32),
            pltpu.VMEM((n_inst, 2 * PAIR, 2 * A_DK), F32),
            pltpu.VMEM((n_inst, PAIR, 2 * A_DK), BF16),
        ],
        compiler_params=_cparams(("arbitrary", "arbitrary")),
        name="delta_scan",
    )(*args)
    return o_f.reshape(rows, A_WIDTH), o_b.reshape(rows, A_WIDTH)


def _lru_kernel(a0_ref, b0_ref, a1_ref, b1_ref, hf_ref, hb_ref, carry_ref):
    @pl.when(pl.program_id(1) == 0)
    def _():
        carry_ref[...] = jnp.zeros_like(carry_ref)

    row = lax.broadcasted_iota(jnp.int32, (SUBLANES, B_WIDTH), 0)
    ngroups = TM // SUBLANES

    def scan_group(a_ref, b_ref, h_ref, r0, h_in, reverse):
        a = a_ref[pl.ds(r0, SUBLANES), :]
        b = b_ref[pl.ds(r0, SUBLANES), :]
        for sft in (1, 2, 4):
            shift = SUBLANES - sft if reverse else sft
            keep = (row < SUBLANES - sft) if reverse else (row >= sft)
            a_sh = pltpu.roll(a, shift, axis=0)
            b_sh = pltpu.roll(b, shift, axis=0)
            b = jnp.where(keep, a * b_sh + b, b)
            a = jnp.where(keep, a * a_sh, a)
        hrows = a * h_in + b
        h_ref[pl.ds(r0, SUBLANES), :] = hrows
        return hrows[0:1, :] if reverse else hrows[SUBLANES - 1:SUBLANES, :]

    def fwd_body(t, h_in):
        r0 = pl.multiple_of(t * SUBLANES, SUBLANES)
        return scan_group(a0_ref, b0_ref, hf_ref, r0, h_in, False)

    def bwd_body(t, h_in):
        r0 = pl.multiple_of((ngroups - 1 - t) * SUBLANES, SUBLANES)
        return scan_group(a1_ref, b1_ref, hb_ref, r0, h_in, True)

    carry_ref[0:1, :] = lax.fori_loop(0, ngroups, fwd_body, carry_ref[0:1, :])
    carry_ref[1:2, :] = lax.fori_loop(0, ngroups, bwd_body, carry_ref[1:2, :])


def _lru(a0, b0, a1, b1, nb, nt, nc):
    rows = a0.shape[0]
    fwd = pl.BlockSpec((TM, B_WIDTH), lambda b, s: (b * nt + s, 0))
    bwd = pl.BlockSpec((TM, B_WIDTH), lambda b, s: (b * nt + _bwd_order(s, nc, nt), 0))
    return pl.pallas_call(
        _lru_kernel,
        grid=(nb, nt),
        in_specs=[fwd, fwd, bwd, bwd],
        out_specs=[fwd, bwd],
        out_shape=[jax.ShapeDtypeStruct((rows, B_WIDTH), F32)] * 2,
        scratch_shapes=[pltpu.VMEM((SUBLANES, B_WIDTH), F32)],
        compiler_params=_cparams(("arbitrary", "arbitrary")),
        name="rg_lru_scan",
    )(a0, b0, a1, b1)


def _merge_ab_kernel(of_ref, ob_ref, hf_ref, hb_ref, go_ref, yb_ref, x_ref, mod_ref, on_ref, w_ref, xo_ref):
    parts = []
    for h in range(A_HEADS):
        sl = slice(h * A_DK, (h + 1) * A_DK)
        o = of_ref[:, sl] + ob_ref[:, sl]
        n = o * lax.rsqrt(jnp.mean(o * o, axis=-1, keepdims=True) + EPS) * on_ref[...]
        parts.append((n * _silu(go_ref[:, sl])).astype(BF16))
    parts.append(((hf_ref[...] + hb_ref[...]) * _gelu_tanh(yb_ref[...])).astype(BF16))
    cat = jnp.concatenate(parts, axis=-1)
    y = jnp.dot(cat, w_ref[...], preferred_element_type=F32)
    xo_ref[...] = x_ref[...] + mod_ref[2:3, :] * y


def _merge_ab(o_f, o_b, h_f, h_b, gout, yb, xu, mods, onorm, w_out, nb, nt, nc):
    rows = xu.shape[0]
    half = pl.BlockSpec((TM, A_WIDTH), lambda i: (i, 0))
    wide = pl.BlockSpec((TM, D_MODEL), lambda i: (i, 0))
    mrow = lambda i: jnp.where(i % nt < nc, nb, i // nt)
    return pl.pallas_call(
        _merge_ab_kernel,
        grid=(rows // TM,),
        in_specs=[half] * 6 + [
            wide,
            pl.BlockSpec((None, 6, D_MODEL), lambda i: (mrow(i), 0, 0)),
            pl.BlockSpec((1, A_DK), lambda i: (0, 0)),
            pl.BlockSpec((D_MODEL, D_MODEL), lambda i: (0, 0)),
        ],
        out_specs=wide,
        out_shape=jax.ShapeDtypeStruct((rows, D_MODEL), F32),
        compiler_params=_cparams(("arbitrary",)),
        name="merge_ab",
    )(o_f, o_b, h_f, h_b, gout, yb, xu, mods, onorm, w_out)


def _inproj_attn_kernel(x_ref, mod_ref, n1_ref, w_ref, qn_ref, kn_ref, cos_ref, sin_ref, q_ref, k_ref, v_ref):
    h = _modulate(x_ref[...], n1_ref[...], mod_ref[0:1, :], mod_ref[1:2, :]).astype(BF16)
    z = jnp.dot(h, w_ref[...], preferred_element_type=F32)
    cos = cos_ref[...]
    sin = sin_ref[...]
    for hd in range(C_HEADS + C_KV_HEADS):
        xh = z[:, hd * C_HD:(hd + 1) * C_HD]
        gain = qn_ref[...] if hd < C_HEADS else kn_ref[...]
        n = xh * lax.rsqrt(jnp.mean(xh * xh, axis=-1, keepdims=True) + EPS) * gain
        rot = n * cos + pltpu.roll(n, C_HD // 2, axis=1) * sin
        if hd < C_HEADS:
            q_ref[:, hd * C_HD:(hd + 1) * C_HD] = (rot * (C_HD ** -0.5)).astype(BF16)
        else:
            kh = hd - C_HEADS
            k_ref[:, kh * C_HD:(kh + 1) * C_HD] = rot.astype(BF16)
    v_ref[...] = z[:, (C_HEADS + C_KV_HEADS) * C_HD:].astype(BF16)


def _inproj_attn(xu, mods, norm1, w_perm, qn, kn, cos_tab, sin_tab, nb, nt, nc):
    rows = xu.shape[0]
    nqk = (C_HEADS + 2 * C_KV_HEADS) * C_HD
    mrow = lambda i: jnp.where(i % nt < nc, nb, i // nt)
    full = lambda shape: pl.BlockSpec(shape, lambda i: (0,) * len(shape))
    tab = pl.BlockSpec((TM, C_HD), lambda i: (i % nt, 0))
    return pl.pallas_call(
        _inproj_attn_kernel,
        grid=(rows // TM,),
        in_specs=[
            pl.BlockSpec((TM, D_MODEL), lambda i: (i, 0)),
            pl.BlockSpec((None, 6, D_MODEL), lambda i: (mrow(i), 0, 0)),
            full((1, D_MODEL)),
            full((D_MODEL, nqk)),
            full((1, C_HD)),
            full((1, C_HD)),
            tab, tab,
        ],
        out_specs=[
            pl.BlockSpec((TM, C_HEADS * C_HD), lambda i: (i, 0)),
            pl.BlockSpec((TM, C_KV_HEADS * C_HD), lambda i: (i, 0)),
            pl.BlockSpec((TM, C_KV_HEADS * C_HD), lambda i: (i, 0)),
        ],
        out_shape=[
            jax.ShapeDtypeStruct((rows, C_HEADS * C_HD), BF16),
            jax.ShapeDtypeStruct((rows, C_KV_HEADS * C_HD), BF16),
            jax.ShapeDtypeStruct((rows, C_KV_HEADS * C_HD), BF16),
        ],
        compiler_params=_cparams(("arbitrary",)),
        name="inproj_attn",
    )(xu, mods, norm1, w_perm, qn, kn, cos_tab, sin_tab)


def _attn_kernel(q_ref, k_ref, v_ref, o_ref):
    k = k_ref[...]
    v = v_ref[...]
    for g in range(C_GRP):
        sl = slice(g * C_HD, (g + 1) * C_HD)
        s = lax.dot_general(q_ref[:, sl], k, (((1,), (1,)), ((), ())), preferred_element_type=F32)
        p = jnp.exp(s - jnp.max(s, axis=-1, keepdims=True))
        denom = jnp.sum(p, axis=-1, keepdims=True)
        o = jnp.dot(p.astype(BF16), v, preferred_element_type=F32) / denom
        o_ref[:, sl] = o.astype(BF16)


def _attention(q, k, v, nb, nt, nc, p_rows):
    nq = nt - nc
    gw = C_GRP * C_HD
    return pl.pallas_call(
        _attn_kernel,
        grid=(nb, C_KV_HEADS, nq),
        in_specs=[
            pl.BlockSpec((TM, gw), lambda b, h, t: (b * nt + nc + t, h)),
            pl.BlockSpec((p_rows, C_HD), lambda b, h, t: (b, h)),
            pl.BlockSpec((p_rows, C_HD), lambda b, h, t: (b, h)),
        ],
        out_specs=pl.BlockSpec((TM, gw), lambda b, h, t: (b * nq + t, h)),
        out_shape=jax.ShapeDtypeStruct((nb * nq * TM, C_HEADS * C_HD), BF16),
        compiler_params=_cparams(("arbitrary", "arbitrary", "arbitrary")),
        name="gqa_attention",
    )(q, k, v)


def _outproj_kernel(a_ref, x_ref, mod_ref, w_ref, xo_ref):
    y = jnp.dot(a_ref[...], w_ref[...], preferred_element_type=F32)
    xo_ref[...] = x_ref[...] + mod_ref[2:3, :] * y


def _outproj_lat(a, xu, mods, w_out, nt, nc):
    rows = a.shape[0]
    nq = nt - nc
    return pl.pallas_call(
        _outproj_kernel,
        grid=(rows // TM,),
        in_specs=[
            pl.BlockSpec((TM, D_MODEL), lambda j: (j, 0)),
            pl.BlockSpec((TM, D_MODEL), lambda j: ((j // nq) * nt + nc + j % nq, 0)),
            pl.BlockSpec((None, 6, D_MODEL), lambda j: (j // nq, 0, 0)),
            pl.BlockSpec((D_MODEL, D_MODEL), lambda j: (0, 0)),
        ],
        out_specs=pl.BlockSpec((TM, D_MODEL), lambda j: (j, 0)),
        out_shape=jax.ShapeDtypeStruct((rows, D_MODEL), F32),
        compiler_params=_cparams(("arbitrary",)),
        name="outproj_attn",
    )(a, xu, mods, w_out)


TE = 256
SEL_E1, SEL_E2, SEL_W1, SEL_W2 = 0, 1, 2, 3


def _route_kernel(x_ref, mod_ref, n2_ref, wr_ref, br_ref, h_ref, sel_ref):
    h = _modulate(x_ref[...], n2_ref[...], mod_ref[3:4, :], mod_ref[4:5, :])
    h_ref[...] = h
    lg = jnp.dot(h, wr_ref[...], precision=HIGHEST, preferred_element_type=F32) + br_ref[...]
    lane = lax.broadcasted_iota(jnp.int32, lg.shape, 1)
    neg = jnp.float32(-jnp.inf)
    big = jnp.int32(1 << 20)
    is_grp = jnp.logical_and(lane >= N_EXPERTS, lane < N_EXPERTS + N_GROUPS)
    gl = jnp.where(is_grp, lg, neg)
    gmax = jnp.max(gl, axis=-1, keepdims=True)
    gidx = jnp.min(jnp.where(gl == gmax, lane - N_EXPERTS, big), axis=-1, keepdims=True)
    g_w = 1.0 / jnp.sum(jnp.where(is_grp, jnp.exp(gl - gmax), 0.0), axis=-1, keepdims=True)
    in_grp = jnp.logical_and(lane < N_EXPERTS, jnp.right_shift(lane, 3) == gidx)
    e1 = jnp.where(in_grp, lg, neg)
    m1 = jnp.max(e1, axis=-1, keepdims=True)
    i1 = jnp.min(jnp.where(e1 == m1, lane, big), axis=-1, keepdims=True)
    e2 = jnp.where(lane == i1, neg, e1)
    m2 = jnp.max(e2, axis=-1, keepdims=True)
    i2 = jnp.min(jnp.where(e2 == m2, lane, big), axis=-1, keepdims=True)
    t = jnp.exp(m2 - m1)
    w1 = g_w / (1.0 + t)
    w2 = g_w * t / (1.0 + t)
    sel_ref[...] = jnp.where(lane == SEL_E1, i1.astype(F32),
                             jnp.where(lane == SEL_E2, i2.astype(F32),
                                       jnp.where(lane == SEL_W1, w1, jnp.where(lane == SEL_W2, w2, 0.0))))


def _route_call(x, mods, norm2, w_route, b_route, mrow):
    rows = x.shape[0]
    return pl.pallas_call(
        _route_kernel,
        grid=(rows // TM,),
        in_specs=[
            pl.BlockSpec((TM, D_MODEL), lambda i: (i, 0)),
            pl.BlockSpec((None, 6, D_MODEL), lambda i: (mrow(i), 0, 0)),
            pl.BlockSpec((1, D_MODEL), lambda i: (0, 0)),
            pl.BlockSpec((D_MODEL, LANES), lambda i: (0, 0)),
            pl.BlockSpec((1, LANES), lambda i: (0, 0)),
        ],
        out_specs=[
            pl.BlockSpec((TM, D_MODEL), lambda i: (i, 0)),
            pl.BlockSpec((TM, LANES), lambda i: (i, 0)),
        ],
        out_shape=[
            jax.ShapeDtypeStruct((rows, D_MODEL), F32),
            jax.ShapeDtypeStruct((rows, LANES), F32),
        ],
        compiler_params=_cparams(("arbitrary",)),
        name="moe_route",
    )(x, mods, norm2, w_route, b_route)


def _moe_plan(sel, n_tok):
    n_asg = 2 * n_tok
    max_tiles = n_asg // TE + N_EXPERTS
    flat_e = sel[:, SEL_E1:SEL_E2 + 1].astype(jnp.int32).reshape(n_asg)
    onehot = (flat_e[:, None] == jnp.arange(N_EXPERTS, dtype=jnp.int32)[None, :]).astype(jnp.int32)
    cum = jnp.cumsum(onehot, axis=0)
    counts = cum[-1]
    tiles_e = (counts + TE - 1) // TE
    cum_tiles = jnp.cumsum(tiles_e)
    n_tiles = cum_tiles[-1]
    off = (cum_tiles - tiles_e) * TE
    start = jnp.cumsum(counts) - counts
    rank = jnp.take_along_axis(cum, flat_e[:, None], axis=1)[:, 0] - 1
    pos = off[flat_e] + rank
    order = jnp.argsort(flat_e, stable=True).astype(jnp.int32)
    last_e = jnp.max(jnp.where(tiles_e > 0, jnp.arange(N_EXPERTS), 0))
    t_idx = jnp.arange(max_tiles, dtype=jnp.int32)
    tile_e = jnp.minimum(jnp.sum((cum_tiles[None, :] <= t_idx[:, None]).astype(jnp.int32), axis=1), last_e)
    tile_e = tile_e.astype(jnp.int32)
    drow = jnp.arange(max_tiles * TE, dtype=jnp.int32)
    e_d = tile_e[drow // TE]
    r_d = drow - off[e_d]
    valid = jnp.logical_and(r_d < counts[e_d], drow < n_tiles * TE)
    src = jnp.where(valid, order[jnp.clip(start[e_d] + r_d, 0, n_asg - 1)] // 2, 0).astype(jnp.int32)
    return tile_e, n_tiles.reshape(1).astype(jnp.int32), src, pos.astype(jnp.int32)


def _moe_experts_kernel(te_ref, nt_ref, src_ref, h_hbm, wg_ref, wu_ref, wd_ref, y_ref,
                        buf, wg16, wu16, wd16, sem):
    t = pl.program_id(0)
    n_tiles = nt_ref[0]

    def row_copy(tok, slot, r):
        return pltpu.make_async_copy(h_hbm.at[pl.ds(tok, 1), :], buf.at[slot, pl.ds(r, 1), :], sem.at[slot])

    def issue(tile, slot):
        base = tile * TE

        def body(r, carry):
            row_copy(src_ref[base + r], slot, r).start()
            return carry

        lax.fori_loop(0, TE, body, 0, unroll=8)

    @pl.when(t == 0)
    def _():
        issue(0, 0)

    @pl.when(t + 1 < n_tiles)
    def _():
        issue(t + 1, (t + 1) % 2)

    changed = jnp.logical_or(t == 0, te_ref[t] != te_ref[jnp.maximum(t - 1, 0)])

    @pl.when(jnp.logical_and(t < n_tiles, changed))
    def _():
        wg16[...] = wg_ref[...].astype(BF16)
        wu16[...] = wu_ref[...].astype(BF16)
        wd16[...] = wd_ref[...].astype(BF16)

    @pl.when(t < n_tiles)
    def _():
        slot = t % 2
        pltpu.make_async_copy(h_hbm.at[pl.ds(0, TE), :], buf.at[slot], sem.at[slot]).wait()
        x = buf[slot].astype(BF16)
        hg = jnp.dot(x, wg16[...], preferred_element_type=F32)
        hu = jnp.dot(x, wu16[...], preferred_element_type=F32)
        act = (_silu(hg) * hu).astype(BF16)
        y_ref[...] = jnp.dot(act, wd16[...], preferred_element_type=F32)

    @pl.when(t >= n_tiles)
    def _():
        y_ref[...] = jnp.zeros_like(y_ref)


def _moe_experts(h2, tile_e, n_tiles, src, w_gate, w_up, w_down):
    max_tiles = tile_e.shape[0]
    wspec = lambda a, b: pl.BlockSpec((None, a, b), lambda t, te, nt, sr: (te[t], 0, 0))
    return pl.pallas_call(
        _moe_experts_kernel,
        grid_spec=pltpu.PrefetchScalarGridSpec(
            num_scalar_prefetch=3,
            grid=(max_tiles,),
            in_specs=[
                pl.BlockSpec(memory_space=pl.ANY),
                wspec(D_MODEL, D_EXPERT), wspec(D_MODEL, D_EXPERT), wspec(D_EXPERT, D_MODEL),
            ],
            out_specs=pl.BlockSpec((TE, D_MODEL), lambda t, te, nt, sr: (t, 0)),
            scratch_shapes=[
                pltpu.VMEM((2, TE, D_MODEL), F32),
                pltpu.VMEM((D_MODEL, D_EXPERT), BF16),
                pltpu.VMEM((D_MODEL, D_EXPERT), BF16),
                pltpu.VMEM((D_EXPERT, D_MODEL), BF16),
                pltpu.SemaphoreType.DMA((2,)),
            ],
        ),
        out_shape=jax.ShapeDtypeStruct((max_tiles * TE, D_MODEL), F32),
        compiler_params=_cparams(("arbitrary",)),
        name="moe_experts",
    )(tile_e, n_tiles, src, h2, w_gate, w_up, w_down)


def _moe_combine_kernel(final, pos_ref, y_hbm, x_ref, sel_ref, mod_ref, fn_ref, o_ref, buf, sem):
    i = pl.program_id(0)
    n = pl.num_programs(0)

    def row_copy(p, slot, k, r):
        return pltpu.make_async_copy(y_hbm.at[pl.ds(p, 1), :], buf.at[slot, k, pl.ds(r, 1), :], sem.at[slot])

    def issue(tile, slot):
        base = tile * (2 * TM)

        def body(r, carry):
            row_copy(pos_ref[base + 2 * r], slot, 0, r).start()
            row_copy(pos_ref[base + 2 * r + 1], slot, 1, r).start()
            return carry

        lax.fori_loop(0, TM, body, 0, unroll=8)

    @pl.when(i == 0)
    def _():
        issue(0, 0)

    @pl.when(i + 1 < n)
    def _():
        issue(i + 1, (i + 1) % 2)

    slot = i % 2
    for k in range(2):
        pltpu.make_async_copy(y_hbm.at[pl.ds(0, TM), :], buf.at[slot, k], sem.at[slot]).wait()
    sel = sel_ref[...]
    y = sel[:, SEL_W1:SEL_W1 + 1] * buf[slot, 0] + sel[:, SEL_W2:SEL_W2 + 1] * buf[slot, 1]
    x = x_ref[...] + mod_ref[5:6, :] * y
    if final:
        x = x * lax.rsqrt(jnp.mean(x * x, axis=-1, keepdims=True) + EPS) * fn_ref[...]
    o_ref[...] = x


def _moe_combine(y_sorted, pos, x, sel, mods, final_norm, mrow, final):
    rows = x.shape[0]
    wide = pl.BlockSpec((TM, D_MODEL), lambda i, ps: (i, 0))
    return pl.pallas_call(
        functools.partial(_moe_combine_kernel, final),
        grid_spec=pltpu.PrefetchScalarGridSpec(
            num_scalar_prefetch=1,
            grid=(rows // TM,),
            in_specs=[
                pl.BlockSpec(memory_space=pl.ANY),
                wide,
                pl.BlockSpec((TM, LANES), lambda i, ps: (i, 0)),
                pl.BlockSpec((None, 6, D_MODEL), lambda i, ps: (mrow(i), 0, 0)),
                pl.BlockSpec((1, D_MODEL), lambda i, ps: (0, 0)),
            ],
            out_specs=wide,
            scratch_shapes=[pltpu.VMEM((2, 2, TM, D_MODEL), F32), pltpu.SemaphoreType.DMA((2,))],
        ),
        out_shape=jax.ShapeDtypeStruct((rows, D_MODEL), F32),
        compiler_params=_cparams(("arbitrary",)),
        name="moe_combine",
    )(pos, y_sorted, x, sel, mods, final_norm)


def _moe_block(x, mods, norm2, w_grp, b_grp, w_exp, b_exp, w_gate, w_up, w_down, final_norm, mrow, final):
    w_route = jnp.zeros((D_MODEL, LANES), F32)
    w_route = w_route.at[:, :N_EXPERTS].set(w_exp).at[:, N_EXPERTS:N_EXPERTS + N_GROUPS].set(w_grp)
    b_route = jnp.zeros((1, LANES), F32)
    b_route = b_route.at[0, :N_EXPERTS].set(b_exp).at[0, N_EXPERTS:N_EXPERTS + N_GROUPS].set(b_grp)
    h2, sel = _route_call(x, mods, norm2, w_route, b_route, mrow)
    tile_e, n_tiles, src, pos = _moe_plan(sel, x.shape[0])
    y_sorted = _moe_experts(h2, tile_e, n_tiles, src, w_gate, w_up, w_down)
    return _moe_combine(y_sorted, pos, x, sel, mods, final_norm, mrow, final)


def _ab_params(w_in, conv_qkv, a_log, dt_bias, conv_x, rg_wr, rg_br, rg_wi, rg_bi, rg_lam):
    o1 = 3 * A_WIDTH
    o2 = 4 * A_WIDTH
    o3 = o2 + 4 * A_HEADS
    o4 = o3 + B_WIDTH
    gate_cols = jnp.zeros((D_MODEL, LANES), F32).at[:, :4 * A_HEADS].set(w_in[:, o2:o3])
    w_pad = jnp.concatenate([w_in[:, :o1], w_in[:, o3:o4], w_in[:, o1:o2], w_in[:, o4:], gate_cols],
                            axis=1).astype(BF16)
    alog_vec = jnp.zeros((1, LANES), F32)
    dt_vec = jnp.zeros((1, LANES), F32)
    for d in range(2):
        alog_vec = alog_vec.at[0, d * 8:d * 8 + A_HEADS].set(a_log[d])
        dt_vec = dt_vec.at[0, d * 8:d * 8 + A_HEADS].set(dt_bias[d])
    wbd = jnp.zeros((B_WIDTH, 4 * B_WIDTH), F32)
    for m, wsrc in enumerate((rg_wr, rg_wi)):
        for d in range(2):
            for g in range(B_BLOCKS):
                c0 = (2 * m + d) * B_WIDTH + g * B_BLK
                wbd = wbd.at[g * B_BLK:(g + 1) * B_BLK, c0:c0 + B_BLK].set(wsrc[d, g])
    bbd = jnp.concatenate([rg_br[0].reshape(-1), rg_br[1].reshape(-1),
                           rg_bi[0].reshape(-1), rg_bi[1].reshape(-1)]).reshape(1, 4 * B_WIDTH)
    lam = rg_lam.reshape(1, 2 * B_WIDTH)
    return w_pad, alog_vec, dt_vec, wbd.astype(BF16), bbd, lam


def _attn_params(w_qkv, q_norm, k_norm, t_lat, c_rows):
    half = C_HD // 2
    perm = jnp.concatenate([jnp.arange(half) * 2, jnp.arange(half) * 2 + 1])
    nrot = (C_HEADS + C_KV_HEADS) * C_HD
    cols = (jnp.arange(C_HEADS + C_KV_HEADS)[:, None] * C_HD + perm[None, :]).reshape(-1)
    cols = jnp.concatenate([cols, jnp.arange(nrot, w_qkv.shape[1])])
    w_perm = w_qkv[:, cols].astype(BF16)
    qn = q_norm[perm].reshape(1, C_HD)
    kn = k_norm[perm].reshape(1, C_HD)
    return w_perm, qn, kn


def _rope_tables(t_lat, c_rows, grid_w):
    rows = t_lat // grid_w
    row = jnp.repeat(jnp.arange(rows, dtype=F32), grid_w)
    col = jnp.tile(jnp.arange(grid_w, dtype=F32), rows)
    n_freq = C_HD // 4
    inv = ROPE_THETA ** (-jnp.arange(n_freq, dtype=F32) / n_freq)
    ang = jnp.concatenate([row[:, None] * inv, col[:, None] * inv], axis=-1)
    cos = jnp.cos(ang)
    sin = jnp.sin(ang)
    cos_tab = jnp.concatenate([jnp.ones((c_rows, C_HD), F32), jnp.concatenate([cos, cos], axis=-1)], axis=0)
    sin_tab = jnp.concatenate([jnp.zeros((c_rows, C_HD), F32), jnp.concatenate([-sin, sin], axis=-1)], axis=0)
    return cos_tab, sin_tab


GRID_W = 64


def kernel(x, c, ctx, c_ctx, ada_w, ada_b, norm1, norm2, final_norm, ab_w_in, ab_conv_qkv, ab_a_log, ab_dt_bias, ab_onorm, ab_conv_x, ab_rg_wr, ab_rg_br, ab_rg_wi, ab_rg_bi, ab_rg_lam, ab_w_out, at_w_qkv, at_q_norm, at_k_norm, at_w_out, moe_w_grp, moe_b_grp, moe_w_exp, moe_b_exp, moe_w_gate, moe_w_up, moe_w_down):
    nb, t_lat, _ = x.shape
    c_rows = ctx.shape[1]
    p_rows = c_rows + t_lat
    nt = p_rows // TM
    nc = c_rows // TM
    depth = ada_w.shape[0]
    assert depth == 2 and nb < 16 and c_rows % TM == 0 and t_lat % TM == 0

    cond = jnp.zeros((16, D_MODEL), F32).at[:nb].set(c).at[nb].set(c_ctx)
    mods = _ada_mod(cond, ada_w, ada_b)
    xu = jnp.concatenate([ctx, x], axis=1).reshape(nb * p_rows, D_MODEL)
    uni_mrow = lambda i: jnp.where(i % nt < nc, nb, i // nt)
    fnorm = final_norm.reshape(1, D_MODEL)

    w_pad, alog_vec, dt_vec, wbd, bbd, lam = _ab_params(
        ab_w_in[0], ab_conv_qkv[0], ab_a_log[0], ab_dt_bias[0], ab_conv_x[0],
        ab_rg_wr[0], ab_rg_br[0], ab_rg_wi[0], ab_rg_bi[0], ab_rg_lam[0])
    q, k, v, gout, yb, gates, a0, b0, a1, b1 = _inproj_ab(
        xu, mods[0], norm1[0].reshape(1, D_MODEL), w_pad, ab_conv_qkv[0], ab_conv_x[0],
        alog_vec, dt_vec, wbd, bbd, lam, nb, nt, nc)
    o_f, o_b = _delta(q, k, v, gates, nb, p_rows, c_rows)
    h_f, h_b = _lru(a0, b0, a1, b1, nb, nt, nc)
    xu = _merge_ab(o_f, o_b, h_f, h_b, gout, yb, xu, mods[0], ab_onorm[0].reshape(1, A_DK),
                   ab_w_out[0].astype(BF16), nb, nt, nc)
    xu = _moe_block(xu, mods[0], norm2[0].reshape(1, D_MODEL), moe_w_grp[0], moe_b_grp[0], moe_w_exp[0],
                    moe_b_exp[0], moe_w_gate[0], moe_w_up[0], moe_w_down[0], fnorm, uni_mrow, False)

    w_perm, qn, kn = _attn_params(at_w_qkv[0], at_q_norm[0], at_k_norm[0], t_lat, c_rows)
    cos_tab, sin_tab = _rope_tables(t_lat, c_rows, GRID_W)
    q, k, v = _inproj_attn(xu, mods[1], norm1[1].reshape(1, D_MODEL), w_perm, qn, kn, cos_tab, sin_tab,
                           nb, nt, nc)
    att = _attention(q, k, v, nb, nt, nc, p_rows)
    xl = _outproj_lat(att, xu, mods[1], at_w_out[0].astype(BF16), nt, nc)
    nq = nt - nc
    xl = _moe_block(xl, mods[1], norm2[1].reshape(1, D_MODEL), moe_w_grp[1], moe_b_grp[1], moe_w_exp[1],
                    moe_b_exp[1], moe_w_gate[1], moe_w_up[1], moe_w_down[1], fnorm, lambda j: j // nq, True)
    return xl.reshape(nb, t_lat, D_MODEL)
```

```python
import functools
import math

import jax
import jax.numpy as jnp
from jax import lax
from jax.experimental import pallas as pl
from jax.experimental.pallas import tpu as pltpu

F32 = jnp.float32
BF16 = jnp.bfloat16
HIGHEST = lax.Precision.HIGHEST

D_MODEL = 1024
EPS = 1e-6
TM = 256
LANES = 128
SUBLANES = 8

A_HEADS = 4
A_DK = 128
A_WIDTH = A_HEADS * A_DK
CHUNK = 64
CONV_W = 4
B_WIDTH = 512
B_BLOCKS = 8
B_BLK = B_WIDTH // B_BLOCKS
RG_C = 8.0

C_HEADS = 8
C_KV_HEADS = 2
C_HD = 128
C_GRP = C_HEADS // C_KV_HEADS
ROPE_THETA = 10000.0

N_GROUPS = 4
EXP_PER_GROUP = 8
N_EXPERTS = N_GROUPS * EXP_PER_GROUP
D_EXPERT = 512

VMEM_LIMIT = 56 * 1024 * 1024

ZC_QKV = 0
ZC_XB = 3 * A_WIDTH
ZC_GOUT = ZC_XB + B_WIDTH
ZC_YB = ZC_GOUT + A_WIDTH
ZC_GATE = ZC_YB + B_WIDTH
ZC_TOTAL = ZC_GATE + LANES
ZC_CONV = ZC_GOUT
HALO = SUBLANES


def _cparams(sem):
    return pltpu.CompilerParams(dimension_semantics=sem, vmem_limit_bytes=VMEM_LIMIT)


def _sigmoid(x):
    return jax.nn.sigmoid(x)


def _silu(x):
    return x * jax.nn.sigmoid(x)


def _softplus(x):
    return jnp.maximum(x, 0.0) + jnp.log1p(jnp.exp(-jnp.abs(x)))


def _gelu_tanh(x):
    c = math.sqrt(2.0 / math.pi)
    return 0.5 * x * (1.0 + jnp.tanh(c * (x + 0.044715 * (x * x * x))))


def _dot_split(a, b):
    a_hi = a.astype(BF16)
    b_hi = b.astype(BF16)
    a_lo = (a - a_hi.astype(F32)).astype(BF16)
    b_lo = (b - b_hi.astype(F32)).astype(BF16)
    dot = functools.partial(jnp.dot, preferred_element_type=F32)
    return dot(a_hi, b_hi) + (dot(a_hi, b_lo) + dot(a_lo, b_hi))


def _modulate(x, gain, shift, scale):
    y = x * lax.rsqrt(jnp.mean(x * x, axis=-1, keepdims=True) + EPS)
    return (y * gain) * (1.0 + scale) + shift


def _ada_kernel(cond_ref, w_ref, b_ref, o_ref):
    s = _silu(cond_ref[...]).astype(BF16)
    o_ref[...] = jnp.dot(s, w_ref[...].astype(BF16), preferred_element_type=F32) + b_ref[...]


def _ada_mod(cond, ada_w, ada_b):
    depth = ada_w.shape[0]
    tn = 1536
    nn = 6 * D_MODEL // tn
    out = pl.pallas_call(
        _ada_kernel,
        grid=(depth, nn),
        in_specs=[
            pl.BlockSpec((16, D_MODEL), lambda l, n: (0, 0)),
            pl.BlockSpec((None, D_MODEL, tn), lambda l, n: (l, 0, n)),
            pl.BlockSpec((None, 1, tn), lambda l, n: (l, 0, n)),
        ],
        out_specs=pl.BlockSpec((None, 16, tn), lambda l, n: (l, 0, n)),
        out_shape=jax.ShapeDtypeStruct((depth, 16, 6 * D_MODEL), F32),
        compiler_params=_cparams(("arbitrary", "arbitrary")),
        name="ada_mod",
    )(cond, ada_w, ada_b.reshape(depth, 1, 6 * D_MODEL))
    return out.reshape(depth, 16, 6, D_MODEL)


def _inproj_ab_kernel(nt, nc, xc_ref, xp_ref, xn_ref, mod_ref, n1_ref, w_ref, cq_ref, cx_ref,
                      alog_ref, dt_ref, wbd_ref, bbd_ref, lam_ref,
                      q_ref, k_ref, v_ref, go_ref, yb_ref, g_ref, a0_ref, b0_ref, a1_ref, b1_ref,
                      zbuf, xcbuf):
    i = pl.program_id(0)
    r = i % nt
    is_ctx = r < nc
    prev_ok = jnp.logical_and(r > 0, ((r - 1) < nc) == is_ctx)
    next_ok = jnp.logical_and(r < nt - 1, ((r + 1) < nc) == is_ctx)

    gain = n1_ref[...]
    shift = mod_ref[0:1, :]
    scale = mod_ref[1:2, :]
    xall = jnp.concatenate([xp_ref[...], xc_ref[...], xn_ref[...]], axis=0)
    h = _modulate(xall, gain, shift, scale).astype(BF16)
    zbuf[...] = jnp.dot(h, w_ref[...], preferred_element_type=F32)
    zbuf[0:HALO, 0:ZC_CONV] = jnp.where(prev_ok, zbuf[0:HALO, 0:ZC_CONV], 0.0)
    zbuf[HALO + TM:, 0:ZC_CONV] = jnp.where(next_ok, zbuf[HALO + TM:, 0:ZC_CONV], 0.0)

    def conv(c0, w_taps_ref, wc0):
        acc = None
        for j in range(CONV_W):
            start = HALO - CONV_W // 2 + j
            term = zbuf[start:start + TM, c0:c0 + LANES] * w_taps_ref[j:j + 1, wc0:wc0 + LANES]
            acc = term if acc is None else acc + term
        return acc

    outs = (q_ref, k_ref, v_ref)
    for s in range(3 * A_HEADS):
        y = _silu(conv(s * LANES, cq_ref, s * LANES))
        if s < 2 * A_HEADS:
            y = y * lax.rsqrt(jnp.sum(y * y, axis=-1, keepdims=True) + EPS)
        if s < A_HEADS:
            y = y * (A_DK ** -0.5)
        hh = s % A_HEADS
        outs[s // A_HEADS][:, hh * LANES:(hh + 1) * LANES] = y

    for s in range(B_WIDTH // LANES):
        xcbuf[:, s * LANES:(s + 1) * LANES] = conv(ZC_XB + s * LANES, cx_ref, s * LANES)

    go_ref[...] = zbuf[HALO:HALO + TM, ZC_GOUT:ZC_GOUT + A_WIDTH]
    yb_ref[...] = zbuf[HALO:HALO + TM, ZC_YB:ZC_YB + B_WIDTH]

    zg = zbuf[HALO:HALO + TM, ZC_GATE:ZC_GATE + LANES]
    lane = lax.broadcasted_iota(jnp.int32, (TM, LANES), 1)
    dec = -jnp.exp(alog_ref[...]) * _softplus(zg + dt_ref[...])
    g_ref[...] = jnp.where(jnp.bitwise_and(lane, 7) < A_HEADS, dec, _sigmoid(zg))

    xc = xcbuf[...]
    rg = jnp.dot(xc.astype(BF16), wbd_ref[...], preferred_element_type=F32) + bbd_ref[...]
    sp = _softplus(-lam_ref[...])
    for d, (a_ref, b_ref) in enumerate(((a0_ref, b0_ref), (a1_ref, b1_ref))):
        rr = _sigmoid(rg[:, d * B_WIDTH:(d + 1) * B_WIDTH])
        ii = _sigmoid(rg[:, (2 + d) * B_WIDTH:(3 + d) * B_WIDTH])
        a = jnp.exp(-RG_C * rr * sp[:, d * B_WIDTH:(d + 1) * B_WIDTH])
        a_ref[...] = a
        b_ref[...] = jnp.sqrt(1.0 - a * a) * ii * xc


def _inproj_ab(xu, mods, norm1, w_pad, conv_qkv, conv_x, alog_vec, dt_vec, wbd, bbd, lam, nb, nt, nc):
    rows = xu.shape[0]
    ntiles = rows // TM
    hb = TM // HALO
    nhalo = rows // HALO

    def mrow(i):
        return jnp.where(i % nt < nc, nb, i // nt)

    full = lambda shape: pl.BlockSpec(shape, lambda i: (0,) * len(shape))
    row_spec = lambda w: pl.BlockSpec((TM, w), lambda i: (i, 0))
    sds = lambda w: jax.ShapeDtypeStruct((rows, w), F32)
    return pl.pallas_call(
        functools.partial(_inproj_ab_kernel, nt, nc),
        grid=(ntiles,),
        in_specs=[
            row_spec(D_MODEL),
            pl.BlockSpec((HALO, D_MODEL), lambda i: (jnp.maximum(i * hb - 1, 0), 0)),
            pl.BlockSpec((HALO, D_MODEL), lambda i: (jnp.minimum((i + 1) * hb, nhalo - 1), 0)),
            pl.BlockSpec((None, 6, D_MODEL), lambda i: (mrow(i), 0, 0)),
            full((1, D_MODEL)),
            full((D_MODEL, ZC_TOTAL)),
            full((CONV_W, 3 * A_WIDTH)),
            full((CONV_W, B_WIDTH)),
            full((1, LANES)),
            full((1, LANES)),
            full((B_WIDTH, 4 * B_WIDTH)),
            full((1, 4 * B_WIDTH)),
            full((1, 2 * B_WIDTH)),
        ],
        out_specs=[row_spec(A_WIDTH)] * 5 + [row_spec(LANES)] + [row_spec(B_WIDTH)] * 4,
        out_shape=[sds(A_WIDTH)] * 5 + [sds(LANES)] + [sds(B_WIDTH)] * 4,
        scratch_shapes=[pltpu.VMEM((TM + 2 * HALO, ZC_TOTAL), F32), pltpu.VMEM((TM, B_WIDTH), F32)],
        compiler_params=_cparams(("arbitrary",)),
        name="inproj_ab",
    )(xu, xu, xu, mods, norm1, w_pad, conv_qkv, conv_x, alog_vec, dt_vec, wbd, bbd, lam)


PAIR = 2 * CHUNK
N_STREAMS = 4
PREP_CHUNKS = 4
PREP_PROBLEMS = PREP_CHUNKS * N_STREAMS
SCAN_BATCH = 2
DIAG_BLK = 8


def _delta_prep_kernel(q_ref, k_ref, v_ref, g_ref, u_ref, wq_ref, at_ref, kt_ref, aux_ref,
                       gcum_s, dec_s, kq_s, rhs_s, n_s, pwf_s, l_s):
    ii = lax.broadcasted_iota(jnp.int32, (PAIR, PAIR), 0)
    jj = lax.broadcasted_iota(jnp.int32, (PAIR, PAIR), 1)
    same = (ii < CHUNK) == (jj < CHUNK)
    ci = lax.broadcasted_iota(jnp.int32, (CHUNK, CHUNK), 0)
    cj = lax.broadcasted_iota(jnp.int32, (CHUNK, CHUNK), 1)
    aux_ref[...] = jnp.zeros_like(aux_ref)
    problems = [(c, d, p) for c in range(PREP_CHUNKS) for d in range(2) for p in range(2)]

    for c in range(PREP_CHUNKS):
        gates = g_ref[c * CHUNK:(c + 1) * CHUNK, :]
        for d in range(2):
            tri = (ci >= cj) if d == 0 else (ci <= cj)
            gcum_s[2 * c + d] = jnp.dot(tri.astype(F32), gates, precision=HIGHEST, preferred_element_type=F32)

    for n, (c, d, p) in enumerate(problems):
        x = 2 * d + p
        rows = slice(c * CHUNK, (c + 1) * CHUNK)
        heads = (2 * p, 2 * p + 1)
        lanes = [d * 2 * A_HEADS + h for h in heads]
        last = CHUNK - 1 if d == 0 else 0
        lower = jnp.logical_and(same, (ii >= jj) if d == 0 else (ii <= jj))
        stack = lambda ref: jnp.concatenate([ref[rows, h * A_DK:(h + 1) * A_DK] for h in heads], axis=0)
        gcum = gcum_s[2 * c + d]
        gates = g_ref[rows, :]
        gc_col = jnp.concatenate([gcum[:, l:l + 1] for l in lanes], axis=0)
        beta = jnp.concatenate([gates[:, l + A_HEADS:l + A_HEADS + 1] for l in lanes], axis=0)
        g_last = [gcum[last:last + 1, l:l + 1] for l in lanes]
        gl_col = jnp.concatenate([jnp.broadcast_to(g, (CHUNK, 1)) for g in g_last], axis=0)
        gc_mat = jnp.broadcast_to(gc_col, (PAIR, PAIR))
        dec_s[n] = jnp.where(lower, jnp.exp(jnp.where(lower, gc_mat - gc_mat.T, 0.0)), 0.0)
        eg = jnp.exp(gc_col)
        q = stack(q_ref)
        k = stack(k_ref)
        kb = k * beta
        kq_s[n] = lax.dot_general(jnp.concatenate([kb, q], axis=0).astype(BF16), k.astype(BF16),
                                  (((1,), (1,)), ((), ())), preferred_element_type=F32)
        rhs_s[n] = jnp.concatenate([stack(v_ref) * beta, kb * eg], axis=-1)
        wq_ref[x, (2 * c + 1) * PAIR:(2 * c + 2) * PAIR, :] = (q * eg).astype(BF16)
        kt_ref[x, c * PAIR:(c + 1) * PAIR, :] = (k * jnp.exp(gl_col - gc_col)).T.astype(BF16)
        aux_ref[c * SUBLANES + x:c * SUBLANES + x + 1, :] = jnp.concatenate(
            [jnp.broadcast_to(jnp.exp(g), (1, A_DK)) for g in g_last], axis=-1)

    blk = lambda s: jnp.right_shift(ii, s) == jnp.right_shift(jj, s)
    diag_sh = int(math.log2(DIAG_BLK))
    for n, (c, d, p) in enumerate(problems):
        strict = jnp.logical_and(same, (ii > jj) if d == 0 else (ii < jj))
        dec = dec_s[n]
        lmat = jnp.where(strict, kq_s[n, :PAIR] * dec, 0.0)
        l_s[n] = lmat
        neg_d = jnp.where(blk(diag_sh), -lmat, 0.0)
        n_s[n] = neg_d
        pwf_s[n] = neg_d
        at_ref[2 * d + p, c * PAIR:(c + 1) * PAIR, :] = (kq_s[n, PAIR:] * dec).astype(BF16)

    for _ in range(diag_sh - 1):
        for n in range(PREP_PROBLEMS):
            pw = pwf_s[n]
            pwf_s[n] = _dot_split(pw, pw)
        for n in range(PREP_PROBLEMS):
            nm = n_s[n]
            pw = pwf_s[n]
            n_s[n] = nm + pw + _dot_split(nm, pw)

    for s in range(diag_sh, int(math.log2(CHUNK))):
        off = jnp.logical_and(blk(s + 1), jnp.logical_not(blk(s)))
        for n in range(PREP_PROBLEMS):
            l_off = jnp.where(off, l_s[n], 0.0)
            pwf_s[n] = l_off + jnp.dot(l_off.astype(BF16), n_s[n].astype(BF16), preferred_element_type=F32)
        for n in range(PREP_PROBLEMS):
            nm = n_s[n]
            xm = pwf_s[n]
            n_s[n] = nm - xm - jnp.dot(nm.astype(BF16), xm.astype(BF16), preferred_element_type=F32)

    for n, (c, d, p) in enumerate(problems):
        x = 2 * d + p
        rhs = rhs_s[n]
        sol = rhs + jnp.dot(n_s[n].astype(BF16), rhs.astype(BF16), preferred_element_type=F32)
        u_ref[x, c * PAIR:(c + 1) * PAIR, :] = sol[:, :A_DK]
        wq_ref[x, 2 * c * PAIR:(2 * c + 1) * PAIR, :] = sol[:, A_DK:].astype(BF16)


def _delta_scan_kernel(*refs):
    ins = refs[:4 * N_STREAMS]
    auxf_ref, auxb_ref, of_ref, ob_ref, s_ref, r_s, vbd_s = refs[4 * N_STREAMS:]
    insts = [(bb, x) for bb in range(SCAN_BATCH) for x in range(N_STREAMS)]

    @pl.when(pl.program_id(1) == 0)
    def _():
        s_ref[...] = jnp.zeros_like(s_ref)

    zeros = jnp.zeros((CHUNK, A_DK), F32)
    for n, (bb, x) in enumerate(insts):
        wq_ref = ins[4 * x + 1]
        r_s[n] = jnp.dot(wq_ref[bb], s_ref[n].astype(BF16), preferred_element_type=F32)
    for n, (bb, x) in enumerate(insts):
        d, p = divmod(x, 2)
        u_ref, _, at_ref, _ = ins[4 * x:4 * x + 4]
        o_ref = of_ref if d == 0 else ob_ref
        vn_a = u_ref[bb, :CHUNK, :] - r_s[n, :CHUNK, :A_DK]
        vn_b = u_ref[bb, CHUNK:, :] - r_s[n, CHUNK:PAIR, A_DK:]
        av = jnp.dot(at_ref[bb], jnp.concatenate([vn_a, vn_b], axis=0).astype(BF16),
                     preferred_element_type=F32)
        o_ref[bb, :, 2 * p * A_DK:(2 * p + 1) * A_DK] = r_s[n, PAIR:PAIR + CHUNK, :A_DK] + av[:CHUNK]
        o_ref[bb, :, (2 * p + 1) * A_DK:(2 * p + 2) * A_DK] = r_s[n, PAIR + CHUNK:, A_DK:] + av[CHUNK:]
        vbd_s[n] = jnp.concatenate([jnp.concatenate([vn_a, zeros], axis=1),
                                    jnp.concatenate([zeros, vn_b], axis=1)], axis=0).astype(BF16)
    for n, (bb, x) in enumerate(insts):
        aux_ref = auxf_ref if x < 2 else auxb_ref
        kt_ref = ins[4 * x + 3]
        s_ref[n] = s_ref[n] * aux_ref[bb, x:x + 1, :] + jnp.dot(kt_ref[bb], vbd_s[n],
                                                               preferred_element_type=F32)


def _bwd_order(s, n_ctx, n_all):
    return jnp.where(s < n_ctx, n_ctx - 1 - s, n_all - 1 - (s - n_ctx))


def _delta(q, k, v, g, nb, p_rows, c_rows):
    rows = q.shape[0]
    nchunks = rows // CHUNK
    pr = PREP_CHUNKS * CHUNK
    row_spec = lambda w: pl.BlockSpec((pr, w), lambda i: (i, 0))
    stream_spec = lambda m: pl.BlockSpec((N_STREAMS, PREP_CHUNKS * m, A_DK), lambda i: (0, i, 0))
    stream_sds = lambda m, dt: jax.ShapeDtypeStruct((N_STREAMS, nchunks * m, A_DK), dt)
    u, wq, at, kt, aux = pl.pallas_call(
        _delta_prep_kernel,
        grid=(nchunks // PREP_CHUNKS,),
        in_specs=[row_spec(A_WIDTH)] * 3 + [row_spec(LANES)],
        out_specs=[stream_spec(PAIR), stream_spec(2 * PAIR), stream_spec(PAIR), stream_spec(PAIR),
                   pl.BlockSpec((PREP_CHUNKS * SUBLANES, 2 * A_DK), lambda i: (i, 0))],
        out_shape=[stream_sds(PAIR, F32), stream_sds(2 * PAIR, BF16), stream_sds(PAIR, BF16),
                   stream_sds(PAIR, BF16), jax.ShapeDtypeStruct((nchunks * SUBLANES, 2 * A_DK), F32)],
        scratch_shapes=[
            pltpu.VMEM((2 * PREP_CHUNKS, CHUNK, LANES), F32),
            pltpu.VMEM((PREP_PROBLEMS, PAIR, PAIR), F32),
            pltpu.VMEM((PREP_PROBLEMS, 2 * PAIR, PAIR), F32),
            pltpu.VMEM((PREP_PROBLEMS, PAIR, 2 * A_DK), F32),
            pltpu.VMEM((PREP_PROBLEMS, PAIR, PAIR), F32),
            pltpu.VMEM((PREP_PROBLEMS, PAIR, PAIR), F32),
            pltpu.VMEM((PREP_PROBLEMS, PAIR, PAIR), F32),
        ],
        compiler_params=_cparams(("arbitrary",)),
        name="delta_prep",
    )(q, k, v, g)

    assert nb % SCAN_BATCH == 0
    ng = nb // SCAN_BATCH
    n_all = p_rows // CHUNK
    n_ctx = c_rows // CHUNK
    order = (lambda s: s, lambda s: _bwd_order(s, n_ctx, n_all))
    view = lambda arr, m: arr.reshape(N_STREAMS, ng, SCAN_BATCH, n_all * m, A_DK)
    in_specs, args = [], []
    for x in range(N_STREAMS):
        pos = order[x // 2]
        for arr, m in ((u, PAIR), (wq, 2 * PAIR), (at, PAIR), (kt, PAIR)):
            in_specs.append(pl.BlockSpec((None, None, SCAN_BATCH, m, A_DK),
                                         lambda b, s, x=x, pos=pos: (x, b, 0, pos(s), 0)))
            args.append(view(arr, m))
    for pos in order:
        in_specs.append(pl.BlockSpec((None, SCAN_BATCH, SUBLANES, 2 * A_DK), lambda b, s, pos=pos: (b, 0, pos(s), 0)))
        args.append(aux.reshape(ng, SCAN_BATCH, n_all * SUBLANES, 2 * A_DK))
    out_spec = lambda pos: pl.BlockSpec((None, SCAN_BATCH, CHUNK, A_WIDTH), lambda b, s: (b, 0, pos(s), 0))
    n_inst = SCAN_BATCH * N_STREAMS
    o_f, o_b = pl.pallas_call(
        _delta_scan_kernel,
        grid=(ng, n_all),
        in_specs=in_specs,
        out_specs=[out_spec(order[0]), out_spec(order[1])],
        out_shape=[jax.ShapeDtypeStruct((ng, SCAN_BATCH, p_rows, A_WIDTH), F32)] * 2,
        scratch_shapes=[
            pltpu.VMEM((n_inst, A_DK, 2 * A_DK), F32),
            pltpu.VMEM((n_inst, 2 * PAIR, 2 * A_DK), F32),
            pltpu.VMEM((n_inst, PAIR, 2 * A_DK), BF16),
        ],
        compiler_params=_cparams(("arbitrary", "arbitrary")),
        name="delta_scan",
    )(*args)
    return o_f.reshape(rows, A_WIDTH), o_b.reshape(rows, A_WIDTH)


def _lru_kernel(a0_ref, b0_ref, a1_ref, b1_ref, hf_ref, hb_ref, carry_ref):
    @pl.when(pl.program_id(1) == 0)
    def _():
        carry_ref[...] = jnp.zeros_like(carry_ref)

    row = lax.broadcasted_iota(jnp.int32, (SUBLANES, B_WIDTH), 0)
    ngroups = TM // SUBLANES

    def scan_group(a_ref, b_ref, h_ref, r0, h_in, reverse):
        a = a_ref[pl.ds(r0, SUBLANES), :]
        b = b_ref[pl.ds(r0, SUBLANES), :]
        for sft in (1, 2, 4):
            shift = SUBLANES - sft if reverse else sft
            keep = (row < SUBLANES - sft) if reverse else (row >= sft)
            a_sh = pltpu.roll(a, shift, axis=0)
            b_sh = pltpu.roll(b, shift, axis=0)
            b = jnp.where(keep, a * b_sh + b, b)
            a = jnp.where(keep, a * a_sh, a)
        hrows = a * h_in + b
        h_ref[pl.ds(r0, SUBLANES), :] = hrows
        return hrows[0:1, :] if reverse else hrows[SUBLANES - 1:SUBLANES, :]

    def fwd_body(t, h_in):
        r0 = pl.multiple_of(t * SUBLANES, SUBLANES)
        return scan_group(a0_ref, b0_ref, hf_ref, r0, h_in, False)

    def bwd_body(t, h_in):
        r0 = pl.multiple_of((ngroups - 1 - t) * SUBLANES, SUBLANES)
        return scan_group(a1_ref, b1_ref, hb_ref, r0, h_in, True)

    carry_ref[0:1, :] = lax.fori_loop(0, ngroups, fwd_body, carry_ref[0:1, :])
    carry_ref[1:2, :] = lax.fori_loop(0, ngroups, bwd_body, carry_ref[1:2, :])


def _lru(a0, b0, a1, b1, nb, nt, nc):
    rows = a0.shape[0]
    fwd = pl.BlockSpec((TM, B_WIDTH), lambda b, s: (b * nt + s, 0))
    bwd = pl.BlockSpec((TM, B_WIDTH), lambda b, s: (b * nt + _bwd_order(s, nc, nt), 0))
    return pl.pallas_call(
        _lru_kernel,
        grid=(nb, nt),
        in_specs=[fwd, fwd, bwd, bwd],
        out_specs=[fwd, bwd],
        out_shape=[jax.ShapeDtypeStruct((rows, B_WIDTH), F32)] * 2,
        scratch_shapes=[pltpu.VMEM((SUBLANES, B_WIDTH), F32)],
        compiler_params=_cparams(("arbitrary", "arbitrary")),
        name="rg_lru_scan",
    )(a0, b0, a1, b1)


def _merge_ab_kernel(of_ref, ob_ref, hf_ref, hb_ref, go_ref, yb_ref, x_ref, mod_ref, on_ref, w_ref, xo_ref):
    parts = []
    for h in range(A_HEADS):
        sl = slice(h * A_DK, (h + 1) * A_DK)
        o = of_ref[:, sl] + ob_ref[:, sl]
        n = o * lax.rsqrt(jnp.mean(o * o, axis=-1, keepdims=True) + EPS) * on_ref[...]
        parts.append((n * _silu(go_ref[:, sl])).astype(BF16))
    parts.append(((hf_ref[...] + hb_ref[...]) * _gelu_tanh(yb_ref[...])).astype(BF16))
    cat = jnp.concatenate(parts, axis=-1)
    y = jnp.dot(cat, w_ref[...], preferred_element_type=F32)
    xo_ref[...] = x_ref[...] + mod_ref[2:3, :] * y


def _merge_ab(o_f, o_b, h_f, h_b, gout, yb, xu, mods, onorm, w_out, nb, nt, nc):
    rows = xu.shape[0]
    half = pl.BlockSpec((TM, A_WIDTH), lambda i: (i, 0))
    wide = pl.BlockSpec((TM, D_MODEL), lambda i: (i, 0))
    mrow = lambda i: jnp.where(i % nt < nc, nb, i // nt)
    return pl.pallas_call(
        _merge_ab_kernel,
        grid=(rows // TM,),
        in_specs=[half] * 6 + [
            wide,
            pl.BlockSpec((None, 6, D_MODEL), lambda i: (mrow(i), 0, 0)),
            pl.BlockSpec((1, A_DK), lambda i: (0, 0)),
            pl.BlockSpec((D_MODEL, D_MODEL), lambda i: (0, 0)),
        ],
        out_specs=wide,
        out_shape=jax.ShapeDtypeStruct((rows, D_MODEL), F32),
        compiler_params=_cparams(("arbitrary",)),
        name="merge_ab",
    )(o_f, o_b, h_f, h_b, gout, yb, xu, mods, onorm, w_out)


def _inproj_attn_kernel(x_ref, mod_ref, n1_ref, w_ref, qn_ref, kn_ref, cos_ref, sin_ref, q_ref, k_ref, v_ref):
    h = _modulate(x_ref[...], n1_ref[...], mod_ref[0:1, :], mod_ref[1:2, :]).astype(BF16)
    z = jnp.dot(h, w_ref[...], preferred_element_type=F32)
    cos = cos_ref[...]
    sin = sin_ref[...]
    for hd in range(C_HEADS + C_KV_HEADS):
        xh = z[:, hd * C_HD:(hd + 1) * C_HD]
        gain = qn_ref[...] if hd < C_HEADS else kn_ref[...]
        n = xh * lax.rsqrt(jnp.mean(xh * xh, axis=-1, keepdims=True) + EPS) * gain
        rot = n * cos + pltpu.roll(n, C_HD // 2, axis=1) * sin
        if hd < C_HEADS:
            q_ref[:, hd * C_HD:(hd + 1) * C_HD] = (rot * (C_HD ** -0.5)).astype(BF16)
        else:
            kh = hd - C_HEADS
            k_ref[:, kh * C_HD:(kh + 1) * C_HD] = rot.astype(BF16)
    v_ref[...] = z[:, (C_HEADS + C_KV_HEADS) * C_HD:].astype(BF16)


def _inproj_attn(xu, mods, norm1, w_perm, qn, kn, cos_tab, sin_tab, nb, nt, nc):
    rows = xu.shape[0]
    nqk = (C_HEADS + 2 * C_KV_HEADS) * C_HD
    mrow = lambda i: jnp.where(i % nt < nc, nb, i // nt)
    full = lambda shape: pl.BlockSpec(shape, lambda i: (0,) * len(shape))
    tab = pl.BlockSpec((TM, C_HD), lambda i: (i % nt, 0))
    return pl.pallas_call(
        _inproj_attn_kernel,
        grid=(rows // TM,),
        in_specs=[
            pl.BlockSpec((TM, D_MODEL), lambda i: (i, 0)),
            pl.BlockSpec((None, 6, D_MODEL), lambda i: (mrow(i), 0, 0)),
            full((1, D_MODEL)),
            full((D_MODEL, nqk)),
            full((1, C_HD)),
            full((1, C_HD)),
            tab, tab,
        ],
        out_specs=[
            pl.BlockSpec((TM, C_HEADS * C_HD), lambda i: (i, 0)),
            pl.BlockSpec((TM, C_KV_HEADS * C_HD), lambda i: (i, 0)),
            pl.BlockSpec((TM, C_KV_HEADS * C_HD), lambda i: (i, 0)),
        ],
        out_shape=[
            jax.ShapeDtypeStruct((rows, C_HEADS * C_HD), BF16),
            jax.ShapeDtypeStruct((rows, C_KV_HEADS * C_HD), BF16),
            jax.ShapeDtypeStruct((rows, C_KV_HEADS * C_HD), BF16),
        ],
        compiler_params=_cparams(("arbitrary",)),
        name="inproj_attn",
    )(xu, mods, norm1, w_perm, qn, kn, cos_tab, sin_tab)


def _attn_kernel(q_ref, k_ref, v_ref, o_ref):
    k = k_ref[...]
    v = v_ref[...]
    for g in range(C_GRP):
        sl = slice(g * C_HD, (g + 1) * C_HD)
        s = lax.dot_general(q_ref[:, sl], k, (((1,), (1,)), ((), ())), preferred_element_type=F32)
        p = jnp.exp(s - jnp.max(s, axis=-1, keepdims=True))
        denom = jnp.sum(p, axis=-1, keepdims=True)
        o = jnp.dot(p.astype(BF16), v, preferred_element_type=F32) / denom
        o_ref[:, sl] = o.astype(BF16)


def _attention(q, k, v, nb, nt, nc, p_rows):
    nq = nt - nc
    gw = C_GRP * C_HD
    return pl.pallas_call(
        _attn_kernel,
        grid=(nb, C_KV_HEADS, nq),
        in_specs=[
            pl.BlockSpec((TM, gw), lambda b, h, t: (b * nt + nc + t, h)),
            pl.BlockSpec((p_rows, C_HD), lambda b, h, t: (b, h)),
            pl.BlockSpec((p_rows, C_HD), lambda b, h, t: (b, h)),
        ],
        out_specs=pl.BlockSpec((TM, gw), lambda b, h, t: (b * nq + t, h)),
        out_shape=jax.ShapeDtypeStruct((nb * nq * TM, C_HEADS * C_HD), BF16),
        compiler_params=_cparams(("arbitrary", "arbitrary", "arbitrary")),
        name="gqa_attention",
    )(q, k, v)


def _outproj_kernel(a_ref, x_ref, mod_ref, w_ref, xo_ref):
    y = jnp.dot(a_ref[...], w_ref[...], preferred_element_type=F32)
    xo_ref[...] = x_ref[...] + mod_ref[2:3, :] * y


def _outproj_lat(a, xu, mods, w_out, nt, nc):
    rows = a.shape[0]
    nq = nt - nc
    return pl.pallas_call(
        _outproj_kernel,
        grid=(rows // TM,),
        in_specs=[
            pl.BlockSpec((TM, D_MODEL), lambda j: (j, 0)),
            pl.BlockSpec((TM, D_MODEL), lambda j: ((j // nq) * nt + nc + j % nq, 0)),
            pl.BlockSpec((None, 6, D_MODEL), lambda j: (j // nq, 0, 0)),
            pl.BlockSpec((D_MODEL, D_MODEL), lambda j: (0, 0)),
        ],
        out_specs=pl.BlockSpec((TM, D_MODEL), lambda j: (j, 0)),
        out_shape=jax.ShapeDtypeStruct((rows, D_MODEL), F32),
        compiler_params=_cparams(("arbitrary",)),
        name="outproj_attn",
    )(a, xu, mods, w_out)


TE = 256
SEL_E1, SEL_E2, SEL_W1, SEL_W2, SEL_R1, SEL_R2 = 0, 1, 2, 3, 4, 5


def _route_kernel(x_ref, mod_ref, n2_ref, wr_ref, br_ref, h_ref, sel_ref, cnt_ref, run_s):
    @pl.when(pl.program_id(0) == 0)
    def _():
        run_s[...] = jnp.zeros_like(run_s)

    h = _modulate(x_ref[...], n2_ref[...], mod_ref[3:4, :], mod_ref[4:5, :])
    h_ref[...] = h
    lg = jnp.dot(h, wr_ref[...], precision=HIGHEST, preferred_element_type=F32) + br_ref[...]
    lane = lax.broadcasted_iota(jnp.int32, lg.shape, 1)
    neg = jnp.float32(-jnp.inf)
    big = jnp.int32(1 << 20)
    is_grp = jnp.logical_and(lane >= N_EXPERTS, lane < N_EXPERTS + N_GROUPS)
    gl = jnp.where(is_grp, lg, neg)
    gmax = jnp.max(gl, axis=-1, keepdims=True)
    gidx = jnp.min(jnp.where(gl == gmax, lane - N_EXPERTS, big), axis=-1, keepdims=True)
    g_w = 1.0 / jnp.sum(jnp.where(is_grp, jnp.exp(gl - gmax), 0.0), axis=-1, keepdims=True)
    in_grp = jnp.logical_and(lane < N_EXPERTS, jnp.right_shift(lane, 3) == gidx)
    e1 = jnp.where(in_grp, lg, neg)
    m1 = jnp.max(e1, axis=-1, keepdims=True)
    i1 = jnp.min(jnp.where(e1 == m1, lane, big), axis=-1, keepdims=True)
    e2 = jnp.where(lane == i1, neg, e1)
    m2 = jnp.max(e2, axis=-1, keepdims=True)
    i2 = jnp.min(jnp.where(e2 == m2, lane, big), axis=-1, keepdims=True)
    t = jnp.exp(m2 - m1)
    w1 = g_w / (1.0 + t)
    w2 = g_w * t / (1.0 + t)

    hit1 = lane == i1
    hit2 = lane == i2
    onehot = jnp.where(hit1, 1.0, 0.0) + jnp.where(hit2, 1.0, 0.0)
    ri = lax.broadcasted_iota(jnp.int32, (TM, TM), 0)
    rj = lax.broadcasted_iota(jnp.int32, (TM, TM), 1)
    earlier = jnp.dot(jnp.where(rj < ri, 1.0, 0.0).astype(BF16), onehot.astype(BF16),
                      preferred_element_type=F32)
    base = run_s[0:1, :] + earlier
    r1 = jnp.sum(jnp.where(hit1, base, 0.0), axis=-1, keepdims=True)
    r2 = jnp.sum(jnp.where(hit2, base, 0.0), axis=-1, keepdims=True)
    total = run_s[0:1, :] + jnp.sum(onehot, axis=0, keepdims=True)
    run_s[0:1, :] = total
    cnt_ref[...] = jnp.broadcast_to(total, cnt_ref.shape)

    rec = jnp.zeros(lg.shape, F32)
    for ln, val in ((SEL_E1, i1.astype(F32)), (SEL_E2, i2.astype(F32)), (SEL_W1, w1), (SEL_W2, w2),
                    (SEL_R1, r1), (SEL_R2, r2)):
        rec = jnp.where(lane == ln, val, rec)
    sel_ref[...] = rec


def _route_call(x, mods, norm2, w_route, b_route, mrow):
    rows = x.shape[0]
    return pl.pallas_call(
        _route_kernel,
        grid=(rows // TM,),
        in_specs=[
            pl.BlockSpec((TM, D_MODEL), lambda i: (i, 0)),
            pl.BlockSpec((None, 6, D_MODEL), lambda i: (mrow(i), 0, 0)),
            pl.BlockSpec((1, D_MODEL), lambda i: (0, 0)),
            pl.BlockSpec((D_MODEL, LANES), lambda i: (0, 0)),
            pl.BlockSpec((1, LANES), lambda i: (0, 0)),
        ],
        out_specs=[
            pl.BlockSpec((TM, D_MODEL), lambda i: (i, 0)),
            pl.BlockSpec((TM, LANES), lambda i: (i, 0)),
            pl.BlockSpec((SUBLANES, LANES), lambda i: (0, 0)),
        ],
        out_shape=[
            jax.ShapeDtypeStruct((rows, D_MODEL), F32),
            jax.ShapeDtypeStruct((rows, LANES), F32),
            jax.ShapeDtypeStruct((SUBLANES, LANES), F32),
        ],
        scratch_shapes=[pltpu.VMEM((SUBLANES, LANES), F32)],
        compiler_params=_cparams(("arbitrary",)),
        name="moe_route",
    )(x, mods, norm2, w_route, b_route)


def _moe_plan(sel, cnt, n_tok, e_base):
    n_asg = 2 * n_tok
    max_tiles = n_asg // TE + N_EXPERTS
    eids = jnp.arange(N_EXPERTS, dtype=jnp.int32)
    counts = cnt[0, :N_EXPERTS].astype(jnp.int32)
    tiles_e = (counts + TE - 1) // TE
    cum_tiles = jnp.cumsum(tiles_e)
    n_tiles = cum_tiles[-1]
    off = (cum_tiles - tiles_e) * TE
    start = jnp.cumsum(counts) - counts
    e = sel[:, SEL_E1:SEL_E2 + 1].astype(jnp.int32)
    rank = sel[:, SEL_R1:SEL_R2 + 1].astype(jnp.int32)
    off_e = jnp.sum(jnp.where(e[:, :, None] == eids, off, 0), axis=-1)
    pos = (off_e + rank).reshape(n_asg)
    tok = jnp.arange(n_asg, dtype=jnp.int32) // 2
    _, tok_sorted = lax.sort_key_val(pos, tok)
    last_e = jnp.max(jnp.where(tiles_e > 0, eids, 0))
    t_idx = jnp.arange(max_tiles, dtype=jnp.int32)
    tile_e = jnp.minimum(jnp.sum((cum_tiles[None, :] <= t_idx[:, None]).astype(jnp.int32), axis=1), last_e)
    r_d = (t_idx * TE - off[tile_e])[:, None] + jnp.arange(TE, dtype=jnp.int32)[None, :]
    valid = jnp.logical_and(r_d < counts[tile_e][:, None], (t_idx < n_tiles)[:, None])
    idx = jnp.clip(start[tile_e][:, None] + r_d, 0, n_asg - 1)
    src = jnp.where(valid, tok_sorted[idx], 0).reshape(max_tiles * TE).astype(jnp.int32)
    return (tile_e + e_base).astype(jnp.int32), n_tiles.reshape(1).astype(jnp.int32), src, pos


def _moe_experts_kernel(te_ref, nt_ref, src_ref, h_hbm, wg_ref, wu_ref, wd_ref, y_ref,
                        buf, xb, wg16, wu16, wd16, sem):
    t = pl.program_id(0)
    n_tiles = nt_ref[0]

    def row_copy(tok, slot, r):
        return pltpu.make_async_copy(h_hbm.at[pl.ds(tok, 1), :], buf.at[slot, pl.ds(r, 1), :], sem.at[slot])

    @pl.when(t == 0)
    def _():
        def body(r, carry):
            row_copy(src_ref[r], 0, r).start()
            return carry

        lax.fori_loop(0, TE, body, 0, unroll=8)

    changed = jnp.logical_or(t == 0, te_ref[t] != te_ref[jnp.maximum(t - 1, 0)])

    @pl.when(jnp.logical_and(t < n_tiles, changed))
    def _():
        wg16[...] = wg_ref[...].astype(BF16)
        wu16[...] = wu_ref[...].astype(BF16)
        wd16[...] = wd_ref[...].astype(BF16)

    def run_tile(prefetch):
        slot = t % 2
        pltpu.make_async_copy(h_hbm.at[pl.ds(0, TE), :], buf.at[slot], sem.at[slot]).wait()
        xb[...] = buf[slot].astype(BF16)
        if prefetch:
            base = (t + 1) * TE
            for r in range(TE):
                row_copy(src_ref[base + r], 1 - slot, r).start()
        x = xb[...]
        hg = jnp.dot(x, wg16[...], preferred_element_type=F32)
        hu = jnp.dot(x, wu16[...], preferred_element_type=F32)
        act = (_silu(hg) * hu).astype(BF16)
        y_ref[...] = jnp.dot(act, wd16[...], preferred_element_type=F32)

    @pl.when(t + 1 < n_tiles)
    def _():
        run_tile(True)

    @pl.when(t + 1 == n_tiles)
    def _():
        run_tile(False)

    @pl.when(t >= n_tiles)
    def _():
        y_ref[...] = jnp.zeros_like(y_ref)


def _moe_experts(h2, tile_e, n_tiles, src, w_gate, w_up, w_down):
    max_tiles = tile_e.shape[0]
    wspec = lambda a, b: pl.BlockSpec((None, a, b), lambda t, te, nt, sr: (te[t], 0, 0))
    return pl.pallas_call(
        _moe_experts_kernel,
        grid_spec=pltpu.PrefetchScalarGridSpec(
            num_scalar_prefetch=3,
            grid=(max_tiles,),
            in_specs=[
                pl.BlockSpec(memory_space=pl.ANY),
                wspec(D_MODEL, D_EXPERT), wspec(D_MODEL, D_EXPERT), wspec(D_EXPERT, D_MODEL),
            ],
            out_specs=pl.BlockSpec((TE, D_MODEL), lambda t, te, nt, sr: (t, 0)),
            scratch_shapes=[
                pltpu.VMEM((2, TE, D_MODEL), F32),
                pltpu.VMEM((TE, D_MODEL), BF16),
                pltpu.VMEM((D_MODEL, D_EXPERT), BF16),
                pltpu.VMEM((D_MODEL, D_EXPERT), BF16),
                pltpu.VMEM((D_EXPERT, D_MODEL), BF16),
                pltpu.SemaphoreType.DMA((2,)),
            ],
        ),
        out_shape=jax.ShapeDtypeStruct((max_tiles * TE, D_MODEL), F32),
        compiler_params=_cparams(("arbitrary",)),
        name="moe_experts",
    )(tile_e, n_tiles, src, h2, w_gate, w_up, w_down)


def _moe_combine_kernel(final, pos_ref, y_hbm, x_ref, sel_ref, mod_ref, fn_ref, o_ref, buf, sem):
    i = pl.program_id(0)
    n = pl.num_programs(0)

    def row_copy(p, slot, k, r):
        return pltpu.make_async_copy(y_hbm.at[pl.ds(p, 1), :], buf.at[slot, k, pl.ds(r, 1), :], sem.at[slot])

    @pl.when(i == 0)
    def _():
        def body(r, carry):
            row_copy(pos_ref[2 * r], 0, 0, r).start()
            row_copy(pos_ref[2 * r + 1], 0, 1, r).start()
            return carry

        lax.fori_loop(0, TM, body, 0, unroll=8)

    slot = i % 2

    @pl.when(i + 1 < n)
    def _():
        base = (i + 1) * (2 * TM)
        for r in range(TM):
            row_copy(pos_ref[base + 2 * r], 1 - slot, 0, r).start()
            row_copy(pos_ref[base + 2 * r + 1], 1 - slot, 1, r).start()

    for k in range(2):
        pltpu.make_async_copy(y_hbm.at[pl.ds(0, TM), :], buf.at[slot, k], sem.at[slot]).wait()
    sel = sel_ref[...]
    y = sel[:, SEL_W1:SEL_W1 + 1] * buf[slot, 0] + sel[:, SEL_W2:SEL_W2 + 1] * buf[slot, 1]
    x = x_ref[...] + mod_ref[5:6, :] * y
    if final:
        x = x * lax.rsqrt(jnp.mean(x * x, axis=-1, keepdims=True) + EPS) * fn_ref[...]
    o_ref[...] = x


def _moe_combine(y_sorted, pos, x, sel, mods, final_norm, mrow, final):
    rows = x.shape[0]
    wide = pl.BlockSpec((TM, D_MODEL), lambda i, ps: (i, 0))
    return pl.pallas_call(
        functools.partial(_moe_combine_kernel, final),
        grid_spec=pltpu.PrefetchScalarGridSpec(
            num_scalar_prefetch=1,
            grid=(rows // TM,),
            in_specs=[
                pl.BlockSpec(memory_space=pl.ANY),
                wide,
                pl.BlockSpec((TM, LANES), lambda i, ps: (i, 0)),
                pl.BlockSpec((None, 6, D_MODEL), lambda i, ps: (mrow(i), 0, 0)),
                pl.BlockSpec((1, D_MODEL), lambda i, ps: (0, 0)),
            ],
            out_specs=wide,
            scratch_shapes=[pltpu.VMEM((2, 2, TM, D_MODEL), F32), pltpu.SemaphoreType.DMA((2,))],
        ),
        out_shape=jax.ShapeDtypeStruct((rows, D_MODEL), F32),
        compiler_params=_cparams(("arbitrary",)),
        name="moe_combine",
    )(pos, y_sorted, x, sel, mods, final_norm)


def _moe_block(x, mods, norm2, w_grp, b_grp, w_exp, b_exp, w_gate, w_up, w_down, layer, final_norm, mrow, final):
    pad = LANES - N_EXPERTS - N_GROUPS
    w_route = jnp.concatenate([w_exp, w_grp, jnp.zeros((D_MODEL, pad), F32)], axis=1)
    b_route = jnp.concatenate([b_exp, b_grp, jnp.zeros((pad,), F32)]).reshape(1, LANES)
    h2, sel, cnt = _route_call(x, mods, norm2, w_route, b_route, mrow)
    tile_e, n_tiles, src, pos = _moe_plan(sel, cnt, x.shape[0], layer * N_EXPERTS)
    y_sorted = _moe_experts(h2, tile_e, n_tiles, src, w_gate, w_up, w_down)
    return _moe_combine(y_sorted, pos, x, sel, mods, final_norm, mrow, final)


def _ab_params(w_in, conv_qkv, a_log, dt_bias, conv_x, rg_wr, rg_br, rg_wi, rg_bi, rg_lam):
    o1 = 3 * A_WIDTH
    o2 = 4 * A_WIDTH
    o3 = o2 + 4 * A_HEADS
    o4 = o3 + B_WIDTH
    gate_cols = jnp.concatenate([w_in[:, o2:o3], jnp.zeros((D_MODEL, LANES - 4 * A_HEADS), F32)], axis=1)
    w_pad = jnp.concatenate([w_in[:, :o1], w_in[:, o3:o4], w_in[:, o1:o2], w_in[:, o4:], gate_cols],
                            axis=1).astype(BF16)
    z4 = jnp.zeros((A_HEADS,), F32)
    ztail = jnp.zeros((LANES - 4 * A_HEADS,), F32)
    alog_vec = jnp.concatenate([a_log[0], z4, a_log[1], z4, ztail]).reshape(1, LANES)
    dt_vec = jnp.concatenate([dt_bias[0], z4, dt_bias[1], z4, ztail]).reshape(1, LANES)
    eye = jnp.eye(B_BLOCKS, dtype=F32)
    bdiag = lambda w: jnp.einsum('gkj,gh->gkhj', w, eye).reshape(B_WIDTH, B_WIDTH)
    wbd = jnp.concatenate([bdiag(rg_wr[0]), bdiag(rg_wr[1]), bdiag(rg_wi[0]), bdiag(rg_wi[1])], axis=1)
    bbd = jnp.concatenate([rg_br[0].reshape(-1), rg_br[1].reshape(-1),
                           rg_bi[0].reshape(-1), rg_bi[1].reshape(-1)]).reshape(1, 4 * B_WIDTH)
    lam = rg_lam.reshape(1, 2 * B_WIDTH)
    return w_pad, alog_vec, dt_vec, wbd.astype(BF16), bbd, lam


def _attn_params(w_qkv, q_norm, k_norm, t_lat, c_rows):
    half = C_HD // 2
    perm = jnp.concatenate([jnp.arange(half) * 2, jnp.arange(half) * 2 + 1])
    nrot = (C_HEADS + C_KV_HEADS) * C_HD
    cols = (jnp.arange(C_HEADS + C_KV_HEADS)[:, None] * C_HD + perm[None, :]).reshape(-1)
    cols = jnp.concatenate([cols, jnp.arange(nrot, w_qkv.shape[1])])
    w_perm = w_qkv[:, cols].astype(BF16)
    qn = q_norm[perm].reshape(1, C_HD)
    kn = k_norm[perm].reshape(1, C_HD)
    return w_perm, qn, kn


def _rope_tables(t_lat, c_rows, grid_w):
    rows = t_lat // grid_w
    row = jnp.repeat(jnp.arange(rows, dtype=F32), grid_w)
    col = jnp.tile(jnp.arange(grid_w, dtype=F32), rows)
    n_freq = C_HD // 4
    inv = ROPE_THETA ** (-jnp.arange(n_freq, dtype=F32) / n_freq)
    ang = jnp.concatenate([row[:, None] * inv, col[:, None] * inv], axis=-1)
    cos = jnp.cos(ang)
    sin = jnp.sin(ang)
    cos_tab = jnp.concatenate([jnp.ones((c_rows, C_HD), F32), jnp.concatenate([cos, cos], axis=-1)], axis=0)
    sin_tab = jnp.concatenate([jnp.zeros((c_rows, C_HD), F32), jnp.concatenate([-sin, sin], axis=-1)], axis=0)
    return cos_tab, sin_tab


GRID_W = 64


def kernel(x, c, ctx, c_ctx, ada_w, ada_b, norm1, norm2, final_norm, ab_w_in, ab_conv_qkv, ab_a_log, ab_dt_bias, ab_onorm, ab_conv_x, ab_rg_wr, ab_rg_br, ab_rg_wi, ab_rg_bi, ab_rg_lam, ab_w_out, at_w_qkv, at_q_norm, at_k_norm, at_w_out, moe_w_grp, moe_b_grp, moe_w_exp, moe_b_exp, moe_w_gate, moe_w_up, moe_w_down):
    nb, t_lat, _ = x.shape
    c_rows = ctx.shape[1]
    p_rows = c_rows + t_lat
    nt = p_rows // TM
    nc = c_rows // TM
    depth = ada_w.shape[0]
    assert depth == 2 and nb < 16 and c_rows % TM == 0 and t_lat % TM == 0

    cond = jnp.concatenate([c, c_ctx[None], jnp.zeros((16 - nb - 1, D_MODEL), F32)], axis=0)
    mods = _ada_mod(cond, ada_w, ada_b)
    xu = jnp.concatenate([ctx, x], axis=1).reshape(nb * p_rows, D_MODEL)
    uni_mrow = lambda i: jnp.where(i % nt < nc, nb, i // nt)
    fnorm = final_norm.reshape(1, D_MODEL)
    w_gate = moe_w_gate.reshape(depth * N_EXPERTS, D_MODEL, D_EXPERT)
    w_up = moe_w_up.reshape(depth * N_EXPERTS, D_MODEL, D_EXPERT)
    w_down = moe_w_down.reshape(depth * N_EXPERTS, D_EXPERT, D_MODEL)

    w_pad, alog_vec, dt_vec, wbd, bbd, lam = _ab_params(
        ab_w_in[0], ab_conv_qkv[0], ab_a_log[0], ab_dt_bias[0], ab_conv_x[0],
        ab_rg_wr[0], ab_rg_br[0], ab_rg_wi[0], ab_rg_bi[0], ab_rg_lam[0])
    q, k, v, gout, yb, gates, a0, b0, a1, b1 = _inproj_ab(
        xu, mods[0], norm1[0].reshape(1, D_MODEL), w_pad, ab_conv_qkv[0], ab_conv_x[0],
        alog_vec, dt_vec, wbd, bbd, lam, nb, nt, nc)
    o_f, o_b = _delta(q, k, v, gates, nb, p_rows, c_rows)
    h_f, h_b = _lru(a0, b0, a1, b1, nb, nt, nc)
    xu = _merge_ab(o_f, o_b, h_f, h_b, gout, yb, xu, mods[0], ab_onorm[0].reshape(1, A_DK),
                   ab_w_out[0].astype(BF16), nb, nt, nc)
    xu = _moe_block(xu, mods[0], norm2[0].reshape(1, D_MODEL), moe_w_grp[0], moe_b_grp[0], moe_w_exp[0],
                    moe_b_exp[0], w_gate, w_up, w_down, 0, fnorm, uni_mrow, False)

    w_perm, qn, kn = _attn_params(at_w_qkv[0], at_q_norm[0], at_k_norm[0], t_lat, c_rows)
    cos_tab, sin_tab = _rope_tables(t_lat, c_rows, GRID_W)
    q, k, v = _inproj_attn(xu, mods[1], norm1[1].reshape(1, D_MODEL), w_perm, qn, kn, cos_tab, sin_tab,
                           nb, nt, nc)
    att = _attention(q, k, v, nb, nt, nc, p_rows)
    xl = _outproj_lat(att, xu, mods[1], at_w_out[0].astype(BF16), nt, nc)
    nq = nt - nc
    xl = _moe_block(xl, mods[1], norm2[1].reshape(1, D_MODEL), moe_w_grp[1], moe_b_grp[1], moe_w_exp[1],
                    moe_b_exp[1], w_gate, w_up, w_down, 1, fnorm, lambda j: j // nq, True)
    return xl.reshape(nb, t_lat, D_MODEL)
```

```python
import functools
import math

import jax
import jax.numpy as jnp
import numpy as np
from jax import lax
from jax.experimental import pallas as pl
from jax.experimental.pallas import tpu as pltpu

F32 = jnp.float32
BF16 = jnp.bfloat16
HIGHEST = lax.Precision.HIGHEST

D_MODEL = 1024
EPS = 1e-6
TM = 256
LANES = 128
SUBLANES = 8

A_HEADS = 4
A_DK = 128
A_WIDTH = A_HEADS * A_DK
CHUNK = 64
CONV_W = 4
B_WIDTH = 512
B_BLOCKS = 8
B_BLK = B_WIDTH // B_BLOCKS
RG_C = 8.0

C_HEADS = 8
C_KV_HEADS = 2
C_HD = 128
C_GRP = C_HEADS // C_KV_HEADS
ROPE_THETA = 10000.0

N_GROUPS = 4
EXP_PER_GROUP = 8
N_EXPERTS = N_GROUPS * EXP_PER_GROUP
D_EXPERT = 512

VMEM_LIMIT = 56 * 1024 * 1024

ZC_QKV = 0
ZC_XB = 3 * A_WIDTH
ZC_GOUT = ZC_XB + B_WIDTH
ZC_YB = ZC_GOUT + A_WIDTH
ZC_GATE = ZC_YB + B_WIDTH
ZC_TOTAL = ZC_GATE + LANES
ZC_CONV = ZC_GOUT
HALO = SUBLANES


def _cparams(sem):
    return pltpu.CompilerParams(dimension_semantics=sem, vmem_limit_bytes=VMEM_LIMIT)


def _sigmoid(x):
    return jax.nn.sigmoid(x)


def _silu(x):
    return x * jax.nn.sigmoid(x)


def _softplus(x):
    return jnp.maximum(x, 0.0) + jnp.log1p(jnp.exp(-jnp.abs(x)))


def _gelu_tanh(x):
    c = math.sqrt(2.0 / math.pi)
    return 0.5 * x * (1.0 + jnp.tanh(c * (x + 0.044715 * (x * x * x))))


def _dot_split(a, b):
    a_hi = a.astype(BF16)
    b_hi = b.astype(BF16)
    a_lo = (a - a_hi.astype(F32)).astype(BF16)
    b_lo = (b - b_hi.astype(F32)).astype(BF16)
    dot = functools.partial(jnp.dot, preferred_element_type=F32)
    return dot(a_hi, b_hi) + (dot(a_hi, b_lo) + dot(a_lo, b_hi))


def _modulate(x, gain, shift, scale):
    y = x * lax.rsqrt(jnp.mean(x * x, axis=-1, keepdims=True) + EPS)
    return (y * gain) * (1.0 + scale) + shift


def _ada_kernel(cond_ref, w_ref, b_ref, o_ref):
    s = _silu(cond_ref[...]).astype(BF16)
    o_ref[...] = jnp.dot(s, w_ref[...].astype(BF16), preferred_element_type=F32) + b_ref[...]


def _ada_mod(cond, ada_w, ada_b):
    depth = ada_w.shape[0]
    tn = 1536
    nn = 6 * D_MODEL // tn
    out = pl.pallas_call(
        _ada_kernel,
        grid=(depth, nn),
        in_specs=[
            pl.BlockSpec((16, D_MODEL), lambda l, n: (0, 0)),
            pl.BlockSpec((None, D_MODEL, tn), lambda l, n: (l, 0, n)),
            pl.BlockSpec((None, 1, tn), lambda l, n: (l, 0, n)),
        ],
        out_specs=pl.BlockSpec((None, 16, tn), lambda l, n: (l, 0, n)),
        out_shape=jax.ShapeDtypeStruct((depth, 16, 6 * D_MODEL), F32),
        compiler_params=_cparams(("arbitrary", "arbitrary")),
        name="ada_mod",
    )(cond, ada_w, ada_b.reshape(depth, 1, 6 * D_MODEL))
    return out.reshape(depth, 16, 6, D_MODEL)


def _inproj_ab_kernel(nt, nc, xc_ref, xp_ref, xn_ref, mod_ref, n1_ref, w_ref, cq_ref, cx_ref,
                      alog_ref, dt_ref, wbd_ref, bbd_ref, lam_ref,
                      q_ref, k_ref, v_ref, go_ref, yb_ref, g_ref, a0_ref, b0_ref, a1_ref, b1_ref,
                      zbuf, xcbuf):
    i = pl.program_id(0)
    r = i % nt
    is_ctx = r < nc
    prev_ok = jnp.logical_and(r > 0, ((r - 1) < nc) == is_ctx)
    next_ok = jnp.logical_and(r < nt - 1, ((r + 1) < nc) == is_ctx)

    gain = n1_ref[...]
    shift = mod_ref[0:1, :]
    scale = mod_ref[1:2, :]
    xall = jnp.concatenate([xp_ref[...], xc_ref[...], xn_ref[...]], axis=0)
    h = _modulate(xall, gain, shift, scale).astype(BF16)
    zbuf[...] = jnp.dot(h, w_ref[...], preferred_element_type=F32)
    zbuf[0:HALO, 0:ZC_CONV] = jnp.where(prev_ok, zbuf[0:HALO, 0:ZC_CONV], 0.0)
    zbuf[HALO + TM:, 0:ZC_CONV] = jnp.where(next_ok, zbuf[HALO + TM:, 0:ZC_CONV], 0.0)

    def conv(c0, w_taps_ref, wc0):
        acc = None
        for j in range(CONV_W):
            start = HALO - CONV_W // 2 + j
            term = zbuf[start:start + TM, c0:c0 + LANES] * w_taps_ref[j:j + 1, wc0:wc0 + LANES]
            acc = term if acc is None else acc + term
        return acc

    outs = (q_ref, k_ref, v_ref)
    for s in range(3 * A_HEADS):
        y = _silu(conv(s * LANES, cq_ref, s * LANES))
        if s < 2 * A_HEADS:
            y = y * lax.rsqrt(jnp.sum(y * y, axis=-1, keepdims=True) + EPS)
        if s < A_HEADS:
            y = y * (A_DK ** -0.5)
        hh = s % A_HEADS
        outs[s // A_HEADS][:, hh * LANES:(hh + 1) * LANES] = y

    for s in range(B_WIDTH // LANES):
        xcbuf[:, s * LANES:(s + 1) * LANES] = conv(ZC_XB + s * LANES, cx_ref, s * LANES)

    go_ref[...] = zbuf[HALO:HALO + TM, ZC_GOUT:ZC_GOUT + A_WIDTH]
    yb_ref[...] = zbuf[HALO:HALO + TM, ZC_YB:ZC_YB + B_WIDTH]

    zg = zbuf[HALO:HALO + TM, ZC_GATE:ZC_GATE + LANES]
    lane = lax.broadcasted_iota(jnp.int32, (TM, LANES), 1)
    dec = -jnp.exp(alog_ref[...]) * _softplus(zg + dt_ref[...])
    g_ref[...] = jnp.where(jnp.bitwise_and(lane, 7) < A_HEADS, dec, _sigmoid(zg))

    xc = xcbuf[...]
    rg = jnp.dot(xc.astype(BF16), wbd_ref[...], preferred_element_type=F32) + bbd_ref[...]
    sp = _softplus(-lam_ref[...])
    for d, (a_ref, b_ref) in enumerate(((a0_ref, b0_ref), (a1_ref, b1_ref))):
        rr = _sigmoid(rg[:, d * B_WIDTH:(d + 1) * B_WIDTH])
        ii = _sigmoid(rg[:, (2 + d) * B_WIDTH:(3 + d) * B_WIDTH])
        a = jnp.exp(-RG_C * rr * sp[:, d * B_WIDTH:(d + 1) * B_WIDTH])
        a_ref[...] = a
        b_ref[...] = jnp.sqrt(1.0 - a * a) * ii * xc


def _inproj_ab(xu, mods, norm1, w_pad, conv_qkv, conv_x, alog_vec, dt_vec, wbd, bbd, lam, nb, nt, nc):
    rows = xu.shape[0]
    ntiles = rows // TM
    hb = TM // HALO
    nhalo = rows // HALO

    def mrow(i):
        return jnp.where(i % nt < nc, nb, i // nt)

    full = lambda shape: pl.BlockSpec(shape, lambda i: (0,) * len(shape))
    row_spec = lambda w: pl.BlockSpec((TM, w), lambda i: (i, 0))
    sds = lambda w: jax.ShapeDtypeStruct((rows, w), F32)
    return pl.pallas_call(
        functools.partial(_inproj_ab_kernel, nt, nc),
        grid=(ntiles,),
        in_specs=[
            row_spec(D_MODEL),
            pl.BlockSpec((HALO, D_MODEL), lambda i: (jnp.maximum(i * hb - 1, 0), 0)),
            pl.BlockSpec((HALO, D_MODEL), lambda i: (jnp.minimum((i + 1) * hb, nhalo - 1), 0)),
            pl.BlockSpec((None, 6, D_MODEL), lambda i: (mrow(i), 0, 0)),
            full((1, D_MODEL)),
            full((D_MODEL, ZC_TOTAL)),
            full((CONV_W, 3 * A_WIDTH)),
            full((CONV_W, B_WIDTH)),
            full((1, LANES)),
            full((1, LANES)),
            full((B_WIDTH, 4 * B_WIDTH)),
            full((1, 4 * B_WIDTH)),
            full((1, 2 * B_WIDTH)),
        ],
        out_specs=[row_spec(A_WIDTH)] * 5 + [row_spec(LANES)] + [row_spec(B_WIDTH)] * 4,
        out_shape=[sds(A_WIDTH)] * 5 + [sds(LANES)] + [sds(B_WIDTH)] * 4,
        scratch_shapes=[pltpu.VMEM((TM + 2 * HALO, ZC_TOTAL), F32), pltpu.VMEM((TM, B_WIDTH), F32)],
        compiler_params=_cparams(("arbitrary",)),
        name="inproj_ab",
    )(xu, xu, xu, mods, norm1, w_pad, conv_qkv, conv_x, alog_vec, dt_vec, wbd, bbd, lam)


PAIR = 2 * CHUNK
N_STREAMS = 4
PREP_CHUNKS = 4
PREP_PROBLEMS = PREP_CHUNKS * N_STREAMS
SCAN_BATCH_MAX = 8
DIAG_BLK = 8


def _delta_prep_kernel(q_ref, k_ref, v_ref, g_ref, u_ref, wq_ref, at_ref, kt_ref, aux_ref,
                       gcum_s, dec_s, kq_s, rhs_s, n_s, pwf_s, l_s):
    ii = lax.broadcasted_iota(jnp.int32, (PAIR, PAIR), 0)
    jj = lax.broadcasted_iota(jnp.int32, (PAIR, PAIR), 1)
    same = (ii < CHUNK) == (jj < CHUNK)
    ci = lax.broadcasted_iota(jnp.int32, (CHUNK, CHUNK), 0)
    cj = lax.broadcasted_iota(jnp.int32, (CHUNK, CHUNK), 1)
    aux_ref[...] = jnp.zeros_like(aux_ref)
    problems = [(c, d, p) for c in range(PREP_CHUNKS) for d in range(2) for p in range(2)]

    for c in range(PREP_CHUNKS):
        gates = g_ref[c * CHUNK:(c + 1) * CHUNK, :]
        for d in range(2):
            tri = (ci >= cj) if d == 0 else (ci <= cj)
            gcum_s[2 * c + d] = jnp.dot(tri.astype(F32), gates, precision=HIGHEST, preferred_element_type=F32)

    for n, (c, d, p) in enumerate(problems):
        x = 2 * d + p
        rows = slice(c * CHUNK, (c + 1) * CHUNK)
        heads = (2 * p, 2 * p + 1)
        lanes = [d * 2 * A_HEADS + h for h in heads]
        last = CHUNK - 1 if d == 0 else 0
        lower = jnp.logical_and(same, (ii >= jj) if d == 0 else (ii <= jj))
        stack = lambda ref: jnp.concatenate([ref[rows, h * A_DK:(h + 1) * A_DK] for h in heads], axis=0)
        gcum = gcum_s[2 * c + d]
        gates = g_ref[rows, :]
        gc_col = jnp.concatenate([gcum[:, l:l + 1] for l in lanes], axis=0)
        beta = jnp.concatenate([gates[:, l + A_HEADS:l + A_HEADS + 1] for l in lanes], axis=0)
        g_last = [gcum[last:last + 1, l:l + 1] for l in lanes]
        gl_col = jnp.concatenate([jnp.broadcast_to(g, (CHUNK, 1)) for g in g_last], axis=0)
        gc_mat = jnp.broadcast_to(gc_col, (PAIR, PAIR))
        dec_s[n] = jnp.where(lower, jnp.exp(jnp.where(lower, gc_mat - gc_mat.T, 0.0)), 0.0)
        eg = jnp.exp(gc_col)
        q = stack(q_ref)
        k = stack(k_ref)
        kb = k * beta
        kq_s[n] = lax.dot_general(jnp.concatenate([kb, q], axis=0).astype(BF16), k.astype(BF16),
                                  (((1,), (1,)), ((), ())), preferred_element_type=F32)
        rhs_s[n] = jnp.concatenate([stack(v_ref) * beta, kb * eg], axis=-1)
        wq_ref[x, (2 * c + 1) * PAIR:(2 * c + 2) * PAIR, :] = (q * eg).astype(BF16)
        kt_ref[x, c * PAIR:(c + 1) * PAIR, :] = (k * jnp.exp(gl_col - gc_col)).T.astype(BF16)
        aux_ref[c * SUBLANES + x:c * SUBLANES + x + 1, :] = jnp.concatenate(
            [jnp.broadcast_to(jnp.exp(g), (1, A_DK)) for g in g_last], axis=-1)

    blk = lambda s: jnp.right_shift(ii, s) == jnp.right_shift(jj, s)
    diag_sh = int(math.log2(DIAG_BLK))
    for n, (c, d, p) in enumerate(problems):
        strict = jnp.logical_and(same, (ii > jj) if d == 0 else (ii < jj))
        dec = dec_s[n]
        lmat = jnp.where(strict, kq_s[n, :PAIR] * dec, 0.0)
        l_s[n] = lmat
        neg_d = jnp.where(blk(diag_sh), -lmat, 0.0)
        n_s[n] = neg_d
        pwf_s[n] = neg_d
        at_ref[2 * d + p, c * PAIR:(c + 1) * PAIR, :] = (kq_s[n, PAIR:] * dec).astype(BF16)

    for _ in range(diag_sh - 1):
        for n in range(PREP_PROBLEMS):
            pw = pwf_s[n]
            pwf_s[n] = _dot_split(pw, pw)
        for n in range(PREP_PROBLEMS):
            nm = n_s[n]
            pw = pwf_s[n]
            n_s[n] = nm + pw + _dot_split(nm, pw)

    for s in range(diag_sh, int(math.log2(CHUNK))):
        off = jnp.logical_and(blk(s + 1), jnp.logical_not(blk(s)))
        for n in range(PREP_PROBLEMS):
            l_off = jnp.where(off, l_s[n], 0.0)
            pwf_s[n] = l_off + jnp.dot(l_off.astype(BF16), n_s[n].astype(BF16), preferred_element_type=F32)
        for n in range(PREP_PROBLEMS):
            nm = n_s[n]
            xm = pwf_s[n]
            n_s[n] = nm - xm - jnp.dot(nm.astype(BF16), xm.astype(BF16), preferred_element_type=F32)

    for n, (c, d, p) in enumerate(problems):
        x = 2 * d + p
        rhs = rhs_s[n]
        sol = rhs + jnp.dot(n_s[n].astype(BF16), rhs.astype(BF16), preferred_element_type=F32)
        u_ref[x, c * PAIR:(c + 1) * PAIR, :] = sol[:, :A_DK]
        wq_ref[x, 2 * c * PAIR:(2 * c + 1) * PAIR, :] = sol[:, A_DK:].astype(BF16)


def _delta_scan_kernel(scan_batch, *refs):
    ins = refs[:4 * N_STREAMS]
    auxf_ref, auxb_ref, of_ref, ob_ref, s_ref, r_s, vbd_s = refs[4 * N_STREAMS:]
    insts = [(bb, x) for bb in range(scan_batch) for x in range(N_STREAMS)]

    @pl.when(pl.program_id(1) == 0)
    def _():
        s_ref[...] = jnp.zeros_like(s_ref)

    zeros = jnp.zeros((CHUNK, A_DK), F32)
    for n, (bb, x) in enumerate(insts):
        wq_ref = ins[4 * x + 1]
        r_s[n] = jnp.dot(wq_ref[bb], s_ref[n].astype(BF16), preferred_element_type=F32)
    for n, (bb, x) in enumerate(insts):
        d, p = divmod(x, 2)
        u_ref, _, at_ref, _ = ins[4 * x:4 * x + 4]
        o_ref = of_ref if d == 0 else ob_ref
        vn_a = u_ref[bb, :CHUNK, :] - r_s[n, :CHUNK, :A_DK]
        vn_b = u_ref[bb, CHUNK:, :] - r_s[n, CHUNK:PAIR, A_DK:]
        av = jnp.dot(at_ref[bb], jnp.concatenate([vn_a, vn_b], axis=0).astype(BF16),
                     preferred_element_type=F32)
        o_ref[bb, :, 2 * p * A_DK:(2 * p + 1) * A_DK] = r_s[n, PAIR:PAIR + CHUNK, :A_DK] + av[:CHUNK]
        o_ref[bb, :, (2 * p + 1) * A_DK:(2 * p + 2) * A_DK] = r_s[n, PAIR + CHUNK:, A_DK:] + av[CHUNK:]
        vbd_s[n] = jnp.concatenate([jnp.concatenate([vn_a, zeros], axis=1),
                                    jnp.concatenate([zeros, vn_b], axis=1)], axis=0).astype(BF16)
    for n, (bb, x) in enumerate(insts):
        aux_ref = auxf_ref if x < 2 else auxb_ref
        kt_ref = ins[4 * x + 3]
        s_ref[n] = s_ref[n] * aux_ref[bb, x:x + 1, :] + jnp.dot(kt_ref[bb], vbd_s[n],
                                                               preferred_element_type=F32)


def _bwd_order(s, n_ctx, n_all):
    return jnp.where(s < n_ctx, n_ctx - 1 - s, n_all - 1 - (s - n_ctx))


def _delta(q, k, v, g, nb, p_rows, c_rows):
    rows = q.shape[0]
    nchunks = rows // CHUNK
    pr = PREP_CHUNKS * CHUNK
    row_spec = lambda w: pl.BlockSpec((pr, w), lambda i: (i, 0))
    stream_spec = lambda m: pl.BlockSpec((N_STREAMS, PREP_CHUNKS * m, A_DK), lambda i: (0, i, 0))
    stream_sds = lambda m, dt: jax.ShapeDtypeStruct((N_STREAMS, nchunks * m, A_DK), dt)
    u, wq, at, kt, aux = pl.pallas_call(
        _delta_prep_kernel,
        grid=(nchunks // PREP_CHUNKS,),
        in_specs=[row_spec(A_WIDTH)] * 3 + [row_spec(LANES)],
        out_specs=[stream_spec(PAIR), stream_spec(2 * PAIR), stream_spec(PAIR), stream_spec(PAIR),
                   pl.BlockSpec((PREP_CHUNKS * SUBLANES, 2 * A_DK), lambda i: (i, 0))],
        out_shape=[stream_sds(PAIR, F32), stream_sds(2 * PAIR, BF16), stream_sds(PAIR, BF16),
                   stream_sds(PAIR, BF16), jax.ShapeDtypeStruct((nchunks * SUBLANES, 2 * A_DK), F32)],
        scratch_shapes=[
            pltpu.VMEM((2 * PREP_CHUNKS, CHUNK, LANES), F32),
            pltpu.VMEM((PREP_PROBLEMS, PAIR, PAIR), F32),
            pltpu.VMEM((PREP_PROBLEMS, 2 * PAIR, PAIR), F32),
            pltpu.VMEM((PREP_PROBLEMS, PAIR, 2 * A_DK), F32),
            pltpu.VMEM((PREP_PROBLEMS, PAIR, PAIR), F32),
            pltpu.VMEM((PREP_PROBLEMS, PAIR, PAIR), F32),
            pltpu.VMEM((PREP_PROBLEMS, PAIR, PAIR), F32),
        ],
        compiler_params=_cparams(("arbitrary",)),
        name="delta_prep",
    )(q, k, v, g)

    sb = math.gcd(nb, SCAN_BATCH_MAX)
    ng = nb // sb
    n_all = p_rows // CHUNK
    n_ctx = c_rows // CHUNK
    order = (lambda s: s, lambda s: _bwd_order(s, n_ctx, n_all))
    view = lambda arr, m: arr.reshape(N_STREAMS, ng, sb, n_all * m, A_DK)
    in_specs, args = [], []
    for x in range(N_STREAMS):
        pos = order[x // 2]
        for arr, m in ((u, PAIR), (wq, 2 * PAIR), (at, PAIR), (kt, PAIR)):
            in_specs.append(pl.BlockSpec((None, None, sb, m, A_DK),
                                         lambda b, s, x=x, pos=pos: (x, b, 0, pos(s), 0)))
            args.append(view(arr, m))
    for pos in order:
        in_specs.append(pl.BlockSpec((None, sb, SUBLANES, 2 * A_DK), lambda b, s, pos=pos: (b, 0, pos(s), 0)))
        args.append(aux.reshape(ng, sb, n_all * SUBLANES, 2 * A_DK))
    out_spec = lambda pos: pl.BlockSpec((None, sb, CHUNK, A_WIDTH), lambda b, s: (b, 0, pos(s), 0))
    n_inst = sb * N_STREAMS
    o_f, o_b = pl.pallas_call(
        functools.partial(_delta_scan_kernel, sb),
        grid=(ng, n_all),
        in_specs=in_specs,
        out_specs=[out_spec(order[0]), out_spec(order[1])],
        out_shape=[jax.ShapeDtypeStruct((ng, sb, p_rows, A_WIDTH), F32)] * 2,
        scratch_shapes=[
            pltpu.VMEM((n_inst, A_DK, 2 * A_DK), F32),
            pltpu.VMEM((n_inst, 2 * PAIR, 2 * A_DK), F32),
            pltpu.VMEM((n_inst, PAIR, 2 * A_DK), BF16),
        ],
        compiler_params=_cparams(("arbitrary", "arbitrary")),
        name="delta_scan",
    )(*args)
    return o_f.reshape(rows, A_WIDTH), o_b.reshape(rows, A_WIDTH)


def _lru_kernel(a0_ref, b0_ref, a1_ref, b1_ref, hf_ref, hb_ref, carry_ref):
    @pl.when(pl.program_id(1) == 0)
    def _():
        carry_ref[...] = jnp.zeros_like(carry_ref)

    row = lax.broadcasted_iota(jnp.int32, (SUBLANES, B_WIDTH), 0)
    ngroups = TM // SUBLANES

    def scan_group(a_ref, b_ref, h_ref, r0, h_in, reverse):
        a = a_ref[pl.ds(r0, SUBLANES), :]
        b = b_ref[pl.ds(r0, SUBLANES), :]
        for sft in (1, 2, 4):
            shift = SUBLANES - sft if reverse else sft
            keep = (row < SUBLANES - sft) if reverse else (row >= sft)
            a_sh = pltpu.roll(a, shift, axis=0)
            b_sh = pltpu.roll(b, shift, axis=0)
            b = jnp.where(keep, a * b_sh + b, b)
            a = jnp.where(keep, a * a_sh, a)
        hrows = a * h_in + b
        h_ref[pl.ds(r0, SUBLANES), :] = hrows
        return hrows[0:1, :] if reverse else hrows[SUBLANES - 1:SUBLANES, :]

    def fwd_body(t, h_in):
        r0 = pl.multiple_of(t * SUBLANES, SUBLANES)
        return scan_group(a0_ref, b0_ref, hf_ref, r0, h_in, False)

    def bwd_body(t, h_in):
        r0 = pl.multiple_of((ngroups - 1 - t) * SUBLANES, SUBLANES)
        return scan_group(a1_ref, b1_ref, hb_ref, r0, h_in, True)

    carry_ref[0:1, :] = lax.fori_loop(0, ngroups, fwd_body, carry_ref[0:1, :])
    carry_ref[1:2, :] = lax.fori_loop(0, ngroups, bwd_body, carry_ref[1:2, :])


def _lru(a0, b0, a1, b1, nb, nt, nc):
    rows = a0.shape[0]
    fwd = pl.BlockSpec((TM, B_WIDTH), lambda b, s: (b * nt + s, 0))
    bwd = pl.BlockSpec((TM, B_WIDTH), lambda b, s: (b * nt + _bwd_order(s, nc, nt), 0))
    return pl.pallas_call(
        _lru_kernel,
        grid=(nb, nt),
        in_specs=[fwd, fwd, bwd, bwd],
        out_specs=[fwd, bwd],
        out_shape=[jax.ShapeDtypeStruct((rows, B_WIDTH), F32)] * 2,
        scratch_shapes=[pltpu.VMEM((SUBLANES, B_WIDTH), F32)],
        compiler_params=_cparams(("arbitrary", "arbitrary")),
        name="rg_lru_scan",
    )(a0, b0, a1, b1)


def _merge_ab_kernel(of_ref, ob_ref, hf_ref, hb_ref, go_ref, yb_ref, x_ref, mod_ref, on_ref, w_ref, xo_ref):
    parts = []
    for h in range(A_HEADS):
        sl = slice(h * A_DK, (h + 1) * A_DK)
        o = of_ref[:, sl] + ob_ref[:, sl]
        n = o * lax.rsqrt(jnp.mean(o * o, axis=-1, keepdims=True) + EPS) * on_ref[...]
        parts.append((n * _silu(go_ref[:, sl])).astype(BF16))
    parts.append(((hf_ref[...] + hb_ref[...]) * _gelu_tanh(yb_ref[...])).astype(BF16))
    cat = jnp.concatenate(parts, axis=-1)
    y = jnp.dot(cat, w_ref[...], preferred_element_type=F32)
    xo_ref[...] = x_ref[...] + mod_ref[2:3, :] * y


def _merge_ab(o_f, o_b, h_f, h_b, gout, yb, xu, mods, onorm, w_out, nb, nt, nc):
    rows = xu.shape[0]
    half = pl.BlockSpec((TM, A_WIDTH), lambda i: (i, 0))
    wide = pl.BlockSpec((TM, D_MODEL), lambda i: (i, 0))
    mrow = lambda i: jnp.where(i % nt < nc, nb, i // nt)
    return pl.pallas_call(
        _merge_ab_kernel,
        grid=(rows // TM,),
        in_specs=[half] * 6 + [
            wide,
            pl.BlockSpec((None, 6, D_MODEL), lambda i: (mrow(i), 0, 0)),
            pl.BlockSpec((1, A_DK), lambda i: (0, 0)),
            pl.BlockSpec((D_MODEL, D_MODEL), lambda i: (0, 0)),
        ],
        out_specs=wide,
        out_shape=jax.ShapeDtypeStruct((rows, D_MODEL), F32),
        compiler_params=_cparams(("arbitrary",)),
        name="merge_ab",
    )(o_f, o_b, h_f, h_b, gout, yb, xu, mods, onorm, w_out)


def _inproj_attn_kernel(x_ref, mod_ref, n1_ref, w_ref, qn_ref, kn_ref, cos_ref, sin_ref, q_ref, k_ref, v_ref):
    h = _modulate(x_ref[...], n1_ref[...], mod_ref[0:1, :], mod_ref[1:2, :]).astype(BF16)
    z = jnp.dot(h, w_ref[...], preferred_element_type=F32)
    cos = cos_ref[...]
    sin = sin_ref[...]
    for hd in range(C_HEADS + C_KV_HEADS):
        xh = z[:, hd * C_HD:(hd + 1) * C_HD]
        gain = qn_ref[...] if hd < C_HEADS else kn_ref[...]
        n = xh * lax.rsqrt(jnp.mean(xh * xh, axis=-1, keepdims=True) + EPS) * gain
        rot = n * cos + pltpu.roll(n, C_HD // 2, axis=1) * sin
        if hd < C_HEADS:
            q_ref[:, hd * C_HD:(hd + 1) * C_HD] = (rot * (C_HD ** -0.5)).astype(BF16)
        else:
            kh = hd - C_HEADS
            k_ref[:, kh * C_HD:(kh + 1) * C_HD] = rot.astype(BF16)
    v_ref[...] = z[:, (C_HEADS + C_KV_HEADS) * C_HD:].astype(BF16)


def _inproj_attn(xu, mods, norm1, w_perm, qn, kn, cos_tab, sin_tab, nb, nt, nc):
    rows = xu.shape[0]
    nqk = (C_HEADS + 2 * C_KV_HEADS) * C_HD
    mrow = lambda i: jnp.where(i % nt < nc, nb, i // nt)
    full = lambda shape: pl.BlockSpec(shape, lambda i: (0,) * len(shape))
    tab = pl.BlockSpec((TM, C_HD), lambda i: (i % nt, 0))
    return pl.pallas_call(
        _inproj_attn_kernel,
        grid=(rows // TM,),
        in_specs=[
            pl.BlockSpec((TM, D_MODEL), lambda i: (i, 0)),
            pl.BlockSpec((None, 6, D_MODEL), lambda i: (mrow(i), 0, 0)),
            full((1, D_MODEL)),
            full((D_MODEL, nqk)),
            full((1, C_HD)),
            full((1, C_HD)),
            tab, tab,
        ],
        out_specs=[
            pl.BlockSpec((TM, C_HEADS * C_HD), lambda i: (i, 0)),
            pl.BlockSpec((TM, C_KV_HEADS * C_HD), lambda i: (i, 0)),
            pl.BlockSpec((TM, C_KV_HEADS * C_HD), lambda i: (i, 0)),
        ],
        out_shape=[
            jax.ShapeDtypeStruct((rows, C_HEADS * C_HD), BF16),
            jax.ShapeDtypeStruct((rows, C_KV_HEADS * C_HD), BF16),
            jax.ShapeDtypeStruct((rows, C_KV_HEADS * C_HD), BF16),
        ],
        compiler_params=_cparams(("arbitrary",)),
        name="inproj_attn",
    )(xu, mods, norm1, w_perm, qn, kn, cos_tab, sin_tab)


def _attn_kernel(q_ref, k_ref, v_ref, o_ref):
    k = k_ref[...]
    v = v_ref[...]
    for g in range(C_GRP):
        sl = slice(g * C_HD, (g + 1) * C_HD)
        s = lax.dot_general(q_ref[:, sl], k, (((1,), (1,)), ((), ())), preferred_element_type=F32)
        p = jnp.exp(s - jnp.max(s, axis=-1, keepdims=True))
        denom = jnp.sum(p, axis=-1, keepdims=True)
        o = jnp.dot(p.astype(BF16), v, preferred_element_type=F32) / denom
        o_ref[:, sl] = o.astype(BF16)


def _attention(q, k, v, nb, nt, nc, p_rows):
    nq = nt - nc
    gw = C_GRP * C_HD
    return pl.pallas_call(
        _attn_kernel,
        grid=(nb, C_KV_HEADS, nq),
        in_specs=[
            pl.BlockSpec((TM, gw), lambda b, h, t: (b * nt + nc + t, h)),
            pl.BlockSpec((p_rows, C_HD), lambda b, h, t: (b, h)),
            pl.BlockSpec((p_rows, C_HD), lambda b, h, t: (b, h)),
        ],
        out_specs=pl.BlockSpec((TM, gw), lambda b, h, t: (b * nq + t, h)),
        out_shape=jax.ShapeDtypeStruct((nb * nq * TM, C_HEADS * C_HD), BF16),
        compiler_params=_cparams(("arbitrary", "arbitrary", "arbitrary")),
        name="gqa_attention",
    )(q, k, v)


def _outproj_kernel(a_ref, x_ref, mod_ref, w_ref, xo_ref):
    y = jnp.dot(a_ref[...], w_ref[...], preferred_element_type=F32)
    xo_ref[...] = x_ref[...] + mod_ref[2:3, :] * y


def _outproj_lat(a, xu, mods, w_out, nt, nc):
    rows = a.shape[0]
    nq = nt - nc
    return pl.pallas_call(
        _outproj_kernel,
        grid=(rows // TM,),
        in_specs=[
            pl.BlockSpec((TM, D_MODEL), lambda j: (j, 0)),
            pl.BlockSpec((TM, D_MODEL), lambda j: ((j // nq) * nt + nc + j % nq, 0)),
            pl.BlockSpec((None, 6, D_MODEL), lambda j: (j // nq, 0, 0)),
            pl.BlockSpec((D_MODEL, D_MODEL), lambda j: (0, 0)),
        ],
        out_specs=pl.BlockSpec((TM, D_MODEL), lambda j: (j, 0)),
        out_shape=jax.ShapeDtypeStruct((rows, D_MODEL), F32),
        compiler_params=_cparams(("arbitrary",)),
        name="outproj_attn",
    )(a, xu, mods, w_out)


TE = 256
GATHER_DEPTH = 4
ROW_TILES = D_MODEL // LANES


def _store_token_tiles(ref, val):
    for s in range(ROW_TILES):
        ref[pl.ds(s, val.shape[0], stride=ROW_TILES), :] = val[:, s * LANES:(s + 1) * LANES]


def _load_token_tiles(ref, s, rows):
    return ref[pl.ds(s, rows, stride=ROW_TILES), :]


def _token_tile(ref, row):
    return ref.at[pl.ds(pl.multiple_of(row * ROW_TILES, ROW_TILES), ROW_TILES), :]
SEL_E1, SEL_E2, SEL_W1, SEL_W2, SEL_R1, SEL_R2 = 0, 1, 2, 3, 4, 5


def _route_kernel(x_ref, mod_ref, n2_ref, wr_ref, br_ref, h_ref, sel_ref, cnt_ref, run_s):
    @pl.when(pl.program_id(0) == 0)
    def _():
        run_s[...] = jnp.zeros_like(run_s)

    h = _modulate(x_ref[...], n2_ref[...], mod_ref[3:4, :], mod_ref[4:5, :])
    _store_token_tiles(h_ref, h)
    lg = jnp.dot(h, wr_ref[...], precision=HIGHEST, preferred_element_type=F32) + br_ref[...]
    lane = lax.broadcasted_iota(jnp.int32, lg.shape, 1)
    neg = jnp.float32(-jnp.inf)
    big = jnp.int32(1 << 20)
    is_grp = jnp.logical_and(lane >= N_EXPERTS, lane < N_EXPERTS + N_GROUPS)
    gl = jnp.where(is_grp, lg, neg)
    gmax = jnp.max(gl, axis=-1, keepdims=True)
    gidx = jnp.min(jnp.where(gl == gmax, lane - N_EXPERTS, big), axis=-1, keepdims=True)
    g_w = 1.0 / jnp.sum(jnp.where(is_grp, jnp.exp(gl - gmax), 0.0), axis=-1, keepdims=True)
    in_grp = jnp.logical_and(lane < N_EXPERTS, jnp.right_shift(lane, 3) == gidx)
    e1 = jnp.where(in_grp, lg, neg)
    m1 = jnp.max(e1, axis=-1, keepdims=True)
    i1 = jnp.min(jnp.where(e1 == m1, lane, big), axis=-1, keepdims=True)
    e2 = jnp.where(lane == i1, neg, e1)
    m2 = jnp.max(e2, axis=-1, keepdims=True)
    i2 = jnp.min(jnp.where(e2 == m2, lane, big), axis=-1, keepdims=True)
    t = jnp.exp(m2 - m1)
    w1 = g_w / (1.0 + t)
    w2 = g_w * t / (1.0 + t)

    hit1 = lane == i1
    hit2 = lane == i2
    onehot = jnp.where(hit1, 1.0, 0.0) + jnp.where(hit2, 1.0, 0.0)
    ri = lax.broadcasted_iota(jnp.int32, (TM, TM), 0)
    rj = lax.broadcasted_iota(jnp.int32, (TM, TM), 1)
    earlier = jnp.dot(jnp.where(rj < ri, 1.0, 0.0).astype(BF16), onehot.astype(BF16),
                      preferred_element_type=F32)
    base = run_s[0:1, :] + earlier
    r1 = jnp.sum(jnp.where(hit1, base, 0.0), axis=-1, keepdims=True)
    r2 = jnp.sum(jnp.where(hit2, base, 0.0), axis=-1, keepdims=True)
    total = run_s[0:1, :] + jnp.sum(onehot, axis=0, keepdims=True)
    run_s[0:1, :] = total
    cnt_ref[...] = jnp.broadcast_to(total, cnt_ref.shape)

    rec = jnp.zeros(lg.shape, F32)
    for ln, val in ((SEL_E1, i1.astype(F32)), (SEL_E2, i2.astype(F32)), (SEL_W1, w1), (SEL_W2, w2),
                    (SEL_R1, r1), (SEL_R2, r2)):
        rec = jnp.where(lane == ln, val, rec)
    sel_ref[...] = rec


def _route_call(x, mods, norm2, w_route, b_route, mrow):
    rows = x.shape[0]
    return pl.pallas_call(
        _route_kernel,
        grid=(rows // TM,),
        in_specs=[
            pl.BlockSpec((TM, D_MODEL), lambda i: (i, 0)),
            pl.BlockSpec((None, 6, D_MODEL), lambda i: (mrow(i), 0, 0)),
            pl.BlockSpec((1, D_MODEL), lambda i: (0, 0)),
            pl.BlockSpec((D_MODEL, LANES), lambda i: (0, 0)),
            pl.BlockSpec((1, LANES), lambda i: (0, 0)),
        ],
        out_specs=[
            pl.BlockSpec((TM * ROW_TILES, LANES), lambda i: (i, 0)),
            pl.BlockSpec((TM, LANES), lambda i: (i, 0)),
            pl.BlockSpec((SUBLANES, LANES), lambda i: (0, 0)),
        ],
        out_shape=[
            jax.ShapeDtypeStruct((rows * ROW_TILES, LANES), F32),
            jax.ShapeDtypeStruct((rows, LANES), F32),
            jax.ShapeDtypeStruct((SUBLANES, LANES), F32),
        ],
        scratch_shapes=[pltpu.VMEM((SUBLANES, LANES), F32)],
        compiler_params=_cparams(("arbitrary",)),
        name="moe_route",
    )(x, mods, norm2, w_route, b_route)


def _moe_plan(sel, cnt, n_tok, e_base):
    n_asg = 2 * n_tok
    max_tiles = n_asg // TE + N_EXPERTS
    eids = jnp.arange(N_EXPERTS, dtype=jnp.int32)
    counts = cnt[0, :N_EXPERTS].astype(jnp.int32)
    tiles_e = (counts + TE - 1) // TE
    cum_tiles = jnp.cumsum(tiles_e)
    n_tiles = cum_tiles[-1]
    off = (cum_tiles - tiles_e) * TE
    start = jnp.cumsum(counts) - counts
    e = sel[:, SEL_E1:SEL_E2 + 1].astype(jnp.int32)
    rank = sel[:, SEL_R1:SEL_R2 + 1].astype(jnp.int32)
    off_e = jnp.sum(jnp.where(e[:, :, None] == eids, off, 0), axis=-1)
    pos = (off_e + rank).reshape(n_asg)
    tok = jnp.arange(n_asg, dtype=jnp.int32) // 2
    _, tok_sorted = lax.sort_key_val(pos, tok)
    last_e = jnp.max(jnp.where(tiles_e > 0, eids, 0))
    t_idx = jnp.arange(max_tiles, dtype=jnp.int32)
    tile_e = jnp.minimum(jnp.sum((cum_tiles[None, :] <= t_idx[:, None]).astype(jnp.int32), axis=1), last_e)
    r_d = (t_idx * TE - off[tile_e])[:, None] + jnp.arange(TE, dtype=jnp.int32)[None, :]
    valid = jnp.logical_and(r_d < counts[tile_e][:, None], (t_idx < n_tiles)[:, None])
    idx = jnp.clip(start[tile_e][:, None] + r_d, 0, n_asg - 1)
    src = jnp.where(valid, tok_sorted[idx], 0).reshape(max_tiles * TE).astype(jnp.int32)
    return (tile_e + e_base).astype(jnp.int32), n_tiles.reshape(1).astype(jnp.int32), src, pos


def _moe_experts_kernel(te_ref, nt_ref, src_ref, h_hbm, wg_ref, wu_ref, wd_ref, y_ref,
                        buf, xb, wg16, wu16, wd16, sem):
    t = pl.program_id(0)
    n_tiles = nt_ref[0]

    def row_copy(tok, slot, r):
        return pltpu.make_async_copy(_token_tile(h_hbm, tok), _token_tile(buf.at[slot], r), sem.at[slot])

    for k in range(GATHER_DEPTH - 1):
        @pl.when(jnp.logical_and(t == 0, k < n_tiles))
        def _(k=k):
            def body(r, carry):
                row_copy(src_ref[k * TE + r], k, r).start()
                return carry

            lax.fori_loop(0, TE, body, 0, unroll=8)

    changed = jnp.logical_or(t == 0, te_ref[t] != te_ref[jnp.maximum(t - 1, 0)])

    @pl.when(jnp.logical_and(t < n_tiles, changed))
    def _():
        wg16[...] = wg_ref[...].astype(BF16)
        wu16[...] = wu_ref[...].astype(BF16)
        wd16[...] = wd_ref[...].astype(BF16)

    ahead = GATHER_DEPTH - 1

    def run_tile(prefetch):
        slot = lax.rem(t, GATHER_DEPTH)
        pltpu.make_async_copy(h_hbm.at[pl.ds(0, TE * ROW_TILES), :], buf.at[slot], sem.at[slot]).wait()
        for s in range(ROW_TILES):
            xb[:, s * LANES:(s + 1) * LANES] = _load_token_tiles(buf.at[slot], s, TE).astype(BF16)
        if prefetch:
            base = (t + ahead) * TE
            nslot = lax.rem(t + ahead, GATHER_DEPTH)
            for r in range(TE):
                row_copy(src_ref[base + r], nslot, r).start()
        x = xb[...]
        hg = jnp.dot(x, wg16[...], preferred_element_type=F32)
        hu = jnp.dot(x, wu16[...], preferred_element_type=F32)
        act = (_silu(hg) * hu).astype(BF16)
        _store_token_tiles(y_ref, jnp.dot(act, wd16[...], preferred_element_type=F32))

    @pl.when(t + ahead < n_tiles)
    def _():
        run_tile(True)

    @pl.when(jnp.logical_and(t < n_tiles, t + ahead >= n_tiles))
    def _():
        run_tile(False)

    @pl.when(t >= n_tiles)
    def _():
        y_ref[...] = jnp.zeros_like(y_ref)


def _moe_experts(h2, tile_e, n_tiles, src, w_gate, w_up, w_down):
    max_tiles = tile_e.shape[0]
    wspec = lambda a, b: pl.BlockSpec((None, a, b), lambda t, te, nt, sr: (te[t], 0, 0))
    return pl.pallas_call(
        _moe_experts_kernel,
        grid_spec=pltpu.PrefetchScalarGridSpec(
            num_scalar_prefetch=3,
            grid=(max_tiles,),
            in_specs=[
                pl.BlockSpec(memory_space=pl.ANY),
                wspec(D_MODEL, D_EXPERT), wspec(D_MODEL, D_EXPERT), wspec(D_EXPERT, D_MODEL),
            ],
            out_specs=pl.BlockSpec((TE * ROW_TILES, LANES), lambda t, te, nt, sr: (t, 0)),
            scratch_shapes=[
                pltpu.VMEM((GATHER_DEPTH, TE * ROW_TILES, LANES), F32),
                pltpu.VMEM((TE, D_MODEL), BF16),
                pltpu.VMEM((D_MODEL, D_EXPERT), BF16),
                pltpu.VMEM((D_MODEL, D_EXPERT), BF16),
                pltpu.VMEM((D_EXPERT, D_MODEL), BF16),
                pltpu.SemaphoreType.DMA((GATHER_DEPTH,)),
            ],
        ),
        out_shape=jax.ShapeDtypeStruct((max_tiles * TE * ROW_TILES, LANES), F32),
        compiler_params=_cparams(("arbitrary",)),
        name="moe_experts",
    )(tile_e, n_tiles, src, h2, w_gate, w_up, w_down)


def _moe_combine_kernel(final, pos_ref, y_hbm, x_ref, sel_ref, mod_ref, fn_ref, o_ref, buf, sem):
    i = pl.program_id(0)
    n = pl.num_programs(0)

    def row_copy(p, slot, k, r):
        return pltpu.make_async_copy(_token_tile(y_hbm, p), _token_tile(buf.at[slot, k], r), sem.at[slot])

    @pl.when(i == 0)
    def _():
        def body(r, carry):
            row_copy(pos_ref[2 * r], 0, 0, r).start()
            row_copy(pos_ref[2 * r + 1], 0, 1, r).start()
            return carry

        lax.fori_loop(0, TM, body, 0, unroll=8)

    slot = i % 2

    @pl.when(i + 1 < n)
    def _():
        base = (i + 1) * (2 * TM)
        for r in range(TM):
            row_copy(pos_ref[base + 2 * r], 1 - slot, 0, r).start()
            row_copy(pos_ref[base + 2 * r + 1], 1 - slot, 1, r).start()

    for k in range(2):
        pltpu.make_async_copy(y_hbm.at[pl.ds(0, TM * ROW_TILES), :], buf.at[slot, k], sem.at[slot]).wait()
    sel = sel_ref[...]
    w1 = sel[:, SEL_W1:SEL_W1 + 1]
    w2 = sel[:, SEL_W2:SEL_W2 + 1]
    parts = []
    for s in range(ROW_TILES):
        sl = slice(s * LANES, (s + 1) * LANES)
        y = w1 * _load_token_tiles(buf.at[slot, 0], s, TM) + w2 * _load_token_tiles(buf.at[slot, 1], s, TM)
        parts.append(x_ref[:, sl] + mod_ref[5:6, sl] * y)
    x = jnp.concatenate(parts, axis=-1)
    if final:
        x = x * lax.rsqrt(jnp.mean(x * x, axis=-1, keepdims=True) + EPS) * fn_ref[...]
    o_ref[...] = x


def _moe_combine(y_sorted, pos, x, sel, mods, final_norm, mrow, final):
    rows = x.shape[0]
    wide = pl.BlockSpec((TM, D_MODEL), lambda i, ps: (i, 0))
    return pl.pallas_call(
        functools.partial(_moe_combine_kernel, final),
        grid_spec=pltpu.PrefetchScalarGridSpec(
            num_scalar_prefetch=1,
            grid=(rows // TM,),
            in_specs=[
                pl.BlockSpec(memory_space=pl.ANY),
                wide,
                pl.BlockSpec((TM, LANES), lambda i, ps: (i, 0)),
                pl.BlockSpec((None, 6, D_MODEL), lambda i, ps: (mrow(i), 0, 0)),
                pl.BlockSpec((1, D_MODEL), lambda i, ps: (0, 0)),
            ],
            out_specs=wide,
            scratch_shapes=[pltpu.VMEM((2, 2, TM * ROW_TILES, LANES), F32), pltpu.SemaphoreType.DMA((2,))],
        ),
        out_shape=jax.ShapeDtypeStruct((rows, D_MODEL), F32),
        compiler_params=_cparams(("arbitrary",)),
        name="moe_combine",
    )(pos, y_sorted, x, sel, mods, final_norm)


def _moe_block(x, mods, norm2, w_grp, b_grp, w_exp, b_exp, w_gate, w_up, w_down, layer, final_norm, mrow, final):
    pad = LANES - N_EXPERTS - N_GROUPS
    w_route = jnp.concatenate([w_exp, w_grp, jnp.zeros((D_MODEL, pad), F32)], axis=1)
    b_route = jnp.concatenate([b_exp, b_grp, jnp.zeros((pad,), F32)]).reshape(1, LANES)
    h2, sel, cnt = _route_call(x, mods, norm2, w_route, b_route, mrow)
    tile_e, n_tiles, src, pos = _moe_plan(sel, cnt, x.shape[0], layer * N_EXPERTS)
    y_sorted = _moe_experts(h2, tile_e, n_tiles, src, w_gate, w_up, w_down)
    return _moe_combine(y_sorted, pos, x, sel, mods, final_norm, mrow, final)


def _ab_params(w_in, conv_qkv, a_log, dt_bias, conv_x, rg_wr, rg_br, rg_wi, rg_bi, rg_lam):
    o1 = 3 * A_WIDTH
    o2 = 4 * A_WIDTH
    o3 = o2 + 4 * A_HEADS
    o4 = o3 + B_WIDTH
    gate_cols = jnp.concatenate([w_in[:, o2:o3], jnp.zeros((D_MODEL, LANES - 4 * A_HEADS), F32)], axis=1)
    w_pad = jnp.concatenate([w_in[:, :o1], w_in[:, o3:o4], w_in[:, o1:o2], w_in[:, o4:], gate_cols],
                            axis=1).astype(BF16)
    z4 = jnp.zeros((A_HEADS,), F32)
    ztail = jnp.zeros((LANES - 4 * A_HEADS,), F32)
    alog_vec = jnp.concatenate([a_log[0], z4, a_log[1], z4, ztail]).reshape(1, LANES)
    dt_vec = jnp.concatenate([dt_bias[0], z4, dt_bias[1], z4, ztail]).reshape(1, LANES)
    eye = jnp.eye(B_BLOCKS, dtype=F32)
    bdiag = lambda w: jnp.einsum('gkj,gh->gkhj', w, eye).reshape(B_WIDTH, B_WIDTH)
    wbd = jnp.concatenate([bdiag(rg_wr[0]), bdiag(rg_wr[1]), bdiag(rg_wi[0]), bdiag(rg_wi[1])], axis=1)
    bbd = jnp.concatenate([rg_br[0].reshape(-1), rg_br[1].reshape(-1),
                           rg_bi[0].reshape(-1), rg_bi[1].reshape(-1)]).reshape(1, 4 * B_WIDTH)
    lam = rg_lam.reshape(1, 2 * B_WIDTH)
    return w_pad, alog_vec, dt_vec, wbd.astype(BF16), bbd, lam


def _attn_params(w_qkv, q_norm, k_norm, t_lat, c_rows):
    half = C_HD // 2
    perm = jnp.concatenate([jnp.arange(half) * 2, jnp.arange(half) * 2 + 1])
    nrot = (C_HEADS + C_KV_HEADS) * C_HD
    cols = (jnp.arange(C_HEADS + C_KV_HEADS)[:, None] * C_HD + perm[None, :]).reshape(-1)
    cols = jnp.concatenate([cols, jnp.arange(nrot, w_qkv.shape[1])])
    w_perm = w_qkv[:, cols].astype(BF16)
    qn = q_norm[perm].reshape(1, C_HD)
    kn = k_norm[perm].reshape(1, C_HD)
    return w_perm, qn, kn


def _rope_tables(t_lat, c_rows, grid_w):
    rows = t_lat // grid_w
    row = np.repeat(np.arange(rows, dtype=np.float64), grid_w)
    col = np.tile(np.arange(grid_w, dtype=np.float64), rows)
    n_freq = C_HD // 4
    inv = ROPE_THETA ** (-np.arange(n_freq, dtype=np.float64) / n_freq)
    ang = np.concatenate([row[:, None] * inv, col[:, None] * inv], axis=-1)
    cos = np.cos(ang).astype(np.float32)
    sin = np.sin(ang).astype(np.float32)
    cos_tab = np.concatenate([np.ones((c_rows, C_HD), np.float32), np.concatenate([cos, cos], axis=-1)], axis=0)
    sin_tab = np.concatenate([np.zeros((c_rows, C_HD), np.float32), np.concatenate([-sin, sin], axis=-1)], axis=0)
    return jnp.asarray(cos_tab), jnp.asarray(sin_tab)


GRID_W = 64


def kernel(x, c, ctx, c_ctx, ada_w, ada_b, norm1, norm2, final_norm, ab_w_in, ab_conv_qkv, ab_a_log, ab_dt_bias, ab_onorm, ab_conv_x, ab_rg_wr, ab_rg_br, ab_rg_wi, ab_rg_bi, ab_rg_lam, ab_w_out, at_w_qkv, at_q_norm, at_k_norm, at_w_out, moe_w_grp, moe_b_grp, moe_w_exp, moe_b_exp, moe_w_gate, moe_w_up, moe_w_down):
    nb, t_lat, _ = x.shape
    c_rows = ctx.shape[1]
    p_rows = c_rows + t_lat
    nt = p_rows // TM
    nc = c_rows // TM
    depth = ada_w.shape[0]
    assert depth == 2 and nb < 16 and c_rows % TM == 0 and t_lat % TM == 0

    cond = jnp.concatenate([c, c_ctx[None], jnp.zeros((16 - nb - 1, D_MODEL), F32)], axis=0)
    mods = _ada_mod(cond, ada_w, ada_b)
    xu = jnp.concatenate([ctx, x], axis=1).reshape(nb * p_rows, D_MODEL)
    uni_mrow = lambda i: jnp.where(i % nt < nc, nb, i // nt)
    fnorm = final_norm.reshape(1, D_MODEL)
    w_gate = moe_w_gate.reshape(depth * N_EXPERTS, D_MODEL, D_EXPERT)
    w_up = moe_w_up.reshape(depth * N_EXPERTS, D_MODEL, D_EXPERT)
    w_down = moe_w_down.reshape(depth * N_EXPERTS, D_EXPERT, D_MODEL)

    w_pad, alog_vec, dt_vec, wbd, bbd, lam = _ab_params(
        ab_w_in[0], ab_conv_qkv[0], ab_a_log[0], ab_dt_bias[0], ab_conv_x[0],
        ab_rg_wr[0], ab_rg_br[0], ab_rg_wi[0], ab_rg_bi[0], ab_rg_lam[0])
    q, k, v, gout, yb, gates, a0, b0, a1, b1 = _inproj_ab(
        xu, mods[0], norm1[0].reshape(1, D_MODEL), w_pad, ab_conv_qkv[0], ab_conv_x[0],
        alog_vec, dt_vec, wbd, bbd, lam, nb, nt, nc)
    o_f, o_b = _delta(q, k, v, gates, nb, p_rows, c_rows)
    h_f, h_b = _lru(a0, b0, a1, b1, nb, nt, nc)
    xu = _merge_ab(o_f, o_b, h_f, h_b, gout, yb, xu, mods[0], ab_onorm[0].reshape(1, A_DK),
                   ab_w_out[0].astype(BF16), nb, nt, nc)
    xu = _moe_block(xu, mods[0], norm2[0].reshape(1, D_MODEL), moe_w_grp[0], moe_b_grp[0], moe_w_exp[0],
                    moe_b_exp[0], w_gate, w_up, w_down, 0, fnorm, uni_mrow, False)

    w_perm, qn, kn = _attn_params(at_w_qkv[0], at_q_norm[0], at_k_norm[0], t_lat, c_rows)
    cos_tab, sin_tab = _rope_tables(t_lat, c_rows, GRID_W)
    q, k, v = _inproj_attn(xu, mods[1], norm1[1].reshape(1, D_MODEL), w_perm, qn, kn, cos_tab, sin_tab,
                           nb, nt, nc)
    att = _attention(q, k, v, nb, nt, nc, p_rows)
    xl = _outproj_lat(att, xu, mods[1], at_w_out[0].astype(BF16), nt, nc)
    nq = nt - nc
    xl = _moe_block(xl, mods[1], norm2[1].reshape(1, D_MODEL), moe_w_grp[1], moe_b_grp[1], moe_w_exp[1],
                    moe_b_exp[1], w_gate, w_up, w_down, 1, fnorm, lambda j: j // nq, True)
    return xl.reshape(nb, t_lat, D_MODEL)
```

```python
import functools
import math

import jax
import jax.numpy as jnp
import numpy as np
from jax import lax
from jax.experimental import pallas as pl
from jax.experimental.pallas import tpu as pltpu

F32 = jnp.float32
BF16 = jnp.bfloat16
HIGHEST = lax.Precision.HIGHEST

D_MODEL = 1024
EPS = 1e-6
TM = 256
LANES = 128
SUBLANES = 8

A_HEADS = 4
A_DK = 128
A_WIDTH = A_HEADS * A_DK
CHUNK = 64
CONV_W = 4
B_WIDTH = 512
B_BLOCKS = 8
B_BLK = B_WIDTH // B_BLOCKS
RG_C = 8.0

C_HEADS = 8
C_KV_HEADS = 2
C_HD = 128
C_GRP = C_HEADS // C_KV_HEADS
ROPE_THETA = 10000.0
LOG2_E = math.log2(math.e)

N_GROUPS = 4
EXP_PER_GROUP = 8
N_EXPERTS = N_GROUPS * EXP_PER_GROUP
D_EXPERT = 512

VMEM_LIMIT = 56 * 1024 * 1024

ZC_QKV = 0
ZC_XB = 3 * A_WIDTH
ZC_GOUT = ZC_XB + B_WIDTH
ZC_YB = ZC_GOUT + A_WIDTH
ZC_GATE = ZC_YB + B_WIDTH
ZC_TOTAL = ZC_GATE + LANES
ZC_CONV = ZC_GOUT
HALO = SUBLANES


def _cparams(sem):
    return pltpu.CompilerParams(dimension_semantics=sem, vmem_limit_bytes=VMEM_LIMIT)


def _sigmoid(x):
    return jax.nn.sigmoid(x)


def _silu(x):
    return x * jax.nn.sigmoid(x)


def _softplus(x):
    return jnp.maximum(x, 0.0) + jnp.log1p(jnp.exp(-jnp.abs(x)))


def _gelu_tanh(x):
    c = math.sqrt(2.0 / math.pi)
    return 0.5 * x * (1.0 + jnp.tanh(c * (x + 0.044715 * (x * x * x))))


def _dot_split(a, b):
    a_hi = a.astype(BF16)
    b_hi = b.astype(BF16)
    a_lo = (a - a_hi.astype(F32)).astype(BF16)
    b_lo = (b - b_hi.astype(F32)).astype(BF16)
    dot = functools.partial(jnp.dot, preferred_element_type=F32)
    return dot(a_hi, b_hi) + (dot(a_hi, b_lo) + dot(a_lo, b_hi))


def _modulate(x, gain, shift, scale):
    y = x * lax.rsqrt(jnp.mean(x * x, axis=-1, keepdims=True) + EPS)
    return (y * gain) * (1.0 + scale) + shift


def _ada_kernel(cond_ref, w_ref, b_ref, o_ref):
    s = _silu(cond_ref[...]).astype(BF16)
    o_ref[...] = jnp.dot(s, w_ref[...].astype(BF16), preferred_element_type=F32) + b_ref[...]


def _ada_mod(cond, ada_w, ada_b):
    depth = ada_w.shape[0]
    tn = 1536
    nn = 6 * D_MODEL // tn
    out = pl.pallas_call(
        _ada_kernel,
        grid=(depth, nn),
        in_specs=[
            pl.BlockSpec((16, D_MODEL), lambda l, n: (0, 0)),
            pl.BlockSpec((None, D_MODEL, tn), lambda l, n: (l, 0, n)),
            pl.BlockSpec((None, 1, tn), lambda l, n: (l, 0, n)),
        ],
        out_specs=pl.BlockSpec((None, 16, tn), lambda l, n: (l, 0, n)),
        out_shape=jax.ShapeDtypeStruct((depth, 16, 6 * D_MODEL), F32),
        compiler_params=_cparams(("arbitrary", "arbitrary")),
        name="ada_mod",
    )(cond, ada_w, ada_b.reshape(depth, 1, 6 * D_MODEL))
    return out.reshape(depth, 16, 6, D_MODEL)


def _inproj_ab_kernel(nt, nc, xc_ref, xp_ref, xn_ref, mod_ref, n1_ref, w_ref, cq_ref, cx_ref,
                      alog_ref, dt_ref, wbd_ref, bbd_ref, lam_ref,
                      q_ref, k_ref, v_ref, go_ref, yb_ref, g_ref, a0_ref, b0_ref, a1_ref, b1_ref,
                      zbuf, xcbuf):
    i = pl.program_id(0)
    r = i % nt
    is_ctx = r < nc
    prev_ok = jnp.logical_and(r > 0, ((r - 1) < nc) == is_ctx)
    next_ok = jnp.logical_and(r < nt - 1, ((r + 1) < nc) == is_ctx)

    gain = n1_ref[...]
    shift = mod_ref[0:1, :]
    scale = mod_ref[1:2, :]
    xall = jnp.concatenate([xp_ref[...], xc_ref[...], xn_ref[...]], axis=0)
    h = _modulate(xall, gain, shift, scale).astype(BF16)
    zbuf[...] = jnp.dot(h, w_ref[...], preferred_element_type=F32)
    zbuf[0:HALO, 0:ZC_CONV] = jnp.where(prev_ok, zbuf[0:HALO, 0:ZC_CONV], 0.0)
    zbuf[HALO + TM:, 0:ZC_CONV] = jnp.where(next_ok, zbuf[HALO + TM:, 0:ZC_CONV], 0.0)

    def conv(c0, w_taps_ref, wc0):
        acc = None
        for j in range(CONV_W):
            start = HALO - CONV_W // 2 + j
            term = zbuf[start:start + TM, c0:c0 + LANES] * w_taps_ref[j:j + 1, wc0:wc0 + LANES]
            acc = term if acc is None else acc + term
        return acc

    outs = (q_ref, k_ref, v_ref)
    for s in range(3 * A_HEADS):
        y = _silu(conv(s * LANES, cq_ref, s * LANES))
        if s < 2 * A_HEADS:
            y = y * lax.rsqrt(jnp.sum(y * y, axis=-1, keepdims=True) + EPS)
        if s < A_HEADS:
            y = y * (A_DK ** -0.5)
        hh = s % A_HEADS
        outs[s // A_HEADS][:, hh * LANES:(hh + 1) * LANES] = y

    for s in range(B_WIDTH // LANES):
        xcbuf[:, s * LANES:(s + 1) * LANES] = conv(ZC_XB + s * LANES, cx_ref, s * LANES)

    go_ref[...] = zbuf[HALO:HALO + TM, ZC_GOUT:ZC_GOUT + A_WIDTH].astype(BF16)
    yb_ref[...] = zbuf[HALO:HALO + TM, ZC_YB:ZC_YB + B_WIDTH].astype(BF16)

    zg = zbuf[HALO:HALO + TM, ZC_GATE:ZC_GATE + LANES]
    lane = lax.broadcasted_iota(jnp.int32, (TM, LANES), 1)
    dec = -jnp.exp(alog_ref[...]) * _softplus(zg + dt_ref[...])
    g_ref[...] = jnp.where(jnp.bitwise_and(lane, 7) < A_HEADS, dec, _sigmoid(zg))

    xc = xcbuf[...]
    rg = jnp.dot(xc.astype(BF16), wbd_ref[...], preferred_element_type=F32) + bbd_ref[...]
    sp = _softplus(-lam_ref[...])
    for d, (a_ref, b_ref) in enumerate(((a0_ref, b0_ref), (a1_ref, b1_ref))):
        rr = _sigmoid(rg[:, d * B_WIDTH:(d + 1) * B_WIDTH])
        ii = _sigmoid(rg[:, (2 + d) * B_WIDTH:(3 + d) * B_WIDTH])
        a = jnp.exp(-RG_C * rr * sp[:, d * B_WIDTH:(d + 1) * B_WIDTH])
        a_ref[...] = a
        b_ref[...] = jnp.sqrt(1.0 - a * a) * ii * xc


def _inproj_ab(xu, mods, norm1, w_pad, conv_qkv, conv_x, alog_vec, dt_vec, wbd, bbd, lam, nb, nt, nc):
    rows = xu.shape[0]
    ntiles = rows // TM
    hb = TM // HALO
    nhalo = rows // HALO

    def mrow(i):
        return jnp.where(i % nt < nc, nb, i // nt)

    full = lambda shape: pl.BlockSpec(shape, lambda i: (0,) * len(shape))
    row_spec = lambda w: pl.BlockSpec((TM, w), lambda i: (i, 0))
    sds = lambda w: jax.ShapeDtypeStruct((rows, w), F32)
    return pl.pallas_call(
        functools.partial(_inproj_ab_kernel, nt, nc),
        grid=(ntiles,),
        in_specs=[
            row_spec(D_MODEL),
            pl.BlockSpec((HALO, D_MODEL), lambda i: (jnp.maximum(i * hb - 1, 0), 0)),
            pl.BlockSpec((HALO, D_MODEL), lambda i: (jnp.minimum((i + 1) * hb, nhalo - 1), 0)),
            pl.BlockSpec((None, 6, D_MODEL), lambda i: (mrow(i), 0, 0)),
            full((1, D_MODEL)),
            full((D_MODEL, ZC_TOTAL)),
            full((CONV_W, 3 * A_WIDTH)),
            full((CONV_W, B_WIDTH)),
            full((1, LANES)),
            full((1, LANES)),
            full((B_WIDTH, 4 * B_WIDTH)),
            full((1, 4 * B_WIDTH)),
            full((1, 2 * B_WIDTH)),
        ],
        out_specs=[row_spec(A_WIDTH)] * 5 + [row_spec(LANES)] + [row_spec(B_WIDTH)] * 4,
        out_shape=[sds(A_WIDTH)] * 3 + [jax.ShapeDtypeStruct((rows, A_WIDTH), BF16)] * 2
        + [sds(LANES)] + [sds(B_WIDTH)] * 4,
        scratch_shapes=[pltpu.VMEM((TM + 2 * HALO, ZC_TOTAL), F32), pltpu.VMEM((TM, B_WIDTH), F32)],
        compiler_params=_cparams(("arbitrary",)),
        name="inproj_ab",
    )(xu, xu, xu, mods, norm1, w_pad, conv_qkv, conv_x, alog_vec, dt_vec, wbd, bbd, lam)


PAIR = 2 * CHUNK
N_STREAMS = 4
PREP_CHUNKS = 4
PREP_PROBLEMS = PREP_CHUNKS * N_STREAMS
SCAN_BATCH_MAX = 8


def _delta_prep_kernel(q_ref, k_ref, v_ref, g_ref, u_ref, wq_ref, at_ref, kt_ref, aux_ref,
                       gcum_s, dec_s, kq_s, rhs_s, n_s, pwf_s, l_s):
    ii = lax.broadcasted_iota(jnp.int32, (PAIR, PAIR), 0)
    jj = lax.broadcasted_iota(jnp.int32, (PAIR, PAIR), 1)
    same = (ii < CHUNK) == (jj < CHUNK)
    ci = lax.broadcasted_iota(jnp.int32, (CHUNK, CHUNK), 0)
    cj = lax.broadcasted_iota(jnp.int32, (CHUNK, CHUNK), 1)
    aux_ref[...] = jnp.zeros_like(aux_ref)
    problems = [(c, d, p) for c in range(PREP_CHUNKS) for d in range(2) for p in range(2)]

    for c in range(PREP_CHUNKS):
        gates = g_ref[c * CHUNK:(c + 1) * CHUNK, :]
        for d in range(2):
            tri = (ci >= cj) if d == 0 else (ci <= cj)
            gcum_s[2 * c + d] = jnp.dot(tri.astype(F32), gates, precision=HIGHEST, preferred_element_type=F32)

    for n, (c, d, p) in enumerate(problems):
        x = 2 * d + p
        rows = slice(c * CHUNK, (c + 1) * CHUNK)
        heads = (2 * p, 2 * p + 1)
        lanes = [d * 2 * A_HEADS + h for h in heads]
        last = CHUNK - 1 if d == 0 else 0
        lower = jnp.logical_and(same, (ii >= jj) if d == 0 else (ii <= jj))
        stack = lambda ref: jnp.concatenate([ref[rows, h * A_DK:(h + 1) * A_DK] for h in heads], axis=0)
        gcum = gcum_s[2 * c + d]
        gates = g_ref[rows, :]
        gc_col = jnp.concatenate([gcum[:, l:l + 1] for l in lanes], axis=0)
        beta = jnp.concatenate([gates[:, l + A_HEADS:l + A_HEADS + 1] for l in lanes], axis=0)
        g_last = [gcum[last:last + 1, l:l + 1] for l in lanes]
        gl_col = jnp.concatenate([jnp.broadcast_to(g, (CHUNK, 1)) for g in g_last], axis=0)
        gc_mat = jnp.broadcast_to(gc_col, (PAIR, PAIR))
        dec_s[n] = jnp.where(lower, jnp.exp(jnp.where(lower, gc_mat - gc_mat.T, 0.0)), 0.0)
        eg = jnp.exp(gc_col)
        q = stack(q_ref)
        k = stack(k_ref)
        kb = k * beta
        kq_s[n] = lax.dot_general(jnp.concatenate([kb, q], axis=0).astype(BF16), k.astype(BF16),
                                  (((1,), (1,)), ((), ())), preferred_element_type=F32)
        rhs_s[n] = jnp.concatenate([stack(v_ref) * beta, kb * eg], axis=-1)
        wq_ref[x, (2 * c + 1) * PAIR:(2 * c + 2) * PAIR, :] = (q * eg).astype(BF16)
        kt_ref[x, c * PAIR:(c + 1) * PAIR, :] = (k * jnp.exp(gl_col - gc_col)).T.astype(BF16)
        aux_ref[c * SUBLANES + x:c * SUBLANES + x + 1, :] = jnp.concatenate(
            [jnp.broadcast_to(jnp.exp(g), (1, A_DK)) for g in g_last], axis=-1)

    blk = lambda s: jnp.right_shift(ii, s) == jnp.right_shift(jj, s)
    for n, (c, d, p) in enumerate(problems):
        strict = jnp.logical_and(same, (ii > jj) if d == 0 else (ii < jj))
        dec = dec_s[n]
        lmat = jnp.where(strict, kq_s[n, :PAIR] * dec, 0.0)
        l_s[n] = lmat
        n_s[n] = jnp.where(blk(1), -lmat, 0.0)
        at_ref[2 * d + p, c * PAIR:(c + 1) * PAIR, :] = (kq_s[n, PAIR:] * dec).astype(BF16)

    for s in range(1, int(math.log2(CHUNK))):
        off = jnp.logical_and(blk(s + 1), jnp.logical_not(blk(s)))
        for n in range(PREP_PROBLEMS):
            l_off = jnp.where(off, l_s[n], 0.0)
            pwf_s[n] = l_off + jnp.dot(l_off.astype(BF16), n_s[n].astype(BF16), preferred_element_type=F32)
        for n in range(PREP_PROBLEMS):
            nm = n_s[n]
            xm = pwf_s[n]
            n_s[n] = nm - xm - jnp.dot(nm.astype(BF16), xm.astype(BF16), preferred_element_type=F32)

    for n, (c, d, p) in enumerate(problems):
        x = 2 * d + p
        rhs = rhs_s[n]
        sol = rhs + jnp.dot(n_s[n].astype(BF16), rhs.astype(BF16), preferred_element_type=F32)
        u_ref[x, c * PAIR:(c + 1) * PAIR, :] = sol[:, :A_DK]
        wq_ref[x, 2 * c * PAIR:(2 * c + 1) * PAIR, :] = sol[:, A_DK:].astype(BF16)


def _delta_scan_kernel(scan_batch, *refs):
    ins = refs[:4 * N_STREAMS]
    auxf_ref, auxb_ref, of_ref, ob_ref, s_ref, r_s, vbd_s = refs[4 * N_STREAMS:]
    insts = [(bb, x) for bb in range(scan_batch) for x in range(N_STREAMS)]

    @pl.when(pl.program_id(1) == 0)
    def _():
        s_ref[...] = jnp.zeros_like(s_ref)

    zeros = jnp.zeros((CHUNK, A_DK), F32)
    for n, (bb, x) in enumerate(insts):
        wq_ref = ins[4 * x + 1]
        r_s[n] = jnp.dot(wq_ref[bb], s_ref[n].astype(BF16), preferred_element_type=F32)
    for n, (bb, x) in enumerate(insts):
        d, p = divmod(x, 2)
        u_ref, _, at_ref, _ = ins[4 * x:4 * x + 4]
        o_ref = of_ref if d == 0 else ob_ref
        vn_a = u_ref[bb, :CHUNK, :] - r_s[n, :CHUNK, :A_DK]
        vn_b = u_ref[bb, CHUNK:, :] - r_s[n, CHUNK:PAIR, A_DK:]
        av = jnp.dot(at_ref[bb], jnp.concatenate([vn_a, vn_b], axis=0).astype(BF16),
                     preferred_element_type=F32)
        o_ref[bb, :, 2 * p * A_DK:(2 * p + 1) * A_DK] = (r_s[n, PAIR:PAIR + CHUNK, :A_DK] + av[:CHUNK]).astype(BF16)
        o_ref[bb, :, (2 * p + 1) * A_DK:(2 * p + 2) * A_DK] = (r_s[n, PAIR + CHUNK:, A_DK:] + av[CHUNK:]).astype(BF16)
        vbd_s[n] = jnp.concatenate([jnp.concatenate([vn_a, zeros], axis=1),
                                    jnp.concatenate([zeros, vn_b], axis=1)], axis=0).astype(BF16)
    for n, (bb, x) in enumerate(insts):
        aux_ref = auxf_ref if x < 2 else auxb_ref
        kt_ref = ins[4 * x + 3]
        s_ref[n] = s_ref[n] * aux_ref[bb, x:x + 1, :] + jnp.dot(kt_ref[bb], vbd_s[n],
                                                               preferred_element_type=F32)


def _bwd_order(s, n_ctx, n_all):
    return jnp.where(s < n_ctx, n_ctx - 1 - s, n_all - 1 - (s - n_ctx))


def _delta(q, k, v, g, nb, p_rows, c_rows):
    rows = q.shape[0]
    nchunks = rows // CHUNK
    pr = PREP_CHUNKS * CHUNK
    row_spec = lambda w: pl.BlockSpec((pr, w), lambda i: (i, 0))
    stream_spec = lambda m: pl.BlockSpec((N_STREAMS, PREP_CHUNKS * m, A_DK), lambda i: (0, i, 0))
    stream_sds = lambda m, dt: jax.ShapeDtypeStruct((N_STREAMS, nchunks * m, A_DK), dt)
    u, wq, at, kt, aux = pl.pallas_call(
        _delta_prep_kernel,
        grid=(nchunks // PREP_CHUNKS,),
        in_specs=[row_spec(A_WIDTH)] * 3 + [row_spec(LANES)],
        out_specs=[stream_spec(PAIR), stream_spec(2 * PAIR), stream_spec(PAIR), stream_spec(PAIR),
                   pl.BlockSpec((PREP_CHUNKS * SUBLANES, 2 * A_DK), lambda i: (i, 0))],
        out_shape=[stream_sds(PAIR, F32), stream_sds(2 * PAIR, BF16), stream_sds(PAIR, BF16),
                   stream_sds(PAIR, BF16), jax.ShapeDtypeStruct((nchunks * SUBLANES, 2 * A_DK), F32)],
        scratch_shapes=[
            pltpu.VMEM((2 * PREP_CHUNKS, CHUNK, LANES), F32),
            pltpu.VMEM((PREP_PROBLEMS, PAIR, PAIR), F32),
            pltpu.VMEM((PREP_PROBLEMS, 2 * PAIR, PAIR), F32),
            pltpu.VMEM((PREP_PROBLEMS, PAIR, 2 * A_DK), F32),
            pltpu.VMEM((PREP_PROBLEMS, PAIR, PAIR), F32),
            pltpu.VMEM((PREP_PROBLEMS, PAIR, PAIR), F32),
            pltpu.VMEM((PREP_PROBLEMS, PAIR, PAIR), F32),
        ],
        compiler_params=_cparams(("arbitrary",)),
        name="delta_prep",
    )(q, k, v, g)

    sb = math.gcd(nb, SCAN_BATCH_MAX)
    ng = nb // sb
    n_all = p_rows // CHUNK
    n_ctx = c_rows // CHUNK
    order = (lambda s: s, lambda s: _bwd_order(s, n_ctx, n_all))
    view = lambda arr, m: arr.reshape(N_STREAMS, ng, sb, n_all * m, A_DK)
    in_specs, args = [], []
    for x in range(N_STREAMS):
        pos = order[x // 2]
        for arr, m in ((u, PAIR), (wq, 2 * PAIR), (at, PAIR), (kt, PAIR)):
            in_specs.append(pl.BlockSpec((None, None, sb, m, A_DK),
                                         lambda b, s, x=x, pos=pos: (x, b, 0, pos(s), 0)))
            args.append(view(arr, m))
    for pos in order:
        in_specs.append(pl.BlockSpec((None, sb, SUBLANES, 2 * A_DK), lambda b, s, pos=pos: (b, 0, pos(s), 0)))
        args.append(aux.reshape(ng, sb, n_all * SUBLANES, 2 * A_DK))
    out_spec = lambda pos: pl.BlockSpec((None, sb, CHUNK, A_WIDTH), lambda b, s: (b, 0, pos(s), 0))
    n_inst = sb * N_STREAMS
    o_f, o_b = pl.pallas_call(
        functools.partial(_delta_scan_kernel, sb),
        grid=(ng, n_all),
        in_specs=in_specs,
        out_specs=[out_spec(order[0]), out_spec(order[1])],
        out_shape=[jax.ShapeDtypeStruct((ng, sb, p_rows, A_WIDTH), BF16)] * 2,
        scratch_shapes=[
            pltpu.VMEM((n_inst, A_DK, 2 * A_DK), F32),
            pltpu.VMEM((n_inst, 2 * PAIR, 2 * A_DK), F32),
            pltpu.VMEM((n_inst, PAIR, 2 * A_DK), BF16),
        ],
        compiler_params=_cparams(("arbitrary", "arbitrary")),
        name="delta_scan",
    )(*args)
    return o_f.reshape(rows, A_WIDTH), o_b.reshape(rows, A_WIDTH)


def _lru_kernel(a0_ref, b0_ref, a1_ref, b1_ref, hf_ref, hb_ref, carry_ref):
    @pl.when(pl.program_id(1) == 0)
    def _():
        carry_ref[...] = jnp.zeros_like(carry_ref)

    row = lax.broadcasted_iota(jnp.int32, (SUBLANES, B_WIDTH), 0)
    ngroups = TM // SUBLANES

    def scan_group(a_ref, b_ref, h_ref, r0, h_in, reverse):
        a = a_ref[pl.ds(r0, SUBLANES), :]
        b = b_ref[pl.ds(r0, SUBLANES), :]
        for sft in (1, 2, 4):
            shift = SUBLANES - sft if reverse else sft
            keep = (row < SUBLANES - sft) if reverse else (row >= sft)
            a_sh = pltpu.roll(a, shift, axis=0)
            b_sh = pltpu.roll(b, shift, axis=0)
            b = jnp.where(keep, a * b_sh + b, b)
            a = jnp.where(keep, a * a_sh, a)
        hrows = a * h_in + b
        h_ref[pl.ds(r0, SUBLANES), :] = hrows
        return hrows[0:1, :] if reverse else hrows[SUBLANES - 1:SUBLANES, :]

    def fwd_body(t, h_in):
        r0 = pl.multiple_of(t * SUBLANES, SUBLANES)
        return scan_group(a0_ref, b0_ref, hf_ref, r0, h_in, False)

    def bwd_body(t, h_in):
        r0 = pl.multiple_of((ngroups - 1 - t) * SUBLANES, SUBLANES)
        return scan_group(a1_ref, b1_ref, hb_ref, r0, h_in, True)

    carry_ref[0:1, :] = lax.fori_loop(0, ngroups, fwd_body, carry_ref[0:1, :])
    carry_ref[1:2, :] = lax.fori_loop(0, ngroups, bwd_body, carry_ref[1:2, :])


def _lru(a0, b0, a1, b1, nb, nt, nc):
    rows = a0.shape[0]
    fwd = pl.BlockSpec((TM, B_WIDTH), lambda b, s: (b * nt + s, 0))
    bwd = pl.BlockSpec((TM, B_WIDTH), lambda b, s: (b * nt + _bwd_order(s, nc, nt), 0))
    return pl.pallas_call(
        _lru_kernel,
        grid=(nb, nt),
        in_specs=[fwd, fwd, bwd, bwd],
        out_specs=[fwd, bwd],
        out_shape=[jax.ShapeDtypeStruct((rows, B_WIDTH), F32)] * 2,
        scratch_shapes=[pltpu.VMEM((SUBLANES, B_WIDTH), F32)],
        compiler_params=_cparams(("arbitrary", "arbitrary")),
        name="rg_lru_scan",
    )(a0, b0, a1, b1)


def _merge_ab_kernel(of_ref, ob_ref, hf_ref, hb_ref, go_ref, yb_ref, x_ref, mod_ref, on_ref, w_ref, xo_ref):
    parts = []
    for h in range(A_HEADS):
        sl = slice(h * A_DK, (h + 1) * A_DK)
        o = of_ref[:, sl].astype(F32) + ob_ref[:, sl].astype(F32)
        n = o * lax.rsqrt(jnp.mean(o * o, axis=-1, keepdims=True) + EPS) * on_ref[...]
        parts.append((n * _silu(go_ref[:, sl].astype(F32))).astype(BF16))
    parts.append(((hf_ref[...] + hb_ref[...]) * _gelu_tanh(yb_ref[...].astype(F32))).astype(BF16))
    cat = jnp.concatenate(parts, axis=-1)
    y = jnp.dot(cat, w_ref[...], preferred_element_type=F32)
    xo_ref[...] = x_ref[...] + mod_ref[2:3, :] * y


def _merge_ab(o_f, o_b, h_f, h_b, gout, yb, xu, mods, onorm, w_out, nb, nt, nc):
    rows = xu.shape[0]
    half = pl.BlockSpec((TM, A_WIDTH), lambda i: (i, 0))
    wide = pl.BlockSpec((TM, D_MODEL), lambda i: (i, 0))
    mrow = lambda i: jnp.where(i % nt < nc, nb, i // nt)
    return pl.pallas_call(
        _merge_ab_kernel,
        grid=(rows // TM,),
        in_specs=[half] * 6 + [
            wide,
            pl.BlockSpec((None, 6, D_MODEL), lambda i: (mrow(i), 0, 0)),
            pl.BlockSpec((1, A_DK), lambda i: (0, 0)),
            pl.BlockSpec((D_MODEL, D_MODEL), lambda i: (0, 0)),
        ],
        out_specs=wide,
        out_shape=jax.ShapeDtypeStruct((rows, D_MODEL), F32),
        compiler_params=_cparams(("arbitrary",)),
        name="merge_ab",
    )(o_f, o_b, h_f, h_b, gout, yb, xu, mods, onorm, w_out)


def _inproj_attn_kernel(x_ref, mod_ref, n1_ref, w_ref, qn_ref, kn_ref, cos_ref, sin_ref, q_ref, k_ref, v_ref):
    h = _modulate(x_ref[...], n1_ref[...], mod_ref[0:1, :], mod_ref[1:2, :]).astype(BF16)
    z = jnp.dot(h, w_ref[...], preferred_element_type=F32)
    cos = cos_ref[...]
    sin = sin_ref[...]
    for hd in range(C_HEADS + C_KV_HEADS):
        xh = z[:, hd * C_HD:(hd + 1) * C_HD]
        gain = qn_ref[...] if hd < C_HEADS else kn_ref[...]
        n = xh * lax.rsqrt(jnp.mean(xh * xh, axis=-1, keepdims=True) + EPS) * gain
        rot = n * cos + pltpu.roll(n, C_HD // 2, axis=1) * sin
        if hd < C_HEADS:
            q_ref[:, hd * C_HD:(hd + 1) * C_HD] = (rot * (C_HD ** -0.5 * LOG2_E)).astype(BF16)
        else:
            kh = hd - C_HEADS
            k_ref[:, kh * C_HD:(kh + 1) * C_HD] = rot.astype(BF16)
    v_ref[...] = z[:, (C_HEADS + C_KV_HEADS) * C_HD:].astype(BF16)


def _inproj_attn(xu, mods, norm1, w_perm, qn, kn, cos_tab, sin_tab, nb, nt, nc):
    rows = xu.shape[0]
    nqk = (C_HEADS + 2 * C_KV_HEADS) * C_HD
    mrow = lambda i: jnp.where(i % nt < nc, nb, i // nt)
    full = lambda shape: pl.BlockSpec(shape, lambda i: (0,) * len(shape))
    tab = pl.BlockSpec((TM, C_HD), lambda i: (i % nt, 0))
    return pl.pallas_call(
        _inproj_attn_kernel,
        grid=(rows // TM,),
        in_specs=[
            pl.BlockSpec((TM, D_MODEL), lambda i: (i, 0)),
            pl.BlockSpec((None, 6, D_MODEL), lambda i: (mrow(i), 0, 0)),
            full((1, D_MODEL)),
            full((D_MODEL, nqk)),
            full((1, C_HD)),
            full((1, C_HD)),
            tab, tab,
        ],
        out_specs=[
            pl.BlockSpec((TM, C_HEADS * C_HD), lambda i: (i, 0)),
            pl.BlockSpec((TM, C_KV_HEADS * C_HD), lambda i: (i, 0)),
            pl.BlockSpec((TM, C_KV_HEADS * C_HD), lambda i: (i, 0)),
        ],
        out_shape=[
            jax.ShapeDtypeStruct((rows, C_HEADS * C_HD), BF16),
            jax.ShapeDtypeStruct((rows, C_KV_HEADS * C_HD), BF16),
            jax.ShapeDtypeStruct((rows, C_KV_HEADS * C_HD), BF16),
        ],
        compiler_params=_cparams(("arbitrary",)),
        name="inproj_attn",
    )(xu, mods, norm1, w_perm, qn, kn, cos_tab, sin_tab)


def _attn_kernel(q_ref, k_ref, v_ref, o_ref):
    k = k_ref[...]
    v_ext = jnp.concatenate([v_ref[...], jnp.ones(v_ref.shape, v_ref.dtype)], axis=-1)
    for g in range(C_GRP):
        sl = slice(g * C_HD, (g + 1) * C_HD)
        s = lax.dot_general(q_ref[:, sl], k, (((1,), (1,)), ((), ())), preferred_element_type=F32)
        p = jnp.exp2(s - jnp.max(s, axis=-1, keepdims=True))
        o_ext = jnp.dot(p.astype(BF16), v_ext, preferred_element_type=F32)
        o_ref[:, sl] = (o_ext[:, :C_HD] / o_ext[:, C_HD:]).astype(BF16)


def _attention(q, k, v, nb, nt, nc, p_rows):
    nq = nt - nc
    gw = C_GRP * C_HD
    return pl.pallas_call(
        _attn_kernel,
        grid=(nb, C_KV_HEADS, nq),
        in_specs=[
            pl.BlockSpec((TM, gw), lambda b, h, t: (b * nt + nc + t, h)),
            pl.BlockSpec((p_rows, C_HD), lambda b, h, t: (b, h)),
            pl.BlockSpec((p_rows, C_HD), lambda b, h, t: (b, h)),
        ],
        out_specs=pl.BlockSpec((TM, gw), lambda b, h, t: (b * nq + t, h)),
        out_shape=jax.ShapeDtypeStruct((nb * nq * TM, C_HEADS * C_HD), BF16),
        compiler_params=_cparams(("arbitrary", "arbitrary", "arbitrary")),
        name="gqa_attention",
    )(q, k, v)


def _outproj_kernel(a_ref, x_ref, mod_ref, w_ref, xo_ref):
    y = jnp.dot(a_ref[...], w_ref[...], preferred_element_type=F32)
    xo_ref[...] = x_ref[...] + mod_ref[2:3, :] * y


def _outproj_lat(a, xu, mods, w_out, nt, nc):
    rows = a.shape[0]
    nq = nt - nc
    return pl.pallas_call(
        _outproj_kernel,
        grid=(rows // TM,),
        in_specs=[
            pl.BlockSpec((TM, D_MODEL), lambda j: (j, 0)),
            pl.BlockSpec((TM, D_MODEL), lambda j: ((j // nq) * nt + nc + j % nq, 0)),
            pl.BlockSpec((None, 6, D_MODEL), lambda j: (j // nq, 0, 0)),
            pl.BlockSpec((D_MODEL, D_MODEL), lambda j: (0, 0)),
        ],
        out_specs=pl.BlockSpec((TM, D_MODEL), lambda j: (j, 0)),
        out_shape=jax.ShapeDtypeStruct((rows, D_MODEL), F32),
        compiler_params=_cparams(("arbitrary",)),
        name="outproj_attn",
    )(a, xu, mods, w_out)


TE = 256
GATHER_DEPTH = 4
ROW_TILES = D_MODEL // LANES


def _store_token_tiles(ref, val):
    for s in range(ROW_TILES):
        ref[pl.ds(s, val.shape[0], stride=ROW_TILES), :] = val[:, s * LANES:(s + 1) * LANES]


def _load_token_tiles(ref, s, rows):
    return ref[pl.ds(s, rows, stride=ROW_TILES), :]


def _token_tile(ref, row):
    return ref.at[pl.ds(pl.multiple_of(row * ROW_TILES, ROW_TILES), ROW_TILES), :]
SEL_E1, SEL_E2, SEL_W1, SEL_W2, SEL_R1, SEL_R2 = 0, 1, 2, 3, 4, 5


def _route_kernel(x_ref, mod_ref, n2_ref, wr_ref, br_ref, h_ref, sel_ref, cnt_ref, run_s):
    @pl.when(pl.program_id(0) == 0)
    def _():
        run_s[...] = jnp.zeros_like(run_s)

    h = _modulate(x_ref[...], n2_ref[...], mod_ref[3:4, :], mod_ref[4:5, :])
    _store_token_tiles(h_ref, h)
    lg = _dot_split(h, wr_ref[...]) + br_ref[...]
    lane = lax.broadcasted_iota(jnp.int32, lg.shape, 1)
    neg = jnp.float32(-jnp.inf)
    big = jnp.int32(1 << 20)
    is_grp = jnp.logical_and(lane >= N_EXPERTS, lane < N_EXPERTS + N_GROUPS)
    gl = jnp.where(is_grp, lg, neg)
    gmax = jnp.max(gl, axis=-1, keepdims=True)
    gidx = jnp.min(jnp.where(gl == gmax, lane - N_EXPERTS, big), axis=-1, keepdims=True)
    g_w = 1.0 / jnp.sum(jnp.where(is_grp, jnp.exp(gl - gmax), 0.0), axis=-1, keepdims=True)
    in_grp = jnp.logical_and(lane < N_EXPERTS, jnp.right_shift(lane, 3) == gidx)
    e1 = jnp.where(in_grp, lg, neg)
    m1 = jnp.max(e1, axis=-1, keepdims=True)
    i1 = jnp.min(jnp.where(e1 == m1, lane, big), axis=-1, keepdims=True)
    e2 = jnp.where(lane == i1, neg, e1)
    m2 = jnp.max(e2, axis=-1, keepdims=True)
    i2 = jnp.min(jnp.where(e2 == m2, lane, big), axis=-1, keepdims=True)
    t = jnp.exp(m2 - m1)
    w1 = g_w / (1.0 + t)
    w2 = g_w * t / (1.0 + t)

    hit1 = lane == i1
    hit2 = lane == i2
    onehot = jnp.where(hit1, 1.0, 0.0) + jnp.where(hit2, 1.0, 0.0)
    ri = lax.broadcasted_iota(jnp.int32, (TM, TM), 0)
    rj = lax.broadcasted_iota(jnp.int32, (TM, TM), 1)
    earlier = jnp.dot(jnp.where(rj < ri, 1.0, 0.0).astype(BF16), onehot.astype(BF16),
                      preferred_element_type=F32)
    base = run_s[0:1, :] + earlier
    r1 = jnp.sum(jnp.where(hit1, base, 0.0), axis=-1, keepdims=True)
    r2 = jnp.sum(jnp.where(hit2, base, 0.0), axis=-1, keepdims=True)
    total = run_s[0:1, :] + jnp.sum(onehot, axis=0, keepdims=True)
    run_s[0:1, :] = total
    cnt_ref[...] = jnp.broadcast_to(total, cnt_ref.shape)

    rec = jnp.zeros(lg.shape, F32)
    for ln, val in ((SEL_E1, i1.astype(F32)), (SEL_E2, i2.astype(F32)), (SEL_W1, w1), (SEL_W2, w2),
                    (SEL_R1, r1), (SEL_R2, r2)):
        rec = jnp.where(lane == ln, val, rec)
    sel_ref[...] = rec


def _route_call(x, mods, norm2, w_route, b_route, mrow):
    rows = x.shape[0]
    return pl.pallas_call(
        _route_kernel,
        grid=(rows // TM,),
        in_specs=[
            pl.BlockSpec((TM, D_MODEL), lambda i: (i, 0)),
            pl.BlockSpec((None, 6, D_MODEL), lambda i: (mrow(i), 0, 0)),
            pl.BlockSpec((1, D_MODEL), lambda i: (0, 0)),
            pl.BlockSpec((D_MODEL, LANES), lambda i: (0, 0)),
            pl.BlockSpec((1, LANES), lambda i: (0, 0)),
        ],
        out_specs=[
            pl.BlockSpec((TM * ROW_TILES, LANES), lambda i: (i, 0)),
            pl.BlockSpec((TM, LANES), lambda i: (i, 0)),
            pl.BlockSpec((SUBLANES, LANES), lambda i: (0, 0)),
        ],
        out_shape=[
            jax.ShapeDtypeStruct((rows * ROW_TILES, LANES), F32),
            jax.ShapeDtypeStruct((rows, LANES), F32),
            jax.ShapeDtypeStruct((SUBLANES, LANES), F32),
        ],
        scratch_shapes=[pltpu.VMEM((SUBLANES, LANES), F32)],
        compiler_params=_cparams(("arbitrary",)),
        name="moe_route",
    )(x, mods, norm2, w_route, b_route)


def _moe_plan(sel, cnt, n_tok, e_base):
    n_asg = 2 * n_tok
    max_tiles = n_asg // TE + N_EXPERTS
    eids = jnp.arange(N_EXPERTS, dtype=jnp.int32)
    counts = cnt[0, :N_EXPERTS].astype(jnp.int32)
    tiles_e = (counts + TE - 1) // TE
    cum_tiles = jnp.cumsum(tiles_e)
    n_tiles = cum_tiles[-1]
    off = (cum_tiles - tiles_e) * TE
    start = jnp.cumsum(counts) - counts
    e = sel[:, SEL_E1:SEL_E2 + 1].astype(jnp.int32)
    rank = sel[:, SEL_R1:SEL_R2 + 1].astype(jnp.int32)
    off_e = jnp.sum(jnp.where(e[:, :, None] == eids, off, 0), axis=-1)
    pos = (off_e + rank).reshape(n_asg)
    tok = jnp.arange(n_asg, dtype=jnp.int32) // 2
    _, tok_sorted = lax.sort_key_val(pos, tok)
    last_e = jnp.max(jnp.where(tiles_e > 0, eids, 0))
    t_idx = jnp.arange(max_tiles, dtype=jnp.int32)
    tile_e = jnp.minimum(jnp.sum((cum_tiles[None, :] <= t_idx[:, None]).astype(jnp.int32), axis=1), last_e)
    r_d = (t_idx * TE - off[tile_e])[:, None] + jnp.arange(TE, dtype=jnp.int32)[None, :]
    valid = jnp.logical_and(r_d < counts[tile_e][:, None], (t_idx < n_tiles)[:, None])
    idx = jnp.clip(start[tile_e][:, None] + r_d, 0, n_asg - 1)
    src = jnp.where(valid, tok_sorted[idx], 0).reshape(max_tiles * TE).astype(jnp.int32)
    return (tile_e + e_base).astype(jnp.int32), n_tiles.reshape(1).astype(jnp.int32), src, pos


def _moe_experts_kernel(te_ref, nt_ref, src_ref, h_hbm, wg_ref, wu_ref, wd_ref, y_ref,
                        buf, xb, wg16, wu16, wd16, sem):
    t = pl.program_id(0)
    n_tiles = nt_ref[0]

    def row_copy(tok, slot, r):
        return pltpu.make_async_copy(_token_tile(h_hbm, tok), _token_tile(buf.at[slot], r), sem.at[slot])

    for k in range(GATHER_DEPTH - 1):
        @pl.when(jnp.logical_and(t == 0, k < n_tiles))
        def _(k=k):
            def body(r, carry):
                row_copy(src_ref[k * TE + r], k, r).start()
                return carry

            lax.fori_loop(0, TE, body, 0, unroll=8)

    changed = jnp.logical_or(t == 0, te_ref[t] != te_ref[jnp.maximum(t - 1, 0)])

    @pl.when(jnp.logical_and(t < n_tiles, changed))
    def _():
        wg16[...] = wg_ref[...].astype(BF16)
        wu16[...] = wu_ref[...].astype(BF16)
        wd16[...] = wd_ref[...].astype(BF16)

    ahead = GATHER_DEPTH - 1

    def run_tile(prefetch):
        slot = lax.rem(t, GATHER_DEPTH)
        pltpu.make_async_copy(h_hbm.at[pl.ds(0, TE * ROW_TILES), :], buf.at[slot], sem.at[slot]).wait()
        for s in range(ROW_TILES):
            xb[:, s * LANES:(s + 1) * LANES] = _load_token_tiles(buf.at[slot], s, TE).astype(BF16)
        if prefetch:
            base = (t + ahead) * TE
            nslot = lax.rem(t + ahead, GATHER_DEPTH)
            for r in range(TE):
                row_copy(src_ref[base + r], nslot, r).start()
        x = xb[...]
        hg = jnp.dot(x, wg16[...], preferred_element_type=F32)
        hu = jnp.dot(x, wu16[...], preferred_element_type=F32)
        act = (_silu(hg) * hu).astype(BF16)
        _store_token_tiles(y_ref, jnp.dot(act, wd16[...], preferred_element_type=F32))

    @pl.when(t + ahead < n_tiles)
    def _():
        run_tile(True)

    @pl.when(jnp.logical_and(t < n_tiles, t + ahead >= n_tiles))
    def _():
        run_tile(False)

    @pl.when(t >= n_tiles)
    def _():
        y_ref[...] = jnp.zeros_like(y_ref)


def _moe_experts(h2, tile_e, n_tiles, src, w_gate, w_up, w_down):
    max_tiles = tile_e.shape[0]
    wspec = lambda a, b: pl.BlockSpec((None, a, b), lambda t, te, nt, sr: (te[t], 0, 0))
    return pl.pallas_call(
        _moe_experts_kernel,
        grid_spec=pltpu.PrefetchScalarGridSpec(
            num_scalar_prefetch=3,
            grid=(max_tiles,),
            in_specs=[
                pl.BlockSpec(memory_space=pl.ANY),
                wspec(D_MODEL, D_EXPERT), wspec(D_MODEL, D_EXPERT), wspec(D_EXPERT, D_MODEL),
            ],
            out_specs=pl.BlockSpec((TE * ROW_TILES, LANES), lambda t, te, nt, sr: (t, 0)),
            scratch_shapes=[
                pltpu.VMEM((GATHER_DEPTH, TE * ROW_TILES, LANES), F32),
                pltpu.VMEM((TE, D_MODEL), BF16),
                pltpu.VMEM((D_MODEL, D_EXPERT), BF16),
                pltpu.VMEM((D_MODEL, D_EXPERT), BF16),
                pltpu.VMEM((D_EXPERT, D_MODEL), BF16),
                pltpu.SemaphoreType.DMA((GATHER_DEPTH,)),
            ],
        ),
        out_shape=jax.ShapeDtypeStruct((max_tiles * TE * ROW_TILES, LANES), F32),
        compiler_params=_cparams(("arbitrary",)),
        name="moe_experts",
    )(tile_e, n_tiles, src, h2, w_gate, w_up, w_down)


def _moe_combine_kernel(final, pos_ref, y_hbm, x_ref, sel_ref, mod_ref, fn_ref, o_ref, buf, sem):
    i = pl.program_id(0)
    n = pl.num_programs(0)

    def row_copy(p, slot, k, r):
        return pltpu.make_async_copy(_token_tile(y_hbm, p), _token_tile(buf.at[slot, k], r), sem.at[slot])

    @pl.when(i == 0)
    def _():
        def body(r, carry):
            row_copy(pos_ref[2 * r], 0, 0, r).start()
            row_copy(pos_ref[2 * r + 1], 0, 1, r).start()
            return carry

        lax.fori_loop(0, TM, body, 0, unroll=8)

    slot = i % 2

    @pl.when(i + 1 < n)
    def _():
        base = (i + 1) * (2 * TM)
        for r in range(TM):
            row_copy(pos_ref[base + 2 * r], 1 - slot, 0, r).start()
            row_copy(pos_ref[base + 2 * r + 1], 1 - slot, 1, r).start()

    for k in range(2):
        pltpu.make_async_copy(y_hbm.at[pl.ds(0, TM * ROW_TILES), :], buf.at[slot, k], sem.at[slot]).wait()
    sel = sel_ref[...]
    w1 = sel[:, SEL_W1:SEL_W1 + 1]
    w2 = sel[:, SEL_W2:SEL_W2 + 1]
    parts = []
    for s in range(ROW_TILES):
        sl = slice(s * LANES, (s + 1) * LANES)
        y = w1 * _load_token_tiles(buf.at[slot, 0], s, TM) + w2 * _load_token_tiles(buf.at[slot, 1], s, TM)
        parts.append(x_ref[:, sl] + mod_ref[5:6, sl] * y)
    x = jnp.concatenate(parts, axis=-1)
    if final:
        x = x * lax.rsqrt(jnp.mean(x * x, axis=-1, keepdims=True) + EPS) * fn_ref[...]
    o_ref[...] = x


def _moe_combine(y_sorted, pos, x, sel, mods, final_norm, mrow, final):
    rows = x.shape[0]
    wide = pl.BlockSpec((TM, D_MODEL), lambda i, ps: (i, 0))
    return pl.pallas_call(
        functools.partial(_moe_combine_kernel, final),
        grid_spec=pltpu.PrefetchScalarGridSpec(
            num_scalar_prefetch=1,
            grid=(rows // TM,),
            in_specs=[
                pl.BlockSpec(memory_space=pl.ANY),
                wide,
                pl.BlockSpec((TM, LANES), lambda i, ps: (i, 0)),
                pl.BlockSpec((None, 6, D_MODEL), lambda i, ps: (mrow(i), 0, 0)),
                pl.BlockSpec((1, D_MODEL), lambda i, ps: (0, 0)),
            ],
            out_specs=wide,
            scratch_shapes=[pltpu.VMEM((2, 2, TM * ROW_TILES, LANES), F32), pltpu.SemaphoreType.DMA((2,))],
        ),
        out_shape=jax.ShapeDtypeStruct((rows, D_MODEL), F32),
        compiler_params=_cparams(("arbitrary",)),
        name="moe_combine",
    )(pos, y_sorted, x, sel, mods, final_norm)


def _moe_block(x, mods, norm2, w_grp, b_grp, w_exp, b_exp, w_gate, w_up, w_down, layer, final_norm, mrow, final):
    pad = LANES - N_EXPERTS - N_GROUPS
    w_route = jnp.concatenate([w_exp, w_grp, jnp.zeros((D_MODEL, pad), F32)], axis=1)
    b_route = jnp.concatenate([b_exp, b_grp, jnp.zeros((pad,), F32)]).reshape(1, LANES)
    h2, sel, cnt = _route_call(x, mods, norm2, w_route, b_route, mrow)
    tile_e, n_tiles, src, pos = _moe_plan(sel, cnt, x.shape[0], layer * N_EXPERTS)
    y_sorted = _moe_experts(h2, tile_e, n_tiles, src, w_gate, w_up, w_down)
    return _moe_combine(y_sorted, pos, x, sel, mods, final_norm, mrow, final)


def _ab_params(w_in, conv_qkv, a_log, dt_bias, conv_x, rg_wr, rg_br, rg_wi, rg_bi, rg_lam):
    o1 = 3 * A_WIDTH
    o2 = 4 * A_WIDTH
    o3 = o2 + 4 * A_HEADS
    o4 = o3 + B_WIDTH
    gate_cols = jnp.concatenate([w_in[:, o2:o3], jnp.zeros((D_MODEL, LANES - 4 * A_HEADS), F32)], axis=1)
    w_pad = jnp.concatenate([w_in[:, :o1], w_in[:, o3:o4], w_in[:, o1:o2], w_in[:, o4:], gate_cols],
                            axis=1).astype(BF16)
    z4 = jnp.zeros((A_HEADS,), F32)
    ztail = jnp.zeros((LANES - 4 * A_HEADS,), F32)
    alog_vec = jnp.concatenate([a_log[0], z4, a_log[1], z4, ztail]).reshape(1, LANES)
    dt_vec = jnp.concatenate([dt_bias[0], z4, dt_bias[1], z4, ztail]).reshape(1, LANES)
    eye = jnp.eye(B_BLOCKS, dtype=F32)
    bdiag = lambda w: jnp.einsum('gkj,gh->gkhj', w, eye).reshape(B_WIDTH, B_WIDTH)
    wbd = jnp.concatenate([bdiag(rg_wr[0]), bdiag(rg_wr[1]), bdiag(rg_wi[0]), bdiag(rg_wi[1])], axis=1)
    bbd = jnp.concatenate([rg_br[0].reshape(-1), rg_br[1].reshape(-1),
                           rg_bi[0].reshape(-1), rg_bi[1].reshape(-1)]).reshape(1, 4 * B_WIDTH)
    lam = rg_lam.reshape(1, 2 * B_WIDTH)
    return w_pad, alog_vec, dt_vec, wbd.astype(BF16), bbd, lam


def _attn_params(w_qkv, q_norm, k_norm, t_lat, c_rows):
    half = C_HD // 2
    perm = jnp.concatenate([jnp.arange(half) * 2, jnp.arange(half) * 2 + 1])
    nrot = (C_HEADS + C_KV_HEADS) * C_HD
    cols = (jnp.arange(C_HEADS + C_KV_HEADS)[:, None] * C_HD + perm[None, :]).reshape(-1)
    cols = jnp.concatenate([cols, jnp.arange(nrot, w_qkv.shape[1])])
    w_perm = w_qkv[:, cols].astype(BF16)
    qn = q_norm[perm].reshape(1, C_HD)
    kn = k_norm[perm].reshape(1, C_HD)
    return w_perm, qn, kn


def _rope_tables(t_lat, c_rows, grid_w):
    rows = t_lat // grid_w
    row = np.repeat(np.arange(rows, dtype=np.float64), grid_w)
    col = np.tile(np.arange(grid_w, dtype=np.float64), rows)
    n_freq = C_HD // 4
    inv = ROPE_THETA ** (-np.arange(n_freq, dtype=np.float64) / n_freq)
    ang = np.concatenate([row[:, None] * inv, col[:, None] * inv], axis=-1)
    cos = np.cos(ang).astype(np.float32)
    sin = np.sin(ang).astype(np.float32)
    cos_tab = np.concatenate([np.ones((c_rows, C_HD), np.float32), np.concatenate([cos, cos], axis=-1)], axis=0)
    sin_tab = np.concatenate([np.zeros((c_rows, C_HD), np.float32), np.concatenate([-sin, sin], axis=-1)], axis=0)
    return jnp.asarray(cos_tab), jnp.asarray(sin_tab)


GRID_W = 64


def kernel(x, c, ctx, c_ctx, ada_w, ada_b, norm1, norm2, final_norm, ab_w_in, ab_conv_qkv, ab_a_log, ab_dt_bias, ab_onorm, ab_conv_x, ab_rg_wr, ab_rg_br, ab_rg_wi, ab_rg_bi, ab_rg_lam, ab_w_out, at_w_qkv, at_q_norm, at_k_norm, at_w_out, moe_w_grp, moe_b_grp, moe_w_exp, moe_b_exp, moe_w_gate, moe_w_up, moe_w_down):
    nb, t_lat, _ = x.shape
    c_rows = ctx.shape[1]
    p_rows = c_rows + t_lat
    nt = p_rows // TM
    nc = c_rows // TM
    depth = ada_w.shape[0]
    assert depth == 2 and nb < 16 and c_rows % TM == 0 and t_lat % TM == 0

    cond = jnp.concatenate([c, c_ctx[None], jnp.zeros((16 - nb - 1, D_MODEL), F32)], axis=0)
    mods = _ada_mod(cond, ada_w, ada_b)
    xu = jnp.concatenate([ctx, x], axis=1).reshape(nb * p_rows, D_MODEL)
    uni_mrow = lambda i: jnp.where(i % nt < nc, nb, i // nt)
    fnorm = final_norm.reshape(1, D_MODEL)
    w_gate = moe_w_gate.reshape(depth * N_EXPERTS, D_MODEL, D_EXPERT)
    w_up = moe_w_up.reshape(depth * N_EXPERTS, D_MODEL, D_EXPERT)
    w_down = moe_w_down.reshape(depth * N_EXPERTS, D_EXPERT, D_MODEL)

    w_pad, alog_vec, dt_vec, wbd, bbd, lam = _ab_params(
        ab_w_in[0], ab_conv_qkv[0], ab_a_log[0], ab_dt_bias[0], ab_conv_x[0],
        ab_rg_wr[0], ab_rg_br[0], ab_rg_wi[0], ab_rg_bi[0], ab_rg_lam[0])
    q, k, v, gout, yb, gates, a0, b0, a1, b1 = _inproj_ab(
        xu, mods[0], norm1[0].reshape(1, D_MODEL), w_pad, ab_conv_qkv[0], ab_conv_x[0],
        alog_vec, dt_vec, wbd, bbd, lam, nb, nt, nc)
    o_f, o_b = _delta(q, k, v, gates, nb, p_rows, c_rows)
    h_f, h_b = _lru(a0, b0, a1, b1, nb, nt, nc)
    xu = _merge_ab(o_f, o_b, h_f, h_b, gout, yb, xu, mods[0], ab_onorm[0].reshape(1, A_DK),
                   ab_w_out[0].astype(BF16), nb, nt, nc)
    xu = _moe_block(xu, mods[0], norm2[0].reshape(1, D_MODEL), moe_w_grp[0], moe_b_grp[0], moe_w_exp[0],
                    moe_b_exp[0], w_gate, w_up, w_down, 0, fnorm, uni_mrow, False)

    w_perm, qn, kn = _attn_params(at_w_qkv[0], at_q_norm[0], at_k_norm[0], t_lat, c_rows)
    cos_tab, sin_tab = _rope_tables(t_lat, c_rows, GRID_W)
    q, k, v = _inproj_attn(xu, mods[1], norm1[1].reshape(1, D_MODEL), w_perm, qn, kn, cos_tab, sin_tab,
                           nb, nt, nc)
    att = _attention(q, k, v, nb, nt, nc, p_rows)
    xl = _outproj_lat(att, xu, mods[1], at_w_out[0].astype(BF16), nt, nc)
    nq = nt - nc
    xl = _moe_block(xl, mods[1], norm2[1].reshape(1, D_MODEL), moe_w_grp[1], moe_b_grp[1], moe_w_exp[1],
                    moe_b_exp[1], w_gate, w_up, w_down, 1, fnorm, lambda j: j // nq, True)
    return xl.reshape(nb, t_lat, D_MODEL)
```

```python
import functools
import math

import jax
import jax.numpy as jnp
import numpy as np
from jax import lax
from jax.experimental import pallas as pl
from jax.experimental.pallas import tpu as pltpu

F32 = jnp.float32
BF16 = jnp.bfloat16
HIGHEST = lax.Precision.HIGHEST

D_MODEL = 1024
EPS = 1e-6
TM = 256
LANES = 128
SUBLANES = 8

A_HEADS = 4
A_DK = 128
A_WIDTH = A_HEADS * A_DK
CHUNK = 64
CONV_W = 4
B_WIDTH = 512
B_BLOCKS = 8
B_BLK = B_WIDTH // B_BLOCKS
RG_C = 8.0

C_HEADS = 8
C_KV_HEADS = 2
C_HD = 128
C_GRP = C_HEADS // C_KV_HEADS
ROPE_THETA = 10000.0
LOG2_E = math.log2(math.e)

N_GROUPS = 4
EXP_PER_GROUP = 8
N_EXPERTS = N_GROUPS * EXP_PER_GROUP
D_EXPERT = 512

VMEM_LIMIT = 56 * 1024 * 1024

ZC_QKV = 0
ZC_XB = 3 * A_WIDTH
ZC_GOUT = ZC_XB + B_WIDTH
ZC_YB = ZC_GOUT + A_WIDTH
ZC_GATE = ZC_YB + B_WIDTH
ZC_TOTAL = ZC_GATE + LANES
ZC_CONV = ZC_GOUT
HALO = SUBLANES


def _cparams(sem):
    return pltpu.CompilerParams(dimension_semantics=sem, vmem_limit_bytes=VMEM_LIMIT)


def _sigmoid(x):
    return jax.nn.sigmoid(x)


def _silu(x):
    return x * jax.nn.sigmoid(x)


def _softplus(x):
    return jnp.maximum(x, 0.0) + jnp.log1p(jnp.exp(-jnp.abs(x)))


def _gelu_tanh(x):
    c = math.sqrt(2.0 / math.pi)
    return 0.5 * x * (1.0 + jnp.tanh(c * (x + 0.044715 * (x * x * x))))


def _dot_split(a, b):
    a_hi = a.astype(BF16)
    b_hi = b.astype(BF16)
    a_lo = (a - a_hi.astype(F32)).astype(BF16)
    b_lo = (b - b_hi.astype(F32)).astype(BF16)
    dot = functools.partial(jnp.dot, preferred_element_type=F32)
    return dot(a_hi, b_hi) + (dot(a_hi, b_lo) + dot(a_lo, b_hi))


def _modulate(x, gain, shift, scale):
    y = x * lax.rsqrt(jnp.mean(x * x, axis=-1, keepdims=True) + EPS)
    return (y * gain) * (1.0 + scale) + shift


def _ada_kernel(cond_ref, w_ref, b_ref, o_ref):
    s = _silu(cond_ref[...]).astype(BF16)
    o_ref[...] = jnp.dot(s, w_ref[...].astype(BF16), preferred_element_type=F32) + b_ref[...]


def _ada_mod(cond, ada_w, ada_b):
    depth = ada_w.shape[0]
    tn = 1536
    nn = 6 * D_MODEL // tn
    out = pl.pallas_call(
        _ada_kernel,
        grid=(depth, nn),
        in_specs=[
            pl.BlockSpec((16, D_MODEL), lambda l, n: (0, 0)),
            pl.BlockSpec((None, D_MODEL, tn), lambda l, n: (l, 0, n)),
            pl.BlockSpec((None, 1, tn), lambda l, n: (l, 0, n)),
        ],
        out_specs=pl.BlockSpec((None, 16, tn), lambda l, n: (l, 0, n)),
        out_shape=jax.ShapeDtypeStruct((depth, 16, 6 * D_MODEL), F32),
        compiler_params=_cparams(("arbitrary", "arbitrary")),
        name="ada_mod",
    )(cond, ada_w, ada_b.reshape(depth, 1, 6 * D_MODEL))
    return out.reshape(depth, 16, 6, D_MODEL)


def _inproj_ab_kernel(nt, nc, xc_ref, xp_ref, xn_ref, mod_ref, n1_ref, w_ref, cq_ref, cx_ref,
                      alog_ref, dt_ref, wbd_ref, bbd_ref, lam_ref,
                      q_ref, k_ref, v_ref, go_ref, yb_ref, g_ref, a0_ref, b0_ref, a1_ref, b1_ref,
                      zbuf, xcbuf):
    i = pl.program_id(0)
    r = i % nt
    is_ctx = r < nc
    prev_ok = jnp.logical_and(r > 0, ((r - 1) < nc) == is_ctx)
    next_ok = jnp.logical_and(r < nt - 1, ((r + 1) < nc) == is_ctx)

    gain = n1_ref[...]
    shift = mod_ref[0:1, :]
    scale = mod_ref[1:2, :]
    xall = jnp.concatenate([xp_ref[...], xc_ref[...], xn_ref[...]], axis=0)
    h = _modulate(xall, gain, shift, scale).astype(BF16)
    zbuf[...] = jnp.dot(h, w_ref[...], preferred_element_type=F32)
    zbuf[0:HALO, 0:ZC_CONV] = jnp.where(prev_ok, zbuf[0:HALO, 0:ZC_CONV], 0.0)
    zbuf[HALO + TM:, 0:ZC_CONV] = jnp.where(next_ok, zbuf[HALO + TM:, 0:ZC_CONV], 0.0)

    def conv(c0, w_taps_ref, wc0):
        acc = None
        for j in range(CONV_W):
            start = HALO - CONV_W // 2 + j
            term = zbuf[start:start + TM, c0:c0 + LANES] * w_taps_ref[j:j + 1, wc0:wc0 + LANES]
            acc = term if acc is None else acc + term
        return acc

    outs = (q_ref, k_ref, v_ref)
    for s in range(3 * A_HEADS):
        y = _silu(conv(s * LANES, cq_ref, s * LANES))
        if s < 2 * A_HEADS:
            y = y * lax.rsqrt(jnp.sum(y * y, axis=-1, keepdims=True) + EPS)
        if s < A_HEADS:
            y = y * (A_DK ** -0.5)
        hh = s % A_HEADS
        outs[s // A_HEADS][:, hh * LANES:(hh + 1) * LANES] = y

    for s in range(B_WIDTH // LANES):
        xcbuf[:, s * LANES:(s + 1) * LANES] = conv(ZC_XB + s * LANES, cx_ref, s * LANES)

    go_ref[...] = zbuf[HALO:HALO + TM, ZC_GOUT:ZC_GOUT + A_WIDTH].astype(BF16)
    yb_ref[...] = zbuf[HALO:HALO + TM, ZC_YB:ZC_YB + B_WIDTH].astype(BF16)

    zg = zbuf[HALO:HALO + TM, ZC_GATE:ZC_GATE + LANES]
    lane = lax.broadcasted_iota(jnp.int32, (TM, LANES), 1)
    dec = -jnp.exp(alog_ref[...]) * _softplus(zg + dt_ref[...])
    g_ref[...] = jnp.where(jnp.bitwise_and(lane, 7) < A_HEADS, dec, _sigmoid(zg))

    xc = xcbuf[...]
    rg = jnp.dot(xc.astype(BF16), wbd_ref[...], preferred_element_type=F32) + bbd_ref[...]
    sp = _softplus(-lam_ref[...])
    for d, (a_ref, b_ref) in enumerate(((a0_ref, b0_ref), (a1_ref, b1_ref))):
        rr = _sigmoid(rg[:, d * B_WIDTH:(d + 1) * B_WIDTH])
        ii = _sigmoid(rg[:, (2 + d) * B_WIDTH:(3 + d) * B_WIDTH])
        a = jnp.exp(-RG_C * rr * sp[:, d * B_WIDTH:(d + 1) * B_WIDTH])
        a_ref[...] = a
        b_ref[...] = jnp.sqrt(1.0 - a * a) * ii * xc


def _inproj_ab(xu, mods, norm1, w_pad, conv_qkv, conv_x, alog_vec, dt_vec, wbd, bbd, lam, nb, nt, nc):
    rows = xu.shape[0]
    ntiles = rows // TM
    hb = TM // HALO
    nhalo = rows // HALO

    def mrow(i):
        return jnp.where(i % nt < nc, nb, i // nt)

    full = lambda shape: pl.BlockSpec(shape, lambda i: (0,) * len(shape))
    row_spec = lambda w: pl.BlockSpec((TM, w), lambda i: (i, 0))
    sds = lambda w: jax.ShapeDtypeStruct((rows, w), F32)
    return pl.pallas_call(
        functools.partial(_inproj_ab_kernel, nt, nc),
        grid=(ntiles,),
        in_specs=[
            row_spec(D_MODEL),
            pl.BlockSpec((HALO, D_MODEL), lambda i: (jnp.maximum(i * hb - 1, 0), 0)),
            pl.BlockSpec((HALO, D_MODEL), lambda i: (jnp.minimum((i + 1) * hb, nhalo - 1), 0)),
            pl.BlockSpec((None, 6, D_MODEL), lambda i: (mrow(i), 0, 0)),
            full((1, D_MODEL)),
            full((D_MODEL, ZC_TOTAL)),
            full((CONV_W, 3 * A_WIDTH)),
            full((CONV_W, B_WIDTH)),
            full((1, LANES)),
            full((1, LANES)),
            full((B_WIDTH, 4 * B_WIDTH)),
            full((1, 4 * B_WIDTH)),
            full((1, 2 * B_WIDTH)),
        ],
        out_specs=[row_spec(A_WIDTH)] * 5 + [row_spec(LANES)] + [row_spec(B_WIDTH)] * 4,
        out_shape=[sds(A_WIDTH)] * 3 + [jax.ShapeDtypeStruct((rows, A_WIDTH), BF16)] * 2
        + [sds(LANES)] + [sds(B_WIDTH)] * 4,
        scratch_shapes=[pltpu.VMEM((TM + 2 * HALO, ZC_TOTAL), F32), pltpu.VMEM((TM, B_WIDTH), F32)],
        compiler_params=_cparams(("arbitrary",)),
        name="inproj_ab",
    )(xu, xu, xu, mods, norm1, w_pad, conv_qkv, conv_x, alog_vec, dt_vec, wbd, bbd, lam)


PAIR = 2 * CHUNK
N_STREAMS = 4
PREP_CHUNKS = 4
PREP_PROBLEMS = PREP_CHUNKS * N_STREAMS
SCAN_BATCH_MAX = 8


def _delta_prep_kernel(q_ref, k_ref, v_ref, g_ref, u_ref, wq_ref, at_ref, kt_ref, aux_ref,
                       gcum_s, dec_s, kq_s, rhs_s, n_s, pwf_s, l_s):
    ii = lax.broadcasted_iota(jnp.int32, (PAIR, PAIR), 0)
    jj = lax.broadcasted_iota(jnp.int32, (PAIR, PAIR), 1)
    same = (ii < CHUNK) == (jj < CHUNK)
    ci = lax.broadcasted_iota(jnp.int32, (CHUNK, CHUNK), 0)
    cj = lax.broadcasted_iota(jnp.int32, (CHUNK, CHUNK), 1)
    aux_ref[...] = jnp.zeros_like(aux_ref)
    problems = [(c, d, p) for c in range(PREP_CHUNKS) for d in range(2) for p in range(2)]

    for c in range(PREP_CHUNKS):
        gates = g_ref[c * CHUNK:(c + 1) * CHUNK, :]
        for d in range(2):
            tri = (ci >= cj) if d == 0 else (ci <= cj)
            gcum_s[2 * c + d] = jnp.dot(tri.astype(F32), gates, precision=HIGHEST, preferred_element_type=F32)

    for n, (c, d, p) in enumerate(problems):
        x = 2 * d + p
        rows = slice(c * CHUNK, (c + 1) * CHUNK)
        heads = (2 * p, 2 * p + 1)
        lanes = [d * 2 * A_HEADS + h for h in heads]
        last = CHUNK - 1 if d == 0 else 0
        lower = jnp.logical_and(same, (ii >= jj) if d == 0 else (ii <= jj))
        stack = lambda ref: jnp.concatenate([ref[rows, h * A_DK:(h + 1) * A_DK] for h in heads], axis=0)
        gcum = gcum_s[2 * c + d]
        gates = g_ref[rows, :]
        gc_col = jnp.concatenate([gcum[:, l:l + 1] for l in lanes], axis=0)
        beta = jnp.concatenate([gates[:, l + A_HEADS:l + A_HEADS + 1] for l in lanes], axis=0)
        g_last = [gcum[last:last + 1, l:l + 1] for l in lanes]
        gl_col = jnp.concatenate([jnp.broadcast_to(g, (CHUNK, 1)) for g in g_last], axis=0)
        gc_mat = jnp.broadcast_to(gc_col, (PAIR, PAIR))
        dec_s[n] = jnp.where(lower, jnp.exp(jnp.where(lower, gc_mat - gc_mat.T, 0.0)), 0.0)
        eg = jnp.exp(gc_col)
        q = stack(q_ref)
        k = stack(k_ref)
        kb = k * beta
        kq_s[n] = lax.dot_general(jnp.concatenate([kb, q], axis=0).astype(BF16), k.astype(BF16),
                                  (((1,), (1,)), ((), ())), preferred_element_type=F32)
        rhs_s[n] = jnp.concatenate([stack(v_ref) * beta, kb * eg], axis=-1)
        wq_ref[x, (2 * c + 1) * PAIR:(2 * c + 2) * PAIR, :] = (q * eg).astype(BF16)
        kt_ref[x, c * PAIR:(c + 1) * PAIR, :] = (k * jnp.exp(gl_col - gc_col)).T.astype(BF16)
        aux_ref[c * SUBLANES + x:c * SUBLANES + x + 1, :] = jnp.concatenate(
            [jnp.broadcast_to(jnp.exp(g), (1, A_DK)) for g in g_last], axis=-1)

    blk = lambda s: jnp.right_shift(ii, s) == jnp.right_shift(jj, s)
    for n, (c, d, p) in enumerate(problems):
        strict = jnp.logical_and(same, (ii > jj) if d == 0 else (ii < jj))
        dec = dec_s[n]
        lmat = jnp.where(strict, kq_s[n, :PAIR] * dec, 0.0)
        l_s[n] = lmat
        n_s[n] = jnp.where(blk(1), -lmat, 0.0)
        at_ref[2 * d + p, c * PAIR:(c + 1) * PAIR, :] = (kq_s[n, PAIR:] * dec).astype(BF16)

    for s in range(1, int(math.log2(CHUNK))):
        off = jnp.logical_and(blk(s + 1), jnp.logical_not(blk(s)))
        for n in range(PREP_PROBLEMS):
            l_off = jnp.where(off, l_s[n], 0.0)
            pwf_s[n] = l_off + jnp.dot(l_off.astype(BF16), n_s[n].astype(BF16), preferred_element_type=F32)
        for n in range(PREP_PROBLEMS):
            nm = n_s[n]
            xm = pwf_s[n]
            n_s[n] = nm - xm - jnp.dot(nm.astype(BF16), xm.astype(BF16), preferred_element_type=F32)

    for n, (c, d, p) in enumerate(problems):
        x = 2 * d + p
        rhs = rhs_s[n]
        sol = rhs + jnp.dot(n_s[n].astype(BF16), rhs.astype(BF16), preferred_element_type=F32)
        u_ref[x, c * PAIR:(c + 1) * PAIR, :] = sol[:, :A_DK]
        wq_ref[x, 2 * c * PAIR:(2 * c + 1) * PAIR, :] = sol[:, A_DK:].astype(BF16)


def _delta_scan_kernel(scan_batch, *refs):
    ins = refs[:4 * N_STREAMS]
    auxf_ref, auxb_ref, of_ref, ob_ref, s_ref, r_s, vbd_s = refs[4 * N_STREAMS:]
    insts = [(bb, x) for bb in range(scan_batch) for x in range(N_STREAMS)]

    @pl.when(pl.program_id(1) == 0)
    def _():
        s_ref[...] = jnp.zeros_like(s_ref)

    zeros = jnp.zeros((CHUNK, A_DK), F32)
    for n, (bb, x) in enumerate(insts):
        wq_ref = ins[4 * x + 1]
        r_s[n] = jnp.dot(wq_ref[bb], s_ref[n].astype(BF16), preferred_element_type=F32)
    for n, (bb, x) in enumerate(insts):
        d, p = divmod(x, 2)
        u_ref, _, at_ref, _ = ins[4 * x:4 * x + 4]
        o_ref = of_ref if d == 0 else ob_ref
        vn_a = u_ref[bb, :CHUNK, :] - r_s[n, :CHUNK, :A_DK]
        vn_b = u_ref[bb, CHUNK:, :] - r_s[n, CHUNK:PAIR, A_DK:]
        av = jnp.dot(at_ref[bb], jnp.concatenate([vn_a, vn_b], axis=0).astype(BF16),
                     preferred_element_type=F32)
        o_ref[bb, :, 2 * p * A_DK:(2 * p + 1) * A_DK] = (r_s[n, PAIR:PAIR + CHUNK, :A_DK] + av[:CHUNK]).astype(BF16)
        o_ref[bb, :, (2 * p + 1) * A_DK:(2 * p + 2) * A_DK] = (r_s[n, PAIR + CHUNK:, A_DK:] + av[CHUNK:]).astype(BF16)
        vbd_s[n] = jnp.concatenate([jnp.concatenate([vn_a, zeros], axis=1),
                                    jnp.concatenate([zeros, vn_b], axis=1)], axis=0).astype(BF16)
    for n, (bb, x) in enumerate(insts):
        aux_ref = auxf_ref if x < 2 else auxb_ref
        kt_ref = ins[4 * x + 3]
        s_ref[n] = s_ref[n] * aux_ref[bb, x:x + 1, :] + jnp.dot(kt_ref[bb], vbd_s[n],
                                                               preferred_element_type=F32)


def _bwd_order(s, n_ctx, n_all):
    return jnp.where(s < n_ctx, n_ctx - 1 - s, n_all - 1 - (s - n_ctx))


def _delta(q, k, v, g, nb, p_rows, c_rows):
    rows = q.shape[0]
    nchunks = rows // CHUNK
    pr = PREP_CHUNKS * CHUNK
    row_spec = lambda w: pl.BlockSpec((pr, w), lambda i: (i, 0))
    stream_spec = lambda m: pl.BlockSpec((N_STREAMS, PREP_CHUNKS * m, A_DK), lambda i: (0, i, 0))
    stream_sds = lambda m, dt: jax.ShapeDtypeStruct((N_STREAMS, nchunks * m, A_DK), dt)
    u, wq, at, kt, aux = pl.pallas_call(
        _delta_prep_kernel,
        grid=(nchunks // PREP_CHUNKS,),
        in_specs=[row_spec(A_WIDTH)] * 3 + [row_spec(LANES)],
        out_specs=[stream_spec(PAIR), stream_spec(2 * PAIR), stream_spec(PAIR), stream_spec(PAIR),
                   pl.BlockSpec((PREP_CHUNKS * SUBLANES, 2 * A_DK), lambda i: (i, 0))],
        out_shape=[stream_sds(PAIR, F32), stream_sds(2 * PAIR, BF16), stream_sds(PAIR, BF16),
                   stream_sds(PAIR, BF16), jax.ShapeDtypeStruct((nchunks * SUBLANES, 2 * A_DK), F32)],
        scratch_shapes=[
            pltpu.VMEM((2 * PREP_CHUNKS, CHUNK, LANES), F32),
            pltpu.VMEM((PREP_PROBLEMS, PAIR, PAIR), F32),
            pltpu.VMEM((PREP_PROBLEMS, 2 * PAIR, PAIR), F32),
            pltpu.VMEM((PREP_PROBLEMS, PAIR, 2 * A_DK), F32),
            pltpu.VMEM((PREP_PROBLEMS, PAIR, PAIR), F32),
            pltpu.VMEM((PREP_PROBLEMS, PAIR, PAIR), F32),
            pltpu.VMEM((PREP_PROBLEMS, PAIR, PAIR), F32),
        ],
        compiler_params=_cparams(("arbitrary",)),
        name="delta_prep",
    )(q, k, v, g)

    sb = math.gcd(nb, SCAN_BATCH_MAX)
    ng = nb // sb
    n_all = p_rows // CHUNK
    n_ctx = c_rows // CHUNK
    order = (lambda s: s, lambda s: _bwd_order(s, n_ctx, n_all))
    view = lambda arr, m: arr.reshape(N_STREAMS, ng, sb, n_all * m, A_DK)
    in_specs, args = [], []
    for x in range(N_STREAMS):
        pos = order[x // 2]
        for arr, m in ((u, PAIR), (wq, 2 * PAIR), (at, PAIR), (kt, PAIR)):
            in_specs.append(pl.BlockSpec((None, None, sb, m, A_DK),
                                         lambda b, s, x=x, pos=pos: (x, b, 0, pos(s), 0)))
            args.append(view(arr, m))
    for pos in order:
        in_specs.append(pl.BlockSpec((None, sb, SUBLANES, 2 * A_DK), lambda b, s, pos=pos: (b, 0, pos(s), 0)))
        args.append(aux.reshape(ng, sb, n_all * SUBLANES, 2 * A_DK))
    out_spec = lambda pos: pl.BlockSpec((None, sb, CHUNK, A_WIDTH), lambda b, s: (b, 0, pos(s), 0))
    n_inst = sb * N_STREAMS
    o_f, o_b = pl.pallas_call(
        functools.partial(_delta_scan_kernel, sb),
        grid=(ng, n_all),
        in_specs=in_specs,
        out_specs=[out_spec(order[0]), out_spec(order[1])],
        out_shape=[jax.ShapeDtypeStruct((ng, sb, p_rows, A_WIDTH), BF16)] * 2,
        scratch_shapes=[
            pltpu.VMEM((n_inst, A_DK, 2 * A_DK), F32),
            pltpu.VMEM((n_inst, 2 * PAIR, 2 * A_DK), F32),
            pltpu.VMEM((n_inst, PAIR, 2 * A_DK), BF16),
        ],
        compiler_params=_cparams(("arbitrary", "arbitrary")),
        name="delta_scan",
    )(*args)
    return o_f.reshape(rows, A_WIDTH), o_b.reshape(rows, A_WIDTH)


def _lru_kernel(a0_ref, b0_ref, a1_ref, b1_ref, hf_ref, hb_ref, carry_ref):
    @pl.when(pl.program_id(1) == 0)
    def _():
        carry_ref[...] = jnp.zeros_like(carry_ref)

    row = lax.broadcasted_iota(jnp.int32, (SUBLANES, B_WIDTH), 0)
    ngroups = TM // SUBLANES

    def scan_group(a_ref, b_ref, h_ref, r0, h_in, reverse):
        a = a_ref[pl.ds(r0, SUBLANES), :]
        b = b_ref[pl.ds(r0, SUBLANES), :]
        for sft in (1, 2, 4):
            shift = SUBLANES - sft if reverse else sft
            keep = (row < SUBLANES - sft) if reverse else (row >= sft)
            a_sh = pltpu.roll(a, shift, axis=0)
            b_sh = pltpu.roll(b, shift, axis=0)
            b = jnp.where(keep, a * b_sh + b, b)
            a = jnp.where(keep, a * a_sh, a)
        hrows = a * h_in + b
        h_ref[pl.ds(r0, SUBLANES), :] = hrows
        return hrows[0:1, :] if reverse else hrows[SUBLANES - 1:SUBLANES, :]

    def fwd_body(t, h_in):
        r0 = pl.multiple_of(t * SUBLANES, SUBLANES)
        return scan_group(a0_ref, b0_ref, hf_ref, r0, h_in, False)

    def bwd_body(t, h_in):
        r0 = pl.multiple_of((ngroups - 1 - t) * SUBLANES, SUBLANES)
        return scan_group(a1_ref, b1_ref, hb_ref, r0, h_in, True)

    carry_ref[0:1, :] = lax.fori_loop(0, ngroups, fwd_body, carry_ref[0:1, :])
    carry_ref[1:2, :] = lax.fori_loop(0, ngroups, bwd_body, carry_ref[1:2, :])


def _lru(a0, b0, a1, b1, nb, nt, nc):
    rows = a0.shape[0]
    fwd = pl.BlockSpec((TM, B_WIDTH), lambda b, s: (b * nt + s, 0))
    bwd = pl.BlockSpec((TM, B_WIDTH), lambda b, s: (b * nt + _bwd_order(s, nc, nt), 0))
    return pl.pallas_call(
        _lru_kernel,
        grid=(nb, nt),
        in_specs=[fwd, fwd, bwd, bwd],
        out_specs=[fwd, bwd],
        out_shape=[jax.ShapeDtypeStruct((rows, B_WIDTH), F32)] * 2,
        scratch_shapes=[pltpu.VMEM((SUBLANES, B_WIDTH), F32)],
        compiler_params=_cparams(("arbitrary", "arbitrary")),
        name="rg_lru_scan",
    )(a0, b0, a1, b1)


def _merge_ab_kernel(of_ref, ob_ref, hf_ref, hb_ref, go_ref, yb_ref, x_ref, mod_ref, on_ref, w_ref, xo_ref):
    parts = []
    for h in range(A_HEADS):
        sl = slice(h * A_DK, (h + 1) * A_DK)
        o = of_ref[:, sl].astype(F32) + ob_ref[:, sl].astype(F32)
        n = o * lax.rsqrt(jnp.mean(o * o, axis=-1, keepdims=True) + EPS) * on_ref[...]
        parts.append((n * _silu(go_ref[:, sl].astype(F32))).astype(BF16))
    parts.append(((hf_ref[...] + hb_ref[...]) * _gelu_tanh(yb_ref[...].astype(F32))).astype(BF16))
    cat = jnp.concatenate(parts, axis=-1)
    y = jnp.dot(cat, w_ref[...], preferred_element_type=F32)
    xo_ref[...] = x_ref[...] + mod_ref[2:3, :] * y


def _merge_ab(o_f, o_b, h_f, h_b, gout, yb, xu, mods, onorm, w_out, nb, nt, nc):
    rows = xu.shape[0]
    half = pl.BlockSpec((TM, A_WIDTH), lambda i: (i, 0))
    wide = pl.BlockSpec((TM, D_MODEL), lambda i: (i, 0))
    mrow = lambda i: jnp.where(i % nt < nc, nb, i // nt)
    return pl.pallas_call(
        _merge_ab_kernel,
        grid=(rows // TM,),
        in_specs=[half] * 6 + [
            wide,
            pl.BlockSpec((None, 6, D_MODEL), lambda i: (mrow(i), 0, 0)),
            pl.BlockSpec((1, A_DK), lambda i: (0, 0)),
            pl.BlockSpec((D_MODEL, D_MODEL), lambda i: (0, 0)),
        ],
        out_specs=wide,
        out_shape=jax.ShapeDtypeStruct((rows, D_MODEL), F32),
        compiler_params=_cparams(("arbitrary",)),
        name="merge_ab",
    )(o_f, o_b, h_f, h_b, gout, yb, xu, mods, onorm, w_out)


def _inproj_attn_kernel(x_ref, mod_ref, n1_ref, w_ref, qn_ref, kn_ref, cos_ref, sin_ref, q_ref, k_ref, v_ref):
    h = _modulate(x_ref[...], n1_ref[...], mod_ref[0:1, :], mod_ref[1:2, :]).astype(BF16)
    z = jnp.dot(h, w_ref[...], preferred_element_type=F32)
    cos = cos_ref[...]
    sin = sin_ref[...]
    for hd in range(C_HEADS + C_KV_HEADS):
        xh = z[:, hd * C_HD:(hd + 1) * C_HD]
        gain = qn_ref[...] if hd < C_HEADS else kn_ref[...]
        n = xh * lax.rsqrt(jnp.mean(xh * xh, axis=-1, keepdims=True) + EPS) * gain
        rot = n * cos + pltpu.roll(n, C_HD // 2, axis=1) * sin
        if hd < C_HEADS:
            q_ref[:, hd * C_HD:(hd + 1) * C_HD] = (rot * (C_HD ** -0.5 * LOG2_E)).astype(BF16)
        else:
            kh = hd - C_HEADS
            k_ref[:, kh * C_HD:(kh + 1) * C_HD] = rot.astype(BF16)
    v_ref[...] = z[:, (C_HEADS + C_KV_HEADS) * C_HD:].astype(BF16)


def _inproj_attn(xu, mods, norm1, w_perm, qn, kn, cos_tab, sin_tab, nb, nt, nc):
    rows = xu.shape[0]
    nqk = (C_HEADS + 2 * C_KV_HEADS) * C_HD
    mrow = lambda i: jnp.where(i % nt < nc, nb, i // nt)
    full = lambda shape: pl.BlockSpec(shape, lambda i: (0,) * len(shape))
    tab = pl.BlockSpec((TM, C_HD), lambda i: (i % nt, 0))
    return pl.pallas_call(
        _inproj_attn_kernel,
        grid=(rows // TM,),
        in_specs=[
            pl.BlockSpec((TM, D_MODEL), lambda i: (i, 0)),
            pl.BlockSpec((None, 6, D_MODEL), lambda i: (mrow(i), 0, 0)),
            full((1, D_MODEL)),
            full((D_MODEL, nqk)),
            full((1, C_HD)),
            full((1, C_HD)),
            tab, tab,
        ],
        out_specs=[
            pl.BlockSpec((TM, C_HEADS * C_HD), lambda i: (i, 0)),
            pl.BlockSpec((TM, C_KV_HEADS * C_HD), lambda i: (i, 0)),
            pl.BlockSpec((TM, C_KV_HEADS * C_HD), lambda i: (i, 0)),
        ],
        out_shape=[
            jax.ShapeDtypeStruct((rows, C_HEADS * C_HD), BF16),
            jax.ShapeDtypeStruct((rows, C_KV_HEADS * C_HD), BF16),
            jax.ShapeDtypeStruct((rows, C_KV_HEADS * C_HD), BF16),
        ],
        compiler_params=_cparams(("arbitrary",)),
        name="inproj_attn",
    )(xu, mods, norm1, w_perm, qn, kn, cos_tab, sin_tab)


def _attn_kernel(q_ref, k_ref, v_ref, o_ref):
    k = k_ref[...]
    v_ext = jnp.concatenate([v_ref[...], jnp.ones(v_ref.shape, v_ref.dtype)], axis=-1)
    for g in range(C_GRP):
        sl = slice(g * C_HD, (g + 1) * C_HD)
        s = lax.dot_general(q_ref[:, sl], k, (((1,), (1,)), ((), ())), preferred_element_type=F32)
        p = jnp.exp2(s - jnp.max(s, axis=-1, keepdims=True))
        o_ext = jnp.dot(p.astype(BF16), v_ext, preferred_element_type=F32)
        o_ref[:, sl] = (o_ext[:, :C_HD] / o_ext[:, C_HD:]).astype(BF16)


def _attention(q, k, v, nb, nt, nc, p_rows):
    nq = nt - nc
    gw = C_GRP * C_HD
    return pl.pallas_call(
        _attn_kernel,
        grid=(nb, C_KV_HEADS, nq),
        in_specs=[
            pl.BlockSpec((TM, gw), lambda b, h, t: (b * nt + nc + t, h)),
            pl.BlockSpec((p_rows, C_HD), lambda b, h, t: (b, h)),
            pl.BlockSpec((p_rows, C_HD), lambda b, h, t: (b, h)),
        ],
        out_specs=pl.BlockSpec((TM, gw), lambda b, h, t: (b * nq + t, h)),
        out_shape=jax.ShapeDtypeStruct((nb * nq * TM, C_HEADS * C_HD), BF16),
        compiler_params=_cparams(("arbitrary", "arbitrary", "arbitrary")),
        name="gqa_attention",
    )(q, k, v)


def _outproj_kernel(a_ref, x_ref, mod_ref, w_ref, xo_ref):
    y = jnp.dot(a_ref[...], w_ref[...], preferred_element_type=F32)
    xo_ref[...] = x_ref[...] + mod_ref[2:3, :] * y


def _outproj_lat(a, xu, mods, w_out, nt, nc):
    rows = a.shape[0]
    nq = nt - nc
    return pl.pallas_call(
        _outproj_kernel,
        grid=(rows // TM,),
        in_specs=[
            pl.BlockSpec((TM, D_MODEL), lambda j: (j, 0)),
            pl.BlockSpec((TM, D_MODEL), lambda j: ((j // nq) * nt + nc + j % nq, 0)),
            pl.BlockSpec((None, 6, D_MODEL), lambda j: (j // nq, 0, 0)),
            pl.BlockSpec((D_MODEL, D_MODEL), lambda j: (0, 0)),
        ],
        out_specs=pl.BlockSpec((TM, D_MODEL), lambda j: (j, 0)),
        out_shape=jax.ShapeDtypeStruct((rows, D_MODEL), F32),
        compiler_params=_cparams(("arbitrary",)),
        name="outproj_attn",
    )(a, xu, mods, w_out)


TE = 256
GATHER_DEPTH = 4
ROW_TILES = D_MODEL // LANES


def _store_token_tiles(ref, val):
    for s in range(ROW_TILES):
        ref[pl.ds(s, val.shape[0], stride=ROW_TILES), :] = val[:, s * LANES:(s + 1) * LANES]


def _load_token_tiles(ref, s, rows):
    return ref[pl.ds(s, rows, stride=ROW_TILES), :]


def _token_tile(ref, row):
    return ref.at[pl.ds(pl.multiple_of(row * ROW_TILES, ROW_TILES), ROW_TILES), :]
SEL_E1, SEL_E2, SEL_W1, SEL_W2, SEL_R1, SEL_R2 = 0, 1, 2, 3, 4, 5


def _route_kernel(x_ref, mod_ref, n2_ref, wr_ref, br_ref, h_ref, sel_ref, cnt_ref, run_s):
    @pl.when(pl.program_id(0) == 0)
    def _():
        run_s[...] = jnp.zeros_like(run_s)

    h = _modulate(x_ref[...], n2_ref[...], mod_ref[3:4, :], mod_ref[4:5, :])
    _store_token_tiles(h_ref, h)
    lg = _dot_split(h, wr_ref[...]) + br_ref[...]
    lane = lax.broadcasted_iota(jnp.int32, lg.shape, 1)
    neg = jnp.float32(-jnp.inf)
    big = jnp.int32(1 << 20)
    is_grp = jnp.logical_and(lane >= N_EXPERTS, lane < N_EXPERTS + N_GROUPS)
    gl = jnp.where(is_grp, lg, neg)
    gmax = jnp.max(gl, axis=-1, keepdims=True)
    gidx = jnp.min(jnp.where(gl == gmax, lane - N_EXPERTS, big), axis=-1, keepdims=True)
    g_w = 1.0 / jnp.sum(jnp.where(is_grp, jnp.exp(gl - gmax), 0.0), axis=-1, keepdims=True)
    in_grp = jnp.logical_and(lane < N_EXPERTS, jnp.right_shift(lane, 3) == gidx)
    e1 = jnp.where(in_grp, lg, neg)
    m1 = jnp.max(e1, axis=-1, keepdims=True)
    i1 = jnp.min(jnp.where(e1 == m1, lane, big), axis=-1, keepdims=True)
    e2 = jnp.where(lane == i1, neg, e1)
    m2 = jnp.max(e2, axis=-1, keepdims=True)
    i2 = jnp.min(jnp.where(e2 == m2, lane, big), axis=-1, keepdims=True)
    t = jnp.exp(m2 - m1)
    w1 = g_w / (1.0 + t)
    w2 = g_w * t / (1.0 + t)

    hit1 = lane == i1
    hit2 = lane == i2
    onehot = jnp.where(hit1, 1.0, 0.0) + jnp.where(hit2, 1.0, 0.0)
    ri = lax.broadcasted_iota(jnp.int32, (TM, TM), 0)
    rj = lax.broadcasted_iota(jnp.int32, (TM, TM), 1)
    earlier = jnp.dot(jnp.where(rj < ri, 1.0, 0.0).astype(BF16), onehot.astype(BF16),
                      preferred_element_type=F32)
    base = run_s[0:1, :] + earlier
    r1 = jnp.sum(jnp.where(hit1, base, 0.0), axis=-1, keepdims=True)
    r2 = jnp.sum(jnp.where(hit2, base, 0.0), axis=-1, keepdims=True)
    total = run_s[0:1, :] + jnp.sum(onehot, axis=0, keepdims=True)
    run_s[0:1, :] = total
    cnt_ref[...] = jnp.broadcast_to(total, cnt_ref.shape)

    rec = jnp.zeros(lg.shape, F32)
    for ln, val in ((SEL_E1, i1.astype(F32)), (SEL_E2, i2.astype(F32)), (SEL_W1, w1), (SEL_W2, w2),
                    (SEL_R1, r1), (SEL_R2, r2)):
        rec = jnp.where(lane == ln, val, rec)
    sel_ref[...] = rec


def _route_call(x, mods, norm2, w_route, b_route, mrow):
    rows = x.shape[0]
    return pl.pallas_call(
        _route_kernel,
        grid=(rows // TM,),
        in_specs=[
            pl.BlockSpec((TM, D_MODEL), lambda i: (i, 0)),
            pl.BlockSpec((None, 6, D_MODEL), lambda i: (mrow(i), 0, 0)),
            pl.BlockSpec((1, D_MODEL), lambda i: (0, 0)),
            pl.BlockSpec((D_MODEL, LANES), lambda i: (0, 0)),
            pl.BlockSpec((1, LANES), lambda i: (0, 0)),
        ],
        out_specs=[
            pl.BlockSpec((TM * ROW_TILES, LANES), lambda i: (i, 0)),
            pl.BlockSpec((TM, LANES), lambda i: (i, 0)),
            pl.BlockSpec((SUBLANES, LANES), lambda i: (0, 0)),
        ],
        out_shape=[
            jax.ShapeDtypeStruct((rows * ROW_TILES, LANES), F32),
            jax.ShapeDtypeStruct((rows, LANES), F32),
            jax.ShapeDtypeStruct((SUBLANES, LANES), F32),
        ],
        scratch_shapes=[pltpu.VMEM((SUBLANES, LANES), F32)],
        compiler_params=_cparams(("arbitrary",)),
        name="moe_route",
    )(x, mods, norm2, w_route, b_route)


def _moe_plan(sel, cnt, n_tok, e_base):
    n_asg = 2 * n_tok
    max_tiles = n_asg // TE + N_EXPERTS
    eids = jnp.arange(N_EXPERTS, dtype=jnp.int32)
    counts = cnt[0, :N_EXPERTS].astype(jnp.int32)
    tiles_e = (counts + TE - 1) // TE
    cum_tiles = jnp.cumsum(tiles_e)
    n_tiles = cum_tiles[-1]
    off = (cum_tiles - tiles_e) * TE
    start = jnp.cumsum(counts) - counts
    e = sel[:, SEL_E1:SEL_E2 + 1].astype(jnp.int32)
    rank = sel[:, SEL_R1:SEL_R2 + 1].astype(jnp.int32)
    off_e = jnp.sum(jnp.where(e[:, :, None] == eids, off, 0), axis=-1)
    pos = (off_e + rank).reshape(n_asg)
    tok = jnp.arange(n_asg, dtype=jnp.int32) // 2
    _, tok_sorted = lax.sort_key_val(pos, tok)
    last_e = jnp.max(jnp.where(tiles_e > 0, eids, 0))
    t_idx = jnp.arange(max_tiles, dtype=jnp.int32)
    tile_e = jnp.minimum(jnp.sum((cum_tiles[None, :] <= t_idx[:, None]).astype(jnp.int32), axis=1), last_e)
    r_d = (t_idx * TE - off[tile_e])[:, None] + jnp.arange(TE, dtype=jnp.int32)[None, :]
    valid = jnp.logical_and(r_d < counts[tile_e][:, None], (t_idx < n_tiles)[:, None])
    idx = jnp.clip(start[tile_e][:, None] + r_d, 0, n_asg - 1)
    src = jnp.where(valid, tok_sorted[idx], 0).reshape(max_tiles * TE).astype(jnp.int32)
    prev_e = jnp.concatenate([tile_e[:1] - 1, tile_e[:-1]])
    first = jnp.logical_and(tile_e != prev_e, t_idx < n_tiles).astype(jnp.int32)
    run = jnp.cumsum(first) - 1
    later = jnp.logical_and(eids[None, :] > eids[:, None], (tiles_e > 0)[None, :])
    nxt_e = jnp.min(jnp.where(later, eids[None, :], N_EXPERTS), axis=1)
    nxt = jnp.where(nxt_e == N_EXPERTS, eids, nxt_e)[tile_e]
    meta = ((tile_e + e_base).astype(jnp.int32), n_tiles.reshape(1).astype(jnp.int32), src,
            first, run.astype(jnp.int32), (nxt + e_base).astype(jnp.int32))
    return meta, pos


def _moe_experts_kernel(te_ref, nt_ref, src_ref, first_ref, run_ref, nxt_ref,
                        h_hbm, wg_hbm, wu_hbm, wd_hbm, y_ref,
                        buf, xb, wg32, wu32, wd32, wg16, wu16, wd16, sem, wsem):
    t = pl.program_id(0)
    n_tiles = nt_ref[0]

    def row_copy(tok, slot, r):
        return pltpu.make_async_copy(_token_tile(h_hbm, tok), _token_tile(buf.at[slot], r), sem.at[slot])

    def weight_copies(e, slot):
        return [pltpu.make_async_copy(src.at[e], dst.at[slot], wsem.at[slot])
                for src, dst in ((wg_hbm, wg32), (wu_hbm, wu32), (wd_hbm, wd32))]

    @pl.when(t == 0)
    def _():
        for cp in weight_copies(te_ref[0], 0):
            cp.start()

    for k in range(GATHER_DEPTH - 1):
        @pl.when(jnp.logical_and(t == 0, k < n_tiles))
        def _(k=k):
            def body(r, carry):
                row_copy(src_ref[k * TE + r], k, r).start()
                return carry

            lax.fori_loop(0, TE, body, 0, unroll=8)

    run_starts = first_ref[t] != 0
    wslot = lax.rem(run_ref[t], 2)

    @pl.when(jnp.logical_and(run_starts, nxt_ref[t] != te_ref[t]))
    def _():
        for cp in weight_copies(nxt_ref[t], 1 - wslot):
            cp.start()

    @pl.when(run_starts)
    def _():
        for cp in weight_copies(te_ref[t], wslot):
            cp.wait()
        wg16[...] = wg32[wslot].astype(BF16)
        wu16[...] = wu32[wslot].astype(BF16)
        wd16[...] = wd32[wslot].astype(BF16)

    ahead = GATHER_DEPTH - 1

    def run_tile(prefetch):
        slot = lax.rem(t, GATHER_DEPTH)
        pltpu.make_async_copy(h_hbm.at[pl.ds(0, TE * ROW_TILES), :], buf.at[slot], sem.at[slot]).wait()
        for s in range(ROW_TILES):
            xb[:, s * LANES:(s + 1) * LANES] = _load_token_tiles(buf.at[slot], s, TE).astype(BF16)
        if prefetch:
            base = (t + ahead) * TE
            nslot = lax.rem(t + ahead, GATHER_DEPTH)
            for r in range(TE):
                row_copy(src_ref[base + r], nslot, r).start()
        x = xb[...]
        hg = jnp.dot(x, wg16[...], preferred_element_type=F32)
        hu = jnp.dot(x, wu16[...], preferred_element_type=F32)
        act = (_silu(hg) * hu).astype(BF16)
        _store_token_tiles(y_ref, jnp.dot(act, wd16[...], preferred_element_type=F32))

    @pl.when(t + ahead < n_tiles)
    def _():
        run_tile(True)

    @pl.when(jnp.logical_and(t < n_tiles, t + ahead >= n_tiles))
    def _():
        run_tile(False)

    @pl.when(t >= n_tiles)
    def _():
        y_ref[...] = jnp.zeros_like(y_ref)


def _moe_experts(h2, meta, w_gate, w_up, w_down):
    max_tiles = meta[0].shape[0]
    any_spec = pl.BlockSpec(memory_space=pl.ANY)
    return pl.pallas_call(
        _moe_experts_kernel,
        grid_spec=pltpu.PrefetchScalarGridSpec(
            num_scalar_prefetch=len(meta),
            grid=(max_tiles,),
            in_specs=[any_spec] * 4,
            out_specs=pl.BlockSpec((TE * ROW_TILES, LANES), lambda t, *_: (t, 0)),
            scratch_shapes=[
                pltpu.VMEM((GATHER_DEPTH, TE * ROW_TILES, LANES), F32),
                pltpu.VMEM((TE, D_MODEL), BF16),
                pltpu.VMEM((2, D_MODEL, D_EXPERT), F32),
                pltpu.VMEM((2, D_MODEL, D_EXPERT), F32),
                pltpu.VMEM((2, D_EXPERT, D_MODEL), F32),
                pltpu.VMEM((D_MODEL, D_EXPERT), BF16),
                pltpu.VMEM((D_MODEL, D_EXPERT), BF16),
                pltpu.VMEM((D_EXPERT, D_MODEL), BF16),
                pltpu.SemaphoreType.DMA((GATHER_DEPTH,)),
                pltpu.SemaphoreType.DMA((2,)),
            ],
        ),
        out_shape=jax.ShapeDtypeStruct((max_tiles * TE * ROW_TILES, LANES), F32),
        compiler_params=_cparams(("arbitrary",)),
        name="moe_experts",
    )(*meta, h2, w_gate, w_up, w_down)


def _moe_combine_kernel(final, pos_ref, y_hbm, x_ref, sel_ref, mod_ref, fn_ref, o_ref, buf, sem):
    i = pl.program_id(0)
    n = pl.num_programs(0)

    def row_copy(p, slot, k, r):
        return pltpu.make_async_copy(_token_tile(y_hbm, p), _token_tile(buf.at[slot, k], r), sem.at[slot])

    @pl.when(i == 0)
    def _():
        def body(r, carry):
            row_copy(pos_ref[2 * r], 0, 0, r).start()
            row_copy(pos_ref[2 * r + 1], 0, 1, r).start()
            return carry

        lax.fori_loop(0, TM, body, 0, unroll=8)

    slot = i % 2

    @pl.when(i + 1 < n)
    def _():
        base = (i + 1) * (2 * TM)
        for r in range(TM):
            row_copy(pos_ref[base + 2 * r], 1 - slot, 0, r).start()
            row_copy(pos_ref[base + 2 * r + 1], 1 - slot, 1, r).start()

    for k in range(2):
        pltpu.make_async_copy(y_hbm.at[pl.ds(0, TM * ROW_TILES), :], buf.at[slot, k], sem.at[slot]).wait()
    sel = sel_ref[...]
    w1 = sel[:, SEL_W1:SEL_W1 + 1]
    w2 = sel[:, SEL_W2:SEL_W2 + 1]
    parts = []
    for s in range(ROW_TILES):
        sl = slice(s * LANES, (s + 1) * LANES)
        y = w1 * _load_token_tiles(buf.at[slot, 0], s, TM) + w2 * _load_token_tiles(buf.at[slot, 1], s, TM)
        parts.append(x_ref[:, sl] + mod_ref[5:6, sl] * y)
    x = jnp.concatenate(parts, axis=-1)
    if final:
        x = x * lax.rsqrt(jnp.mean(x * x, axis=-1, keepdims=True) + EPS) * fn_ref[...]
    o_ref[...] = x


def _moe_combine(y_sorted, pos, x, sel, mods, final_norm, mrow, final):
    rows = x.shape[0]
    wide = pl.BlockSpec((TM, D_MODEL), lambda i, ps: (i, 0))
    return pl.pallas_call(
        functools.partial(_moe_combine_kernel, final),
        grid_spec=pltpu.PrefetchScalarGridSpec(
            num_scalar_prefetch=1,
            grid=(rows // TM,),
            in_specs=[
                pl.BlockSpec(memory_space=pl.ANY),
                wide,
                pl.BlockSpec((TM, LANES), lambda i, ps: (i, 0)),
                pl.BlockSpec((None, 6, D_MODEL), lambda i, ps: (mrow(i), 0, 0)),
                pl.BlockSpec((1, D_MODEL), lambda i, ps: (0, 0)),
            ],
            out_specs=wide,
            scratch_shapes=[pltpu.VMEM((2, 2, TM * ROW_TILES, LANES), F32), pltpu.SemaphoreType.DMA((2,))],
        ),
        out_shape=jax.ShapeDtypeStruct((rows, D_MODEL), F32),
        compiler_params=_cparams(("arbitrary",)),
        name="moe_combine",
    )(pos, y_sorted, x, sel, mods, final_norm)


def _moe_block(x, mods, norm2, w_grp, b_grp, w_exp, b_exp, w_gate, w_up, w_down, layer, final_norm, mrow, final):
    pad = LANES - N_EXPERTS - N_GROUPS
    w_route = jnp.concatenate([w_exp, w_grp, jnp.zeros((D_MODEL, pad), F32)], axis=1)
    b_route = jnp.concatenate([b_exp, b_grp, jnp.zeros((pad,), F32)]).reshape(1, LANES)
    h2, sel, cnt = _route_call(x, mods, norm2, w_route, b_route, mrow)
    meta, pos = _moe_plan(sel, cnt, x.shape[0], layer * N_EXPERTS)
    y_sorted = _moe_experts(h2, meta, w_gate, w_up, w_down)
    return _moe_combine(y_sorted, pos, x, sel, mods, final_norm, mrow, final)


def _ab_params(w_in, conv_qkv, a_log, dt_bias, conv_x, rg_wr, rg_br, rg_wi, rg_bi, rg_lam):
    o1 = 3 * A_WIDTH
    o2 = 4 * A_WIDTH
    o3 = o2 + 4 * A_HEADS
    o4 = o3 + B_WIDTH
    gate_cols = jnp.concatenate([w_in[:, o2:o3], jnp.zeros((D_MODEL, LANES - 4 * A_HEADS), F32)], axis=1)
    w_pad = jnp.concatenate([w_in[:, :o1], w_in[:, o3:o4], w_in[:, o1:o2], w_in[:, o4:], gate_cols],
                            axis=1).astype(BF16)
    z4 = jnp.zeros((A_HEADS,), F32)
    ztail = jnp.zeros((LANES - 4 * A_HEADS,), F32)
    alog_vec = jnp.concatenate([a_log[0], z4, a_log[1], z4, ztail]).reshape(1, LANES)
    dt_vec = jnp.concatenate([dt_bias[0], z4, dt_bias[1], z4, ztail]).reshape(1, LANES)
    eye = jnp.eye(B_BLOCKS, dtype=F32)
    bdiag = lambda w: jnp.einsum('gkj,gh->gkhj', w, eye).reshape(B_WIDTH, B_WIDTH)
    wbd = jnp.concatenate([bdiag(rg_wr[0]), bdiag(rg_wr[1]), bdiag(rg_wi[0]), bdiag(rg_wi[1])], axis=1)
    bbd = jnp.concatenate([rg_br[0].reshape(-1), rg_br[1].reshape(-1),
                           rg_bi[0].reshape(-1), rg_bi[1].reshape(-1)]).reshape(1, 4 * B_WIDTH)
    lam = rg_lam.reshape(1, 2 * B_WIDTH)
    return w_pad, alog_vec, dt_vec, wbd.astype(BF16), bbd, lam


def _attn_params(w_qkv, q_norm, k_norm, t_lat, c_rows):
    half = C_HD // 2
    perm = jnp.concatenate([jnp.arange(half) * 2, jnp.arange(half) * 2 + 1])
    nrot = (C_HEADS + C_KV_HEADS) * C_HD
    cols = (jnp.arange(C_HEADS + C_KV_HEADS)[:, None] * C_HD + perm[None, :]).reshape(-1)
    cols = jnp.concatenate([cols, jnp.arange(nrot, w_qkv.shape[1])])
    w_perm = w_qkv[:, cols].astype(BF16)
    qn = q_norm[perm].reshape(1, C_HD)
    kn = k_norm[perm].reshape(1, C_HD)
    return w_perm, qn, kn


def _rope_tables(t_lat, c_rows, grid_w):
    rows = t_lat // grid_w
    row = np.repeat(np.arange(rows, dtype=np.float64), grid_w)
    col = np.tile(np.arange(grid_w, dtype=np.float64), rows)
    n_freq = C_HD // 4
    inv = ROPE_THETA ** (-np.arange(n_freq, dtype=np.float64) / n_freq)
    ang = np.concatenate([row[:, None] * inv, col[:, None] * inv], axis=-1)
    cos = np.cos(ang).astype(np.float32)
    sin = np.sin(ang).astype(np.float32)
    cos_tab = np.concatenate([np.ones((c_rows, C_HD), np.float32), np.concatenate([cos, cos], axis=-1)], axis=0)
    sin_tab = np.concatenate([np.zeros((c_rows, C_HD), np.float32), np.concatenate([-sin, sin], axis=-1)], axis=0)
    return jnp.asarray(cos_tab), jnp.asarray(sin_tab)


GRID_W = 64


def kernel(x, c, ctx, c_ctx, ada_w, ada_b, norm1, norm2, final_norm, ab_w_in, ab_conv_qkv, ab_a_log, ab_dt_bias, ab_onorm, ab_conv_x, ab_rg_wr, ab_rg_br, ab_rg_wi, ab_rg_bi, ab_rg_lam, ab_w_out, at_w_qkv, at_q_norm, at_k_norm, at_w_out, moe_w_grp, moe_b_grp, moe_w_exp, moe_b_exp, moe_w_gate, moe_w_up, moe_w_down):
    nb, t_lat, _ = x.shape
    c_rows = ctx.shape[1]
    p_rows = c_rows + t_lat
    nt = p_rows // TM
    nc = c_rows // TM
    depth = ada_w.shape[0]
    assert depth == 2 and nb < 16 and c_rows % TM == 0 and t_lat % TM == 0

    cond = jnp.concatenate([c, c_ctx[None], jnp.zeros((16 - nb - 1, D_MODEL), F32)], axis=0)
    mods = _ada_mod(cond, ada_w, ada_b)
    xu = jnp.concatenate([ctx, x], axis=1).reshape(nb * p_rows, D_MODEL)
    uni_mrow = lambda i: jnp.where(i % nt < nc, nb, i // nt)
    fnorm = final_norm.reshape(1, D_MODEL)
    w_gate = moe_w_gate.reshape(depth * N_EXPERTS, D_MODEL, D_EXPERT)
    w_up = moe_w_up.reshape(depth * N_EXPERTS, D_MODEL, D_EXPERT)
    w_down = moe_w_down.reshape(depth * N_EXPERTS, D_EXPERT, D_MODEL)

    w_pad, alog_vec, dt_vec, wbd, bbd, lam = _ab_params(
        ab_w_in[0], ab_conv_qkv[0], ab_a_log[0], ab_dt_bias[0], ab_conv_x[0],
        ab_rg_wr[0], ab_rg_br[0], ab_rg_wi[0], ab_rg_bi[0], ab_rg_lam[0])
    q, k, v, gout, yb, gates, a0, b0, a1, b1 = _inproj_ab(
        xu, mods[0], norm1[0].reshape(1, D_MODEL), w_pad, ab_conv_qkv[0], ab_conv_x[0],
        alog_vec, dt_vec, wbd, bbd, lam, nb, nt, nc)
    o_f, o_b = _delta(q, k, v, gates, nb, p_rows, c_rows)
    h_f, h_b = _lru(a0, b0, a1, b1, nb, nt, nc)
    xu = _merge_ab(o_f, o_b, h_f, h_b, gout, yb, xu, mods[0], ab_onorm[0].reshape(1, A_DK),
                   ab_w_out[0].astype(BF16), nb, nt, nc)
    xu = _moe_block(xu, mods[0], norm2[0].reshape(1, D_MODEL), moe_w_grp[0], moe_b_grp[0], moe_w_exp[0],
                    moe_b_exp[0], w_gate, w_up, w_down, 0, fnorm, uni_mrow, False)

    w_perm, qn, kn = _attn_params(at_w_qkv[0], at_q_norm[0], at_k_norm[0], t_lat, c_rows)
    cos_tab, sin_tab = _rope_tables(t_lat, c_rows, GRID_W)
    q, k, v = _inproj_attn(xu, mods[1], norm1[1].reshape(1, D_MODEL), w_perm, qn, kn, cos_tab, sin_tab,
                           nb, nt, nc)
    att = _attention(q, k, v, nb, nt, nc, p_rows)
    xl = _outproj_lat(att, xu, mods[1], at_w_out[0].astype(BF16), nt, nc)
    nq = nt - nc
    xl = _moe_block(xl, mods[1], norm2[1].reshape(1, D_MODEL), moe_w_grp[1], moe_b_grp[1], moe_w_exp[1],
                    moe_b_exp[1], w_gate, w_up, w_down, 1, fnorm, lambda j: j // nq, True)
    return xl.reshape(nb, t_lat, D_MODEL)
```

```python
import functools
import math

import jax
import jax.numpy as jnp
import numpy as np
from jax import lax
from jax.experimental import pallas as pl
from jax.experimental.pallas import tpu as pltpu

F32 = jnp.float32
BF16 = jnp.bfloat16
HIGHEST = lax.Precision.HIGHEST

D_MODEL = 1024
EPS = 1e-6
TM = 256
LANES = 128
SUBLANES = 8

A_HEADS = 4
A_DK = 128
A_WIDTH = A_HEADS * A_DK
CHUNK = 64
CONV_W = 4
B_WIDTH = 512
B_BLOCKS = 8
B_BLK = B_WIDTH // B_BLOCKS
RG_C = 8.0

C_HEADS = 8
C_KV_HEADS = 2
C_HD = 128
C_GRP = C_HEADS // C_KV_HEADS
ROPE_THETA = 10000.0
LOG2_E = math.log2(math.e)

N_GROUPS = 4
EXP_PER_GROUP = 8
N_EXPERTS = N_GROUPS * EXP_PER_GROUP
D_EXPERT = 512

VMEM_LIMIT = 56 * 1024 * 1024

ZC_QKV = 0
ZC_XB = 3 * A_WIDTH
ZC_GOUT = ZC_XB + B_WIDTH
ZC_YB = ZC_GOUT + A_WIDTH
ZC_GATE = ZC_YB + B_WIDTH
ZC_TOTAL = ZC_GATE + LANES
ZC_CONV = ZC_GOUT
HALO = SUBLANES


def _cparams(sem):
    return pltpu.CompilerParams(dimension_semantics=sem, vmem_limit_bytes=VMEM_LIMIT)


def _sigmoid(x):
    return jax.nn.sigmoid(x)


def _silu(x):
    return x * jax.nn.sigmoid(x)


def _softplus(x):
    return jnp.maximum(x, 0.0) + jnp.log1p(jnp.exp(-jnp.abs(x)))


def _gelu_tanh(x):
    c = math.sqrt(2.0 / math.pi)
    return 0.5 * x * (1.0 + jnp.tanh(c * (x + 0.044715 * (x * x * x))))


def _dot_split(a, b):
    a_hi = a.astype(BF16)
    b_hi = b.astype(BF16)
    a_lo = (a - a_hi.astype(F32)).astype(BF16)
    b_lo = (b - b_hi.astype(F32)).astype(BF16)
    dot = functools.partial(jnp.dot, preferred_element_type=F32)
    return dot(a_hi, b_hi) + (dot(a_hi, b_lo) + dot(a_lo, b_hi))


def _modulate(x, gain, shift, scale):
    y = x * lax.rsqrt(jnp.mean(x * x, axis=-1, keepdims=True) + EPS)
    return (y * gain) * (1.0 + scale) + shift


def _ada_kernel(cond_ref, w_ref, b_ref, o_ref):
    s = _silu(cond_ref[...]).astype(BF16)
    o_ref[...] = jnp.dot(s, w_ref[...].astype(BF16), preferred_element_type=F32) + b_ref[...]


def _ada_mod(cond, ada_w, ada_b):
    depth = ada_w.shape[0]
    tn = 1536
    nn = 6 * D_MODEL // tn
    out = pl.pallas_call(
        _ada_kernel,
        grid=(depth, nn),
        in_specs=[
            pl.BlockSpec((16, D_MODEL), lambda l, n: (0, 0)),
            pl.BlockSpec((None, D_MODEL, tn), lambda l, n: (l, 0, n)),
            pl.BlockSpec((None, 1, tn), lambda l, n: (l, 0, n)),
        ],
        out_specs=pl.BlockSpec((None, 16, tn), lambda l, n: (l, 0, n)),
        out_shape=jax.ShapeDtypeStruct((depth, 16, 6 * D_MODEL), F32),
        compiler_params=_cparams(("arbitrary", "arbitrary")),
        name="ada_mod",
    )(cond, ada_w, ada_b.reshape(depth, 1, 6 * D_MODEL))
    return out.reshape(depth, 16, 6, D_MODEL)


def _inproj_ab_kernel(nt, nc, xc_ref, xp_ref, xn_ref, mod_ref, n1_ref, w_ref, cq_ref, cx_ref,
                      alog_ref, dt_ref, wbd_ref, bbd_ref, lam_ref,
                      q_ref, k_ref, v_ref, go_ref, yb_ref, g_ref, a0_ref, b0_ref, a1_ref, b1_ref,
                      zbuf, xcbuf):
    i = pl.program_id(0)
    r = i % nt
    is_ctx = r < nc
    prev_ok = jnp.logical_and(r > 0, ((r - 1) < nc) == is_ctx)
    next_ok = jnp.logical_and(r < nt - 1, ((r + 1) < nc) == is_ctx)

    gain = n1_ref[...]
    shift = mod_ref[0:1, :]
    scale = mod_ref[1:2, :]
    xall = jnp.concatenate([xp_ref[...], xc_ref[...], xn_ref[...]], axis=0)
    h = _modulate(xall, gain, shift, scale).astype(BF16)
    zbuf[...] = jnp.dot(h, w_ref[...], preferred_element_type=F32)
    zbuf[0:HALO, 0:ZC_CONV] = jnp.where(prev_ok, zbuf[0:HALO, 0:ZC_CONV], 0.0)
    zbuf[HALO + TM:, 0:ZC_CONV] = jnp.where(next_ok, zbuf[HALO + TM:, 0:ZC_CONV], 0.0)

    def conv(c0, w_taps_ref, wc0):
        acc = None
        for j in range(CONV_W):
            start = HALO - CONV_W // 2 + j
            term = zbuf[start:start + TM, c0:c0 + LANES] * w_taps_ref[j:j + 1, wc0:wc0 + LANES]
            acc = term if acc is None else acc + term
        return acc

    outs = (q_ref, k_ref, v_ref)
    for s in range(3 * A_HEADS):
        y = _silu(conv(s * LANES, cq_ref, s * LANES))
        if s < 2 * A_HEADS:
            y = y * lax.rsqrt(jnp.sum(y * y, axis=-1, keepdims=True) + EPS)
        if s < A_HEADS:
            y = y * (A_DK ** -0.5)
        hh = s % A_HEADS
        outs[s // A_HEADS][:, hh * LANES:(hh + 1) * LANES] = y

    for s in range(B_WIDTH // LANES):
        xcbuf[:, s * LANES:(s + 1) * LANES] = conv(ZC_XB + s * LANES, cx_ref, s * LANES)

    go_ref[...] = zbuf[HALO:HALO + TM, ZC_GOUT:ZC_GOUT + A_WIDTH].astype(BF16)
    yb_ref[...] = zbuf[HALO:HALO + TM, ZC_YB:ZC_YB + B_WIDTH].astype(BF16)

    zg = zbuf[HALO:HALO + TM, ZC_GATE:ZC_GATE + LANES]
    lane = lax.broadcasted_iota(jnp.int32, (TM, LANES), 1)
    dec = -jnp.exp(alog_ref[...]) * _softplus(zg + dt_ref[...])
    g_ref[...] = jnp.where(jnp.bitwise_and(lane, 7) < A_HEADS, dec, _sigmoid(zg))

    xc = xcbuf[...]
    rg = jnp.dot(xc.astype(BF16), wbd_ref[...], preferred_element_type=F32) + bbd_ref[...]
    sp = _softplus(-lam_ref[...])
    for d, (a_ref, b_ref) in enumerate(((a0_ref, b0_ref), (a1_ref, b1_ref))):
        rr = _sigmoid(rg[:, d * B_WIDTH:(d + 1) * B_WIDTH])
        ii = _sigmoid(rg[:, (2 + d) * B_WIDTH:(3 + d) * B_WIDTH])
        a = jnp.exp(-RG_C * rr * sp[:, d * B_WIDTH:(d + 1) * B_WIDTH])
        a_ref[...] = a
        b_ref[...] = jnp.sqrt(1.0 - a * a) * ii * xc


def _inproj_ab(xu, mods, norm1, w_pad, conv_qkv, conv_x, alog_vec, dt_vec, wbd, bbd, lam, nb, nt, nc):
    rows = xu.shape[0]
    ntiles = rows // TM
    hb = TM // HALO
    nhalo = rows // HALO

    def mrow(i):
        return jnp.where(i % nt < nc, nb, i // nt)

    full = lambda shape: pl.BlockSpec(shape, lambda i: (0,) * len(shape))
    row_spec = lambda w: pl.BlockSpec((TM, w), lambda i: (i, 0))
    sds = lambda w: jax.ShapeDtypeStruct((rows, w), F32)
    return pl.pallas_call(
        functools.partial(_inproj_ab_kernel, nt, nc),
        grid=(ntiles,),
        in_specs=[
            row_spec(D_MODEL),
            pl.BlockSpec((HALO, D_MODEL), lambda i: (jnp.maximum(i * hb - 1, 0), 0)),
            pl.BlockSpec((HALO, D_MODEL), lambda i: (jnp.minimum((i + 1) * hb, nhalo - 1), 0)),
            pl.BlockSpec((None, 6, D_MODEL), lambda i: (mrow(i), 0, 0)),
            full((1, D_MODEL)),
            full((D_MODEL, ZC_TOTAL)),
            full((CONV_W, 3 * A_WIDTH)),
            full((CONV_W, B_WIDTH)),
            full((1, LANES)),
            full((1, LANES)),
            full((B_WIDTH, 4 * B_WIDTH)),
            full((1, 4 * B_WIDTH)),
            full((1, 2 * B_WIDTH)),
        ],
        out_specs=[row_spec(A_WIDTH)] * 5 + [row_spec(LANES)] + [row_spec(B_WIDTH)] * 4,
        out_shape=[sds(A_WIDTH)] * 3 + [jax.ShapeDtypeStruct((rows, A_WIDTH), BF16)] * 2
        + [sds(LANES)] + [sds(B_WIDTH)] * 4,
        scratch_shapes=[pltpu.VMEM((TM + 2 * HALO, ZC_TOTAL), F32), pltpu.VMEM((TM, B_WIDTH), F32)],
        compiler_params=_cparams(("arbitrary",)),
        name="inproj_ab",
    )(xu, xu, xu, mods, norm1, w_pad, conv_qkv, conv_x, alog_vec, dt_vec, wbd, bbd, lam)


PAIR = 2 * CHUNK
N_STREAMS = 4
PREP_CHUNKS = 4
PREP_PROBLEMS = PREP_CHUNKS * N_STREAMS
SCAN_BATCH_MAX = 8


def _delta_prep_kernel(q_ref, k_ref, v_ref, g_ref, u_ref, wq_ref, at_ref, kt_ref, aux_ref,
                       gcum_s, dec_s, kq_s, rhs_s, n_s, pwf_s, l_s):
    ii = lax.broadcasted_iota(jnp.int32, (PAIR, PAIR), 0)
    jj = lax.broadcasted_iota(jnp.int32, (PAIR, PAIR), 1)
    same = (ii < CHUNK) == (jj < CHUNK)
    ci = lax.broadcasted_iota(jnp.int32, (CHUNK, CHUNK), 0)
    cj = lax.broadcasted_iota(jnp.int32, (CHUNK, CHUNK), 1)
    aux_ref[...] = jnp.zeros_like(aux_ref)
    problems = [(c, d, p) for c in range(PREP_CHUNKS) for d in range(2) for p in range(2)]

    for c in range(PREP_CHUNKS):
        gates = g_ref[c * CHUNK:(c + 1) * CHUNK, :]
        for d in range(2):
            tri = (ci >= cj) if d == 0 else (ci <= cj)
            gcum_s[2 * c + d] = jnp.dot(tri.astype(F32), gates, precision=HIGHEST, preferred_element_type=F32)

    for n, (c, d, p) in enumerate(problems):
        x = 2 * d + p
        rows = slice(c * CHUNK, (c + 1) * CHUNK)
        heads = (2 * p, 2 * p + 1)
        lanes = [d * 2 * A_HEADS + h for h in heads]
        last = CHUNK - 1 if d == 0 else 0
        lower = jnp.logical_and(same, (ii >= jj) if d == 0 else (ii <= jj))
        stack = lambda ref: jnp.concatenate([ref[rows, h * A_DK:(h + 1) * A_DK] for h in heads], axis=0)
        gcum = gcum_s[2 * c + d]
        gates = g_ref[rows, :]
        gc_col = jnp.concatenate([gcum[:, l:l + 1] for l in lanes], axis=0)
        beta = jnp.concatenate([gates[:, l + A_HEADS:l + A_HEADS + 1] for l in lanes], axis=0)
        g_last = [gcum[last:last + 1, l:l + 1] for l in lanes]
        gl_col = jnp.concatenate([jnp.broadcast_to(g, (CHUNK, 1)) for g in g_last], axis=0)
        gc_mat = jnp.broadcast_to(gc_col, (PAIR, PAIR))
        dec_s[n] = jnp.where(lower, jnp.exp(jnp.where(lower, gc_mat - gc_mat.T, 0.0)), 0.0)
        eg = jnp.exp(gc_col)
        q = stack(q_ref)
        k = stack(k_ref)
        kb = k * beta
        kq_s[n] = lax.dot_general(jnp.concatenate([kb, q], axis=0).astype(BF16), k.astype(BF16),
                                  (((1,), (1,)), ((), ())), preferred_element_type=F32)
        rhs_s[n] = jnp.concatenate([stack(v_ref) * beta, kb * eg], axis=-1)
        wq_ref[x, (2 * c + 1) * PAIR:(2 * c + 2) * PAIR, :] = (q * eg).astype(BF16)
        kt_ref[x, c * PAIR:(c + 1) * PAIR, :] = (k * jnp.exp(gl_col - gc_col)).T.astype(BF16)
        aux_ref[c * SUBLANES + x:c * SUBLANES + x + 1, :] = jnp.concatenate(
            [jnp.broadcast_to(jnp.exp(g), (1, A_DK)) for g in g_last], axis=-1)

    blk = lambda s: jnp.right_shift(ii, s) == jnp.right_shift(jj, s)
    for n, (c, d, p) in enumerate(problems):
        strict = jnp.logical_and(same, (ii > jj) if d == 0 else (ii < jj))
        dec = dec_s[n]
        lmat = jnp.where(strict, kq_s[n, :PAIR] * dec, 0.0)
        l_s[n] = lmat
        n_s[n] = jnp.where(blk(1), -lmat, 0.0)
        at_ref[2 * d + p, c * PAIR:(c + 1) * PAIR, :] = (kq_s[n, PAIR:] * dec).astype(BF16)

    for s in range(1, int(math.log2(CHUNK))):
        off = jnp.logical_and(blk(s + 1), jnp.logical_not(blk(s)))
        for n in range(PREP_PROBLEMS):
            l_off = jnp.where(off, l_s[n], 0.0)
            pwf_s[n] = l_off + jnp.dot(l_off.astype(BF16), n_s[n].astype(BF16), preferred_element_type=F32)
        for n in range(PREP_PROBLEMS):
            nm = n_s[n]
            xm = pwf_s[n]
            n_s[n] = nm - xm - jnp.dot(nm.astype(BF16), xm.astype(BF16), preferred_element_type=F32)

    for n, (c, d, p) in enumerate(problems):
        x = 2 * d + p
        rhs = rhs_s[n]
        sol = rhs + jnp.dot(n_s[n].astype(BF16), rhs.astype(BF16), preferred_element_type=F32)
        u_ref[x, c * PAIR:(c + 1) * PAIR, :] = sol[:, :A_DK]
        wq_ref[x, 2 * c * PAIR:(2 * c + 1) * PAIR, :] = sol[:, A_DK:].astype(BF16)


def _delta_scan_kernel(scan_batch, *refs):
    ins = refs[:4 * N_STREAMS]
    auxf_ref, auxb_ref, of_ref, ob_ref, s_ref, r_s, vbd_s = refs[4 * N_STREAMS:]
    insts = [(bb, x) for bb in range(scan_batch) for x in range(N_STREAMS)]

    @pl.when(pl.program_id(1) == 0)
    def _():
        s_ref[...] = jnp.zeros_like(s_ref)

    zeros = jnp.zeros((CHUNK, A_DK), F32)
    for n, (bb, x) in enumerate(insts):
        wq_ref = ins[4 * x + 1]
        r_s[n] = jnp.dot(wq_ref[bb], s_ref[n].astype(BF16), preferred_element_type=F32)
    for n, (bb, x) in enumerate(insts):
        d, p = divmod(x, 2)
        u_ref, _, at_ref, _ = ins[4 * x:4 * x + 4]
        o_ref = of_ref if d == 0 else ob_ref
        vn_a = u_ref[bb, :CHUNK, :] - r_s[n, :CHUNK, :A_DK]
        vn_b = u_ref[bb, CHUNK:, :] - r_s[n, CHUNK:PAIR, A_DK:]
        av = jnp.dot(at_ref[bb], jnp.concatenate([vn_a, vn_b], axis=0).astype(BF16),
                     preferred_element_type=F32)
        o_ref[bb, :, 2 * p * A_DK:(2 * p + 1) * A_DK] = (r_s[n, PAIR:PAIR + CHUNK, :A_DK] + av[:CHUNK]).astype(BF16)
        o_ref[bb, :, (2 * p + 1) * A_DK:(2 * p + 2) * A_DK] = (r_s[n, PAIR + CHUNK:, A_DK:] + av[CHUNK:]).astype(BF16)
        vbd_s[n] = jnp.concatenate([jnp.concatenate([vn_a, zeros], axis=1),
                                    jnp.concatenate([zeros, vn_b], axis=1)], axis=0).astype(BF16)
    for n, (bb, x) in enumerate(insts):
        aux_ref = auxf_ref if x < 2 else auxb_ref
        kt_ref = ins[4 * x + 3]
        s_ref[n] = s_ref[n] * aux_ref[bb, x:x + 1, :] + jnp.dot(kt_ref[bb], vbd_s[n],
                                                               preferred_element_type=F32)


def _bwd_order(s, n_ctx, n_all):
    return jnp.where(s < n_ctx, n_ctx - 1 - s, n_all - 1 - (s - n_ctx))


def _delta(q, k, v, g, nb, p_rows, c_rows):
    rows = q.shape[0]
    nchunks = rows // CHUNK
    pr = PREP_CHUNKS * CHUNK
    row_spec = lambda w: pl.BlockSpec((pr, w), lambda i: (i, 0))
    stream_spec = lambda m: pl.BlockSpec((N_STREAMS, PREP_CHUNKS * m, A_DK), lambda i: (0, i, 0))
    stream_sds = lambda m, dt: jax.ShapeDtypeStruct((N_STREAMS, nchunks * m, A_DK), dt)
    u, wq, at, kt, aux = pl.pallas_call(
        _delta_prep_kernel,
        grid=(nchunks // PREP_CHUNKS,),
        in_specs=[row_spec(A_WIDTH)] * 3 + [row_spec(LANES)],
        out_specs=[stream_spec(PAIR), stream_spec(2 * PAIR), stream_spec(PAIR), stream_spec(PAIR),
                   pl.BlockSpec((PREP_CHUNKS * SUBLANES, 2 * A_DK), lambda i: (i, 0))],
        out_shape=[stream_sds(PAIR, F32), stream_sds(2 * PAIR, BF16), stream_sds(PAIR, BF16),
                   stream_sds(PAIR, BF16), jax.ShapeDtypeStruct((nchunks * SUBLANES, 2 * A_DK), F32)],
        scratch_shapes=[
            pltpu.VMEM((2 * PREP_CHUNKS, CHUNK, LANES), F32),
            pltpu.VMEM((PREP_PROBLEMS, PAIR, PAIR), F32),
            pltpu.VMEM((PREP_PROBLEMS, 2 * PAIR, PAIR), F32),
            pltpu.VMEM((PREP_PROBLEMS, PAIR, 2 * A_DK), F32),
            pltpu.VMEM((PREP_PROBLEMS, PAIR, PAIR), F32),
            pltpu.VMEM((PREP_PROBLEMS, PAIR, PAIR), F32),
            pltpu.VMEM((PREP_PROBLEMS, PAIR, PAIR), F32),
        ],
        compiler_params=_cparams(("arbitrary",)),
        name="delta_prep",
    )(q, k, v, g)

    sb = math.gcd(nb, SCAN_BATCH_MAX)
    ng = nb // sb
    n_all = p_rows // CHUNK
    n_ctx = c_rows // CHUNK
    order = (lambda s: s, lambda s: _bwd_order(s, n_ctx, n_all))
    view = lambda arr, m: arr.reshape(N_STREAMS, ng, sb, n_all * m, A_DK)
    in_specs, args = [], []
    for x in range(N_STREAMS):
        pos = order[x // 2]
        for arr, m in ((u, PAIR), (wq, 2 * PAIR), (at, PAIR), (kt, PAIR)):
            in_specs.append(pl.BlockSpec((None, None, sb, m, A_DK),
                                         lambda b, s, x=x, pos=pos: (x, b, 0, pos(s), 0)))
            args.append(view(arr, m))
    for pos in order:
        in_specs.append(pl.BlockSpec((None, sb, SUBLANES, 2 * A_DK), lambda b, s, pos=pos: (b, 0, pos(s), 0)))
        args.append(aux.reshape(ng, sb, n_all * SUBLANES, 2 * A_DK))
    out_spec = lambda pos: pl.BlockSpec((None, sb, CHUNK, A_WIDTH), lambda b, s: (b, 0, pos(s), 0))
    n_inst = sb * N_STREAMS
    o_f, o_b = pl.pallas_call(
        functools.partial(_delta_scan_kernel, sb),
        grid=(ng, n_all),
        in_specs=in_specs,
        out_specs=[out_spec(order[0]), out_spec(order[1])],
        out_shape=[jax.ShapeDtypeStruct((ng, sb, p_rows, A_WIDTH), BF16)] * 2,
        scratch_shapes=[
            pltpu.VMEM((n_inst, A_DK, 2 * A_DK), F32),
            pltpu.VMEM((n_inst, 2 * PAIR, 2 * A_DK), F32),
            pltpu.VMEM((n_inst, PAIR, 2 * A_DK), BF16),
        ],
        compiler_params=_cparams(("arbitrary", "arbitrary")),
        name="delta_scan",
    )(*args)
    return o_f.reshape(rows, A_WIDTH), o_b.reshape(rows, A_WIDTH)


def _lru_kernel(a0_ref, b0_ref, a1_ref, b1_ref, hf_ref, hb_ref, carry_ref):
    @pl.when(pl.program_id(1) == 0)
    def _():
        carry_ref[...] = jnp.zeros_like(carry_ref)

    row = lax.broadcasted_iota(jnp.int32, (SUBLANES, B_WIDTH), 0)
    ngroups = TM // SUBLANES

    def scan_group(a_ref, b_ref, h_ref, r0, h_in, reverse):
        a = a_ref[pl.ds(r0, SUBLANES), :]
        b = b_ref[pl.ds(r0, SUBLANES), :]
        for sft in (1, 2, 4):
            shift = SUBLANES - sft if reverse else sft
            keep = (row < SUBLANES - sft) if reverse else (row >= sft)
            a_sh = pltpu.roll(a, shift, axis=0)
            b_sh = pltpu.roll(b, shift, axis=0)
            b = jnp.where(keep, a * b_sh + b, b)
            a = jnp.where(keep, a * a_sh, a)
        hrows = a * h_in + b
        h_ref[pl.ds(r0, SUBLANES), :] = hrows
        return hrows[0:1, :] if reverse else hrows[SUBLANES - 1:SUBLANES, :]

    def fwd_body(t, h_in):
        r0 = pl.multiple_of(t * SUBLANES, SUBLANES)
        return scan_group(a0_ref, b0_ref, hf_ref, r0, h_in, False)

    def bwd_body(t, h_in):
        r0 = pl.multiple_of((ngroups - 1 - t) * SUBLANES, SUBLANES)
        return scan_group(a1_ref, b1_ref, hb_ref, r0, h_in, True)

    carry_ref[0:1, :] = lax.fori_loop(0, ngroups, fwd_body, carry_ref[0:1, :])
    carry_ref[1:2, :] = lax.fori_loop(0, ngroups, bwd_body, carry_ref[1:2, :])


def _lru(a0, b0, a1, b1, nb, nt, nc):
    rows = a0.shape[0]
    fwd = pl.BlockSpec((TM, B_WIDTH), lambda b, s: (b * nt + s, 0))
    bwd = pl.BlockSpec((TM, B_WIDTH), lambda b, s: (b * nt + _bwd_order(s, nc, nt), 0))
    return pl.pallas_call(
        _lru_kernel,
        grid=(nb, nt),
        in_specs=[fwd, fwd, bwd, bwd],
        out_specs=[fwd, bwd],
        out_shape=[jax.ShapeDtypeStruct((rows, B_WIDTH), F32)] * 2,
        scratch_shapes=[pltpu.VMEM((SUBLANES, B_WIDTH), F32)],
        compiler_params=_cparams(("arbitrary", "arbitrary")),
        name="rg_lru_scan",
    )(a0, b0, a1, b1)


def _merge_ab_kernel(of_ref, ob_ref, hf_ref, hb_ref, go_ref, yb_ref, x_ref, mod_ref, on_ref, w_ref, xo_ref):
    parts = []
    for h in range(A_HEADS):
        sl = slice(h * A_DK, (h + 1) * A_DK)
        o = of_ref[:, sl].astype(F32) + ob_ref[:, sl].astype(F32)
        n = o * lax.rsqrt(jnp.mean(o * o, axis=-1, keepdims=True) + EPS) * on_ref[...]
        parts.append((n * _silu(go_ref[:, sl].astype(F32))).astype(BF16))
    parts.append(((hf_ref[...] + hb_ref[...]) * _gelu_tanh(yb_ref[...].astype(F32))).astype(BF16))
    cat = jnp.concatenate(parts, axis=-1)
    y = jnp.dot(cat, w_ref[...], preferred_element_type=F32)
    xo_ref[...] = x_ref[...] + mod_ref[2:3, :] * y


def _merge_ab(o_f, o_b, h_f, h_b, gout, yb, xu, mods, onorm, w_out, nb, nt, nc):
    rows = xu.shape[0]
    half = pl.BlockSpec((TM, A_WIDTH), lambda i: (i, 0))
    wide = pl.BlockSpec((TM, D_MODEL), lambda i: (i, 0))
    mrow = lambda i: jnp.where(i % nt < nc, nb, i // nt)
    return pl.pallas_call(
        _merge_ab_kernel,
        grid=(rows // TM,),
        in_specs=[half] * 6 + [
            wide,
            pl.BlockSpec((None, 6, D_MODEL), lambda i: (mrow(i), 0, 0)),
            pl.BlockSpec((1, A_DK), lambda i: (0, 0)),
            pl.BlockSpec((D_MODEL, D_MODEL), lambda i: (0, 0)),
        ],
        out_specs=wide,
        out_shape=jax.ShapeDtypeStruct((rows, D_MODEL), F32),
        compiler_params=_cparams(("arbitrary",)),
        name="merge_ab",
    )(o_f, o_b, h_f, h_b, gout, yb, xu, mods, onorm, w_out)


def _inproj_attn_kernel(x_ref, mod_ref, n1_ref, w_ref, qn_ref, kn_ref, cos_ref, sin_ref, q_ref, k_ref, v_ref):
    h = _modulate(x_ref[...], n1_ref[...], mod_ref[0:1, :], mod_ref[1:2, :]).astype(BF16)
    z = jnp.dot(h, w_ref[...], preferred_element_type=F32)
    cos = cos_ref[...]
    sin = sin_ref[...]
    for hd in range(C_HEADS + C_KV_HEADS):
        xh = z[:, hd * C_HD:(hd + 1) * C_HD]
        gain = qn_ref[...] if hd < C_HEADS else kn_ref[...]
        n = xh * lax.rsqrt(jnp.mean(xh * xh, axis=-1, keepdims=True) + EPS) * gain
        rot = n * cos + pltpu.roll(n, C_HD // 2, axis=1) * sin
        if hd < C_HEADS:
            q_ref[:, hd * C_HD:(hd + 1) * C_HD] = (rot * (C_HD ** -0.5 * LOG2_E)).astype(BF16)
        else:
            kh = hd - C_HEADS
            k_ref[:, kh * C_HD:(kh + 1) * C_HD] = rot.astype(BF16)
    v_ref[...] = z[:, (C_HEADS + C_KV_HEADS) * C_HD:].astype(BF16)


def _inproj_attn(xu, mods, norm1, w_perm, qn, kn, cos_tab, sin_tab, nb, nt, nc):
    rows = xu.shape[0]
    nqk = (C_HEADS + 2 * C_KV_HEADS) * C_HD
    mrow = lambda i: jnp.where(i % nt < nc, nb, i // nt)
    full = lambda shape: pl.BlockSpec(shape, lambda i: (0,) * len(shape))
    tab = pl.BlockSpec((TM, C_HD), lambda i: (i % nt, 0))
    return pl.pallas_call(
        _inproj_attn_kernel,
        grid=(rows // TM,),
        in_specs=[
            pl.BlockSpec((TM, D_MODEL), lambda i: (i, 0)),
            pl.BlockSpec((None, 6, D_MODEL), lambda i: (mrow(i), 0, 0)),
            full((1, D_MODEL)),
            full((D_MODEL, nqk)),
            full((1, C_HD)),
            full((1, C_HD)),
            tab, tab,
        ],
        out_specs=[
            pl.BlockSpec((TM, C_HEADS * C_HD), lambda i: (i, 0)),
            pl.BlockSpec((TM, C_KV_HEADS * C_HD), lambda i: (i, 0)),
            pl.BlockSpec((TM, C_KV_HEADS * C_HD), lambda i: (i, 0)),
        ],
        out_shape=[
            jax.ShapeDtypeStruct((rows, C_HEADS * C_HD), BF16),
            jax.ShapeDtypeStruct((rows, C_KV_HEADS * C_HD), BF16),
            jax.ShapeDtypeStruct((rows, C_KV_HEADS * C_HD), BF16),
        ],
        compiler_params=_cparams(("arbitrary",)),
        name="inproj_attn",
    )(xu, mods, norm1, w_perm, qn, kn, cos_tab, sin_tab)


def _attn_kernel(q_ref, k_ref, v_ref, o_ref):
    k = k_ref[...]
    v_ext = jnp.concatenate([v_ref[...], jnp.ones(v_ref.shape, v_ref.dtype)], axis=-1)
    for g in range(C_GRP):
        sl = slice(g * C_HD, (g + 1) * C_HD)
        s = lax.dot_general(q_ref[:, sl], k, (((1,), (1,)), ((), ())), preferred_element_type=F32)
        p = jnp.exp2(s - jnp.max(s, axis=-1, keepdims=True))
        o_ext = jnp.dot(p.astype(BF16), v_ext, preferred_element_type=F32)
        o_ref[:, sl] = (o_ext[:, :C_HD] / o_ext[:, C_HD:]).astype(BF16)


def _attention(q, k, v, nb, nt, nc, p_rows):
    nq = nt - nc
    gw = C_GRP * C_HD
    return pl.pallas_call(
        _attn_kernel,
        grid=(nb, C_KV_HEADS, nq),
        in_specs=[
            pl.BlockSpec((TM, gw), lambda b, h, t: (b * nt + nc + t, h)),
            pl.BlockSpec((p_rows, C_HD), lambda b, h, t: (b, h)),
            pl.BlockSpec((p_rows, C_HD), lambda b, h, t: (b, h)),
        ],
        out_specs=pl.BlockSpec((TM, gw), lambda b, h, t: (b * nq + t, h)),
        out_shape=jax.ShapeDtypeStruct((nb * nq * TM, C_HEADS * C_HD), BF16),
        compiler_params=_cparams(("arbitrary", "arbitrary", "arbitrary")),
        name="gqa_attention",
    )(q, k, v)


def _outproj_kernel(a_ref, x_ref, mod_ref, w_ref, xo_ref):
    y = jnp.dot(a_ref[...], w_ref[...], preferred_element_type=F32)
    xo_ref[...] = x_ref[...] + mod_ref[2:3, :] * y


def _outproj_lat(a, xu, mods, w_out, nt, nc):
    rows = a.shape[0]
    nq = nt - nc
    return pl.pallas_call(
        _outproj_kernel,
        grid=(rows // TM,),
        in_specs=[
            pl.BlockSpec((TM, D_MODEL), lambda j: (j, 0)),
            pl.BlockSpec((TM, D_MODEL), lambda j: ((j // nq) * nt + nc + j % nq, 0)),
            pl.BlockSpec((None, 6, D_MODEL), lambda j: (j // nq, 0, 0)),
            pl.BlockSpec((D_MODEL, D_MODEL), lambda j: (0, 0)),
        ],
        out_specs=pl.BlockSpec((TM, D_MODEL), lambda j: (j, 0)),
        out_shape=jax.ShapeDtypeStruct((rows, D_MODEL), F32),
        compiler_params=_cparams(("arbitrary",)),
        name="outproj_attn",
    )(a, xu, mods, w_out)


TE = 256
GATHER_DEPTH = 4
ROW_TILES = D_MODEL // LANES


def _store_token_tiles(ref, val):
    for s in range(ROW_TILES):
        ref[pl.ds(s, val.shape[0], stride=ROW_TILES), :] = val[:, s * LANES:(s + 1) * LANES]


def _load_token_tiles(ref, s, rows):
    return ref[pl.ds(s, rows, stride=ROW_TILES), :]


def _token_tile(ref, row):
    return ref.at[pl.ds(pl.multiple_of(row * ROW_TILES, ROW_TILES), ROW_TILES), :]
SEL_E1, SEL_E2, SEL_W1, SEL_W2, SEL_R1, SEL_R2 = 0, 1, 2, 3, 4, 5


def _route_kernel(x_ref, mod_ref, n2_ref, wr_ref, br_ref, h_ref, sel_ref, cnt_ref, run_s):
    @pl.when(pl.program_id(0) == 0)
    def _():
        run_s[...] = jnp.zeros_like(run_s)

    h = _modulate(x_ref[...], n2_ref[...], mod_ref[3:4, :], mod_ref[4:5, :])
    _store_token_tiles(h_ref, h)
    lg = _dot_split(h, wr_ref[...]) + br_ref[...]
    lane = lax.broadcasted_iota(jnp.int32, lg.shape, 1)
    neg = jnp.float32(-jnp.inf)
    big = jnp.int32(1 << 20)
    is_grp = jnp.logical_and(lane >= N_EXPERTS, lane < N_EXPERTS + N_GROUPS)
    gl = jnp.where(is_grp, lg, neg)
    gmax = jnp.max(gl, axis=-1, keepdims=True)
    gidx = jnp.min(jnp.where(gl == gmax, lane - N_EXPERTS, big), axis=-1, keepdims=True)
    g_w = 1.0 / jnp.sum(jnp.where(is_grp, jnp.exp(gl - gmax), 0.0), axis=-1, keepdims=True)
    in_grp = jnp.logical_and(lane < N_EXPERTS, jnp.right_shift(lane, 3) == gidx)
    e1 = jnp.where(in_grp, lg, neg)
    m1 = jnp.max(e1, axis=-1, keepdims=True)
    i1 = jnp.min(jnp.where(e1 == m1, lane, big), axis=-1, keepdims=True)
    e2 = jnp.where(lane == i1, neg, e1)
    m2 = jnp.max(e2, axis=-1, keepdims=True)
    i2 = jnp.min(jnp.where(e2 == m2, lane, big), axis=-1, keepdims=True)
    t = jnp.exp(m2 - m1)
    w1 = g_w / (1.0 + t)
    w2 = g_w * t / (1.0 + t)

    hit1 = lane == i1
    hit2 = lane == i2
    onehot = jnp.where(hit1, 1.0, 0.0) + jnp.where(hit2, 1.0, 0.0)
    ri = lax.broadcasted_iota(jnp.int32, (TM, TM), 0)
    rj = lax.broadcasted_iota(jnp.int32, (TM, TM), 1)
    earlier = jnp.dot(jnp.where(rj < ri, 1.0, 0.0).astype(BF16), onehot.astype(BF16),
                      preferred_element_type=F32)
    base = run_s[0:1, :] + earlier
    r1 = jnp.sum(jnp.where(hit1, base, 0.0), axis=-1, keepdims=True)
    r2 = jnp.sum(jnp.where(hit2, base, 0.0), axis=-1, keepdims=True)
    total = run_s[0:1, :] + jnp.sum(onehot, axis=0, keepdims=True)
    run_s[0:1, :] = total
    cnt_ref[...] = jnp.broadcast_to(total, cnt_ref.shape)

    rec = jnp.zeros(lg.shape, F32)
    for ln, val in ((SEL_E1, i1.astype(F32)), (SEL_E2, i2.astype(F32)), (SEL_W1, w1), (SEL_W2, w2),
                    (SEL_R1, r1), (SEL_R2, r2)):
        rec = jnp.where(lane == ln, val, rec)
    sel_ref[...] = rec


def _route_call(x, mods, norm2, w_route, b_route, mrow):
    rows = x.shape[0]
    return pl.pallas_call(
        _route_kernel,
        grid=(rows // TM,),
        in_specs=[
            pl.BlockSpec((TM, D_MODEL), lambda i: (i, 0)),
            pl.BlockSpec((None, 6, D_MODEL), lambda i: (mrow(i), 0, 0)),
            pl.BlockSpec((1, D_MODEL), lambda i: (0, 0)),
            pl.BlockSpec((D_MODEL, LANES), lambda i: (0, 0)),
            pl.BlockSpec((1, LANES), lambda i: (0, 0)),
        ],
        out_specs=[
            pl.BlockSpec((TM * ROW_TILES, LANES), lambda i: (i, 0)),
            pl.BlockSpec((TM, LANES), lambda i: (i, 0)),
            pl.BlockSpec((SUBLANES, LANES), lambda i: (0, 0)),
        ],
        out_shape=[
            jax.ShapeDtypeStruct((rows * ROW_TILES, LANES), F32),
            jax.ShapeDtypeStruct((rows, LANES), F32),
            jax.ShapeDtypeStruct((SUBLANES, LANES), F32),
        ],
        scratch_shapes=[pltpu.VMEM((SUBLANES, LANES), F32)],
        compiler_params=_cparams(("arbitrary",)),
        name="moe_route",
    )(x, mods, norm2, w_route, b_route)


def _moe_plan(sel, cnt, n_tok, e_base):
    n_asg = 2 * n_tok
    max_tiles = n_asg // TE + N_EXPERTS
    eids = jnp.arange(N_EXPERTS, dtype=jnp.int32)
    counts = cnt[0, :N_EXPERTS].astype(jnp.int32)
    tiles_e = (counts + TE - 1) // TE
    cum_tiles = jnp.cumsum(tiles_e)
    n_tiles = cum_tiles[-1]
    off = (cum_tiles - tiles_e) * TE
    start = jnp.cumsum(counts) - counts
    e = sel[:, SEL_E1:SEL_E2 + 1].astype(jnp.int32)
    rank = sel[:, SEL_R1:SEL_R2 + 1].astype(jnp.int32)
    off_e = jnp.sum(jnp.where(e[:, :, None] == eids, off, 0), axis=-1)
    pos = (off_e + rank).reshape(n_asg)
    tok = jnp.arange(n_asg, dtype=jnp.int32) // 2
    _, tok_sorted = lax.sort_key_val(pos, tok)
    last_e = jnp.max(jnp.where(tiles_e > 0, eids, 0))
    t_idx = jnp.arange(max_tiles, dtype=jnp.int32)
    tile_e = jnp.minimum(jnp.sum((cum_tiles[None, :] <= t_idx[:, None]).astype(jnp.int32), axis=1), last_e)
    r_d = (t_idx * TE - off[tile_e])[:, None] + jnp.arange(TE, dtype=jnp.int32)[None, :]
    valid = jnp.logical_and(r_d < counts[tile_e][:, None], (t_idx < n_tiles)[:, None])
    idx = jnp.clip(start[tile_e][:, None] + r_d, 0, n_asg - 1)
    filler = (t_idx * TE)[:, None] % n_tok + jnp.arange(TE, dtype=jnp.int32)[None, :]
    src = jnp.where(valid, tok_sorted[idx], filler).reshape(max_tiles * TE).astype(jnp.int32)
    prev_e = jnp.concatenate([tile_e[:1] - 1, tile_e[:-1]])
    first = jnp.logical_and(tile_e != prev_e, t_idx < n_tiles).astype(jnp.int32)
    run = jnp.cumsum(first) - 1
    later = jnp.logical_and(eids[None, :] > eids[:, None], (tiles_e > 0)[None, :])
    nxt_e = jnp.min(jnp.where(later, eids[None, :], N_EXPERTS), axis=1)
    nxt = jnp.where(nxt_e == N_EXPERTS, eids, nxt_e)[tile_e]
    meta = ((tile_e + e_base).astype(jnp.int32), n_tiles.reshape(1).astype(jnp.int32), src,
            first, run.astype(jnp.int32), (nxt + e_base).astype(jnp.int32))
    return meta, pos


def _moe_experts_kernel(te_ref, nt_ref, src_ref, first_ref, run_ref, nxt_ref,
                        h_hbm, wg_hbm, wu_hbm, wd_hbm, y_ref,
                        buf, xb, wg32, wu32, wd32, wg16, wu16, wd16, sem, wsem):
    t = pl.program_id(0)
    n_tiles = nt_ref[0]

    def row_copy(tok, slot, r):
        return pltpu.make_async_copy(_token_tile(h_hbm, tok), _token_tile(buf.at[slot], r), sem.at[slot])

    def weight_copies(e, slot):
        return [pltpu.make_async_copy(src.at[e], dst.at[slot], wsem.at[slot])
                for src, dst in ((wg_hbm, wg32), (wu_hbm, wu32), (wd_hbm, wd32))]

    @pl.when(t == 0)
    def _():
        for cp in weight_copies(te_ref[0], 0):
            cp.start()

    for k in range(GATHER_DEPTH - 1):
        @pl.when(jnp.logical_and(t == 0, k < n_tiles))
        def _(k=k):
            def body(r, carry):
                row_copy(src_ref[k * TE + r], k, r).start()
                return carry

            lax.fori_loop(0, TE, body, 0, unroll=8)

    run_starts = first_ref[t] != 0
    wslot = lax.rem(run_ref[t], 2)

    @pl.when(jnp.logical_and(run_starts, nxt_ref[t] != te_ref[t]))
    def _():
        for cp in weight_copies(nxt_ref[t], 1 - wslot):
            cp.start()

    @pl.when(run_starts)
    def _():
        for cp in weight_copies(te_ref[t], wslot):
            cp.wait()
        wg16[...] = wg32[wslot].astype(BF16)
        wu16[...] = wu32[wslot].astype(BF16)
        wd16[...] = wd32[wslot].astype(BF16)

    ahead = GATHER_DEPTH - 1

    def run_tile(prefetch):
        slot = lax.rem(t, GATHER_DEPTH)
        pltpu.make_async_copy(h_hbm.at[pl.ds(0, TE * ROW_TILES), :], buf.at[slot], sem.at[slot]).wait()
        for s in range(ROW_TILES):
            xb[:, s * LANES:(s + 1) * LANES] = _load_token_tiles(buf.at[slot], s, TE).astype(BF16)
        if prefetch:
            base = (t + ahead) * TE
            nslot = lax.rem(t + ahead, GATHER_DEPTH)
            for r in range(TE):
                row_copy(src_ref[base + r], nslot, r).start()
        x = xb[...]
        hg = jnp.dot(x, wg16[...], preferred_element_type=F32)
        hu = jnp.dot(x, wu16[...], preferred_element_type=F32)
        act = (_silu(hg) * hu).astype(BF16)
        _store_token_tiles(y_ref, jnp.dot(act, wd16[...], preferred_element_type=F32))

    @pl.when(t + ahead < n_tiles)
    def _():
        run_tile(True)

    @pl.when(jnp.logical_and(t < n_tiles, t + ahead >= n_tiles))
    def _():
        run_tile(False)

    @pl.when(t >= n_tiles)
    def _():
        y_ref[...] = jnp.zeros_like(y_ref)


def _moe_experts(h2, meta, w_gate, w_up, w_down):
    max_tiles = meta[0].shape[0]
    any_spec = pl.BlockSpec(memory_space=pl.ANY)
    return pl.pallas_call(
        _moe_experts_kernel,
        grid_spec=pltpu.PrefetchScalarGridSpec(
            num_scalar_prefetch=len(meta),
            grid=(max_tiles,),
            in_specs=[any_spec] * 4,
            out_specs=pl.BlockSpec((TE * ROW_TILES, LANES), lambda t, *_: (t, 0)),
            scratch_shapes=[
                pltpu.VMEM((GATHER_DEPTH, TE * ROW_TILES, LANES), F32),
                pltpu.VMEM((TE, D_MODEL), BF16),
                pltpu.VMEM((2, D_MODEL, D_EXPERT), F32),
                pltpu.VMEM((2, D_MODEL, D_EXPERT), F32),
                pltpu.VMEM((2, D_EXPERT, D_MODEL), F32),
                pltpu.VMEM((D_MODEL, D_EXPERT), BF16),
                pltpu.VMEM((D_MODEL, D_EXPERT), BF16),
                pltpu.VMEM((D_EXPERT, D_MODEL), BF16),
                pltpu.SemaphoreType.DMA((GATHER_DEPTH,)),
                pltpu.SemaphoreType.DMA((2,)),
            ],
        ),
        out_shape=jax.ShapeDtypeStruct((max_tiles * TE * ROW_TILES, LANES), F32),
        compiler_params=_cparams(("arbitrary",)),
        name="moe_experts",
    )(*meta, h2, w_gate, w_up, w_down)


def _moe_combine_kernel(final, pos_ref, y_hbm, x_ref, sel_ref, mod_ref, fn_ref, o_ref, buf, sem):
    i = pl.program_id(0)
    n = pl.num_programs(0)

    def row_copy(p, slot, k, r):
        return pltpu.make_async_copy(_token_tile(y_hbm, p), _token_tile(buf.at[slot, k], r), sem.at[slot])

    @pl.when(i == 0)
    def _():
        def body(r, carry):
            row_copy(pos_ref[2 * r], 0, 0, r).start()
            row_copy(pos_ref[2 * r + 1], 0, 1, r).start()
            return carry

        lax.fori_loop(0, TM, body, 0, unroll=8)

    slot = i % 2

    @pl.when(i + 1 < n)
    def _():
        base = (i + 1) * (2 * TM)
        for r in range(TM):
            row_copy(pos_ref[base + 2 * r], 1 - slot, 0, r).start()
            row_copy(pos_ref[base + 2 * r + 1], 1 - slot, 1, r).start()

    for k in range(2):
        pltpu.make_async_copy(y_hbm.at[pl.ds(0, TM * ROW_TILES), :], buf.at[slot, k], sem.at[slot]).wait()
    sel = sel_ref[...]
    w1 = sel[:, SEL_W1:SEL_W1 + 1]
    w2 = sel[:, SEL_W2:SEL_W2 + 1]
    parts = []
    for s in range(ROW_TILES):
        sl = slice(s * LANES, (s + 1) * LANES)
        y = w1 * _load_token_tiles(buf.at[slot, 0], s, TM) + w2 * _load_token_tiles(buf.at[slot, 1], s, TM)
        parts.append(x_ref[:, sl] + mod_ref[5:6, sl] * y)
    x = jnp.concatenate(parts, axis=-1)
    if final:
        x = x * lax.rsqrt(jnp.mean(x * x, axis=-1, keepdims=True) + EPS) * fn_ref[...]
    o_ref[...] = x


def _moe_combine(y_sorted, pos, x, sel, mods, final_norm, mrow, final):
    rows = x.shape[0]
    wide = pl.BlockSpec((TM, D_MODEL), lambda i, ps: (i, 0))
    return pl.pallas_call(
        functools.partial(_moe_combine_kernel, final),
        grid_spec=pltpu.PrefetchScalarGridSpec(
            num_scalar_prefetch=1,
            grid=(rows // TM,),
            in_specs=[
                pl.BlockSpec(memory_space=pl.ANY),
                wide,
                pl.BlockSpec((TM, LANES), lambda i, ps: (i, 0)),
                pl.BlockSpec((None, 6, D_MODEL), lambda i, ps: (mrow(i), 0, 0)),
                pl.BlockSpec((1, D_MODEL), lambda i, ps: (0, 0)),
            ],
            out_specs=wide,
            scratch_shapes=[pltpu.VMEM((2, 2, TM * ROW_TILES, LANES), F32), pltpu.SemaphoreType.DMA((2,))],
        ),
        out_shape=jax.ShapeDtypeStruct((rows, D_MODEL), F32),
        compiler_params=_cparams(("arbitrary",)),
        name="moe_combine",
    )(pos, y_sorted, x, sel, mods, final_norm)


def _moe_block(x, mods, norm2, w_grp, b_grp, w_exp, b_exp, w_gate, w_up, w_down, layer, final_norm, mrow, final):
    pad = LANES - N_EXPERTS - N_GROUPS
    w_route = jnp.concatenate([w_exp, w_grp, jnp.zeros((D_MODEL, pad), F32)], axis=1)
    b_route = jnp.concatenate([b_exp, b_grp, jnp.zeros((pad,), F32)]).reshape(1, LANES)
    h2, sel, cnt = _route_call(x, mods, norm2, w_route, b_route, mrow)
    meta, pos = _moe_plan(sel, cnt, x.shape[0], layer * N_EXPERTS)
    y_sorted = _moe_experts(h2, meta, w_gate, w_up, w_down)
    return _moe_combine(y_sorted, pos, x, sel, mods, final_norm, mrow, final)


def _ab_params(w_in, conv_qkv, a_log, dt_bias, conv_x, rg_wr, rg_br, rg_wi, rg_bi, rg_lam):
    o1 = 3 * A_WIDTH
    o2 = 4 * A_WIDTH
    o3 = o2 + 4 * A_HEADS
    o4 = o3 + B_WIDTH
    gate_cols = jnp.concatenate([w_in[:, o2:o3], jnp.zeros((D_MODEL, LANES - 4 * A_HEADS), F32)], axis=1)
    w_pad = jnp.concatenate([w_in[:, :o1], w_in[:, o3:o4], w_in[:, o1:o2], w_in[:, o4:], gate_cols],
                            axis=1).astype(BF16)
    z4 = jnp.zeros((A_HEADS,), F32)
    ztail = jnp.zeros((LANES - 4 * A_HEADS,), F32)
    alog_vec = jnp.concatenate([a_log[0], z4, a_log[1], z4, ztail]).reshape(1, LANES)
    dt_vec = jnp.concatenate([dt_bias[0], z4, dt_bias[1], z4, ztail]).reshape(1, LANES)
    eye = jnp.eye(B_BLOCKS, dtype=F32)
    bdiag = lambda w: jnp.einsum('gkj,gh->gkhj', w, eye).reshape(B_WIDTH, B_WIDTH)
    wbd = jnp.concatenate([bdiag(rg_wr[0]), bdiag(rg_wr[1]), bdiag(rg_wi[0]), bdiag(rg_wi[1])], axis=1)
    bbd = jnp.concatenate([rg_br[0].reshape(-1), rg_br[1].reshape(-1),
                           rg_bi[0].reshape(-1), rg_bi[1].reshape(-1)]).reshape(1, 4 * B_WIDTH)
    lam = rg_lam.reshape(1, 2 * B_WIDTH)
    return w_pad, alog_vec, dt_vec, wbd.astype(BF16), bbd, lam


def _attn_params(w_qkv, q_norm, k_norm, t_lat, c_rows):
    half = C_HD // 2
    perm = jnp.concatenate([jnp.arange(half) * 2, jnp.arange(half) * 2 + 1])
    nrot = (C_HEADS + C_KV_HEADS) * C_HD
    cols = (jnp.arange(C_HEADS + C_KV_HEADS)[:, None] * C_HD + perm[None, :]).reshape(-1)
    cols = jnp.concatenate([cols, jnp.arange(nrot, w_qkv.shape[1])])
    w_perm = w_qkv[:, cols].astype(BF16)
    qn = q_norm[perm].reshape(1, C_HD)
    kn = k_norm[perm].reshape(1, C_HD)
    return w_perm, qn, kn


def _rope_tables(t_lat, c_rows, grid_w):
    rows = t_lat // grid_w
    row = np.repeat(np.arange(rows, dtype=np.float64), grid_w)
    col = np.tile(np.arange(grid_w, dtype=np.float64), rows)
    n_freq = C_HD // 4
    inv = ROPE_THETA ** (-np.arange(n_freq, dtype=np.float64) / n_freq)
    ang = np.concatenate([row[:, None] * inv, col[:, None] * inv], axis=-1)
    cos = np.cos(ang).astype(np.float32)
    sin = np.sin(ang).astype(np.float32)
    cos_tab = np.concatenate([np.ones((c_rows, C_HD), np.float32), np.concatenate([cos, cos], axis=-1)], axis=0)
    sin_tab = np.concatenate([np.zeros((c_rows, C_HD), np.float32), np.concatenate([-sin, sin], axis=-1)], axis=0)
    return jnp.asarray(cos_tab), jnp.asarray(sin_tab)


GRID_W = 64


def kernel(x, c, ctx, c_ctx, ada_w, ada_b, norm1, norm2, final_norm, ab_w_in, ab_conv_qkv, ab_a_log, ab_dt_bias, ab_onorm, ab_conv_x, ab_rg_wr, ab_rg_br, ab_rg_wi, ab_rg_bi, ab_rg_lam, ab_w_out, at_w_qkv, at_q_norm, at_k_norm, at_w_out, moe_w_grp, moe_b_grp, moe_w_exp, moe_b_exp, moe_w_gate, moe_w_up, moe_w_down):
    nb, t_lat, _ = x.shape
    c_rows = ctx.shape[1]
    p_rows = c_rows + t_lat
    nt = p_rows // TM
    nc = c_rows // TM
    depth = ada_w.shape[0]
    assert depth == 2 and nb < 16 and c_rows % TM == 0 and t_lat % TM == 0

    cond = jnp.concatenate([c, c_ctx[None], jnp.zeros((16 - nb - 1, D_MODEL), F32)], axis=0)
    mods = _ada_mod(cond, ada_w, ada_b)
    xu = jnp.concatenate([ctx, x], axis=1).reshape(nb * p_rows, D_MODEL)
    uni_mrow = lambda i: jnp.where(i % nt < nc, nb, i // nt)
    fnorm = final_norm.reshape(1, D_MODEL)
    w_gate = moe_w_gate.reshape(depth * N_EXPERTS, D_MODEL, D_EXPERT)
    w_up = moe_w_up.reshape(depth * N_EXPERTS, D_MODEL, D_EXPERT)
    w_down = moe_w_down.reshape(depth * N_EXPERTS, D_EXPERT, D_MODEL)

    w_pad, alog_vec, dt_vec, wbd, bbd, lam = _ab_params(
        ab_w_in[0], ab_conv_qkv[0], ab_a_log[0], ab_dt_bias[0], ab_conv_x[0],
        ab_rg_wr[0], ab_rg_br[0], ab_rg_wi[0], ab_rg_bi[0], ab_rg_lam[0])
    q, k, v, gout, yb, gates, a0, b0, a1, b1 = _inproj_ab(
        xu, mods[0], norm1[0].reshape(1, D_MODEL), w_pad, ab_conv_qkv[0], ab_conv_x[0],
        alog_vec, dt_vec, wbd, bbd, lam, nb, nt, nc)
    o_f, o_b = _delta(q, k, v, gates, nb, p_rows, c_rows)
    h_f, h_b = _lru(a0, b0, a1, b1, nb, nt, nc)
    xu = _merge_ab(o_f, o_b, h_f, h_b, gout, yb, xu, mods[0], ab_onorm[0].reshape(1, A_DK),
                   ab_w_out[0].astype(BF16), nb, nt, nc)
    xu = _moe_block(xu, mods[0], norm2[0].reshape(1, D_MODEL), moe_w_grp[0], moe_b_grp[0], moe_w_exp[0],
                    moe_b_exp[0], w_gate, w_up, w_down, 0, fnorm, uni_mrow, False)

    w_perm, qn, kn = _attn_params(at_w_qkv[0], at_q_norm[0], at_k_norm[0], t_lat, c_rows)
    cos_tab, sin_tab = _rope_tables(t_lat, c_rows, GRID_W)
    q, k, v = _inproj_attn(xu, mods[1], norm1[1].reshape(1, D_MODEL), w_perm, qn, kn, cos_tab, sin_tab,
                           nb, nt, nc)
    att = _attention(q, k, v, nb, nt, nc, p_rows)
    xl = _outproj_lat(att, xu, mods[1], at_w_out[0].astype(BF16), nt, nc)
    nq = nt - nc
    xl = _moe_block(xl, mods[1], norm2[1].reshape(1, D_MODEL), moe_w_grp[1], moe_b_grp[1], moe_w_exp[1],
                    moe_b_exp[1], w_gate, w_up, w_down, 1, fnorm, lambda j: j // nq, True)
    return xl.reshape(nb, t_lat, D_MODEL)
```

```python
import functools
import math

import jax
import jax.numpy as jnp
import numpy as np
from jax import lax
from jax.experimental import pallas as pl
from jax.experimental.pallas import tpu as pltpu

F32 = jnp.float32
BF16 = jnp.bfloat16
HIGHEST = lax.Precision.HIGHEST

D_MODEL = 1024
EPS = 1e-6
TM = 256
LANES = 128
SUBLANES = 8

A_HEADS = 4
A_DK = 128
A_WIDTH = A_HEADS * A_DK
CHUNK = 64
CONV_W = 4
B_WIDTH = 512
B_BLOCKS = 8
B_BLK = B_WIDTH // B_BLOCKS
RG_C = 8.0

C_HEADS = 8
C_KV_HEADS = 2
C_HD = 128
C_GRP = C_HEADS // C_KV_HEADS
ROPE_THETA = 10000.0
LOG2_E = math.log2(math.e)

N_GROUPS = 4
EXP_PER_GROUP = 8
N_EXPERTS = N_GROUPS * EXP_PER_GROUP
D_EXPERT = 512

VMEM_LIMIT = 56 * 1024 * 1024

ZC_QKV = 0
ZC_XB = 3 * A_WIDTH
ZC_GOUT = ZC_XB + B_WIDTH
ZC_YB = ZC_GOUT + A_WIDTH
ZC_GATE = ZC_YB + B_WIDTH
ZC_TOTAL = ZC_GATE + LANES
ZC_CONV = ZC_GOUT
HALO = SUBLANES


def _cparams(sem):
    return pltpu.CompilerParams(dimension_semantics=sem, vmem_limit_bytes=VMEM_LIMIT)


def _sigmoid(x):
    return jax.nn.sigmoid(x)


def _silu(x):
    return x * jax.nn.sigmoid(x)


def _softplus(x):
    return jnp.maximum(x, 0.0) + jnp.log1p(jnp.exp(-jnp.abs(x)))


def _gelu_tanh(x):
    c = math.sqrt(2.0 / math.pi)
    return 0.5 * x * (1.0 + jnp.tanh(c * (x + 0.044715 * (x * x * x))))


def _dot_split(a, b):
    a_hi = a.astype(BF16)
    b_hi = b.astype(BF16)
    a_lo = (a - a_hi.astype(F32)).astype(BF16)
    b_lo = (b - b_hi.astype(F32)).astype(BF16)
    dot = functools.partial(jnp.dot, preferred_element_type=F32)
    return dot(a_hi, b_hi) + (dot(a_hi, b_lo) + dot(a_lo, b_hi))


def _modulate(x, gain, shift, scale):
    y = x * lax.rsqrt(jnp.mean(x * x, axis=-1, keepdims=True) + EPS)
    return (y * gain) * (1.0 + scale) + shift


def _ada_kernel(cond_ref, w_ref, b_ref, o_ref):
    s = _silu(cond_ref[...]).astype(BF16)
    o_ref[...] = jnp.dot(s, w_ref[...].astype(BF16), preferred_element_type=F32) + b_ref[...]


def _ada_mod(cond, ada_w, ada_b):
    depth = ada_w.shape[0]
    tn = 1536
    nn = 6 * D_MODEL // tn
    out = pl.pallas_call(
        _ada_kernel,
        grid=(depth, nn),
        in_specs=[
            pl.BlockSpec((16, D_MODEL), lambda l, n: (0, 0)),
            pl.BlockSpec((None, D_MODEL, tn), lambda l, n: (l, 0, n)),
            pl.BlockSpec((None, 1, tn), lambda l, n: (l, 0, n)),
        ],
        out_specs=pl.BlockSpec((None, 16, tn), lambda l, n: (l, 0, n)),
        out_shape=jax.ShapeDtypeStruct((depth, 16, 6 * D_MODEL), F32),
        compiler_params=_cparams(("arbitrary", "arbitrary")),
        name="ada_mod",
    )(cond, ada_w, ada_b.reshape(depth, 1, 6 * D_MODEL))
    return out.reshape(depth, 16, 6, D_MODEL)


def _inproj_ab_kernel(nt, nc, xc_ref, xp_ref, xn_ref, mod_ref, n1_ref, w_ref, cq_ref, cx_ref,
                      alog_ref, dt_ref, wbd_ref, bbd_ref, lam_ref,
                      q_ref, k_ref, v_ref, go_ref, yb_ref, g_ref, a0_ref, b0_ref, a1_ref, b1_ref,
                      zbuf, xcbuf):
    i = pl.program_id(0)
    r = i % nt
    is_ctx = r < nc
    prev_ok = jnp.logical_and(r > 0, ((r - 1) < nc) == is_ctx)
    next_ok = jnp.logical_and(r < nt - 1, ((r + 1) < nc) == is_ctx)

    gain = n1_ref[...]
    shift = mod_ref[0:1, :]
    scale = mod_ref[1:2, :]
    xall = jnp.concatenate([xp_ref[...], xc_ref[...], xn_ref[...]], axis=0)
    h = _modulate(xall, gain, shift, scale).astype(BF16)
    zbuf[...] = jnp.dot(h, w_ref[...], preferred_element_type=F32)
    zbuf[0:HALO, 0:ZC_CONV] = jnp.where(prev_ok, zbuf[0:HALO, 0:ZC_CONV], 0.0)
    zbuf[HALO + TM:, 0:ZC_CONV] = jnp.where(next_ok, zbuf[HALO + TM:, 0:ZC_CONV], 0.0)

    def conv(c0, w_taps_ref, wc0):
        acc = None
        for j in range(CONV_W):
            start = HALO - CONV_W // 2 + j
            term = zbuf[start:start + TM, c0:c0 + LANES] * w_taps_ref[j:j + 1, wc0:wc0 + LANES]
            acc = term if acc is None else acc + term
        return acc

    outs = (q_ref, k_ref, v_ref)
    for s in range(3 * A_HEADS):
        y = _silu(conv(s * LANES, cq_ref, s * LANES))
        if s < 2 * A_HEADS:
            y = y * lax.rsqrt(jnp.sum(y * y, axis=-1, keepdims=True) + EPS)
        if s < A_HEADS:
            y = y * (A_DK ** -0.5)
        hh = s % A_HEADS
        outs[s // A_HEADS][:, hh * LANES:(hh + 1) * LANES] = y

    for s in range(B_WIDTH // LANES):
        xcbuf[:, s * LANES:(s + 1) * LANES] = conv(ZC_XB + s * LANES, cx_ref, s * LANES)

    go_ref[...] = zbuf[HALO:HALO + TM, ZC_GOUT:ZC_GOUT + A_WIDTH].astype(BF16)
    yb_ref[...] = zbuf[HALO:HALO + TM, ZC_YB:ZC_YB + B_WIDTH].astype(BF16)

    zg = zbuf[HALO:HALO + TM, ZC_GATE:ZC_GATE + LANES]
    lane = lax.broadcasted_iota(jnp.int32, (TM, LANES), 1)
    dec = -jnp.exp(alog_ref[...]) * _softplus(zg + dt_ref[...])
    g_ref[...] = jnp.where(jnp.bitwise_and(lane, 7) < A_HEADS, dec, _sigmoid(zg))

    xc = xcbuf[...]
    rg = jnp.dot(xc.astype(BF16), wbd_ref[...], preferred_element_type=F32) + bbd_ref[...]
    sp = _softplus(-lam_ref[...])
    for d, (a_ref, b_ref) in enumerate(((a0_ref, b0_ref), (a1_ref, b1_ref))):
        rr = _sigmoid(rg[:, d * B_WIDTH:(d + 1) * B_WIDTH])
        ii = _sigmoid(rg[:, (2 + d) * B_WIDTH:(3 + d) * B_WIDTH])
        a = jnp.exp(-RG_C * rr * sp[:, d * B_WIDTH:(d + 1) * B_WIDTH])
        a_ref[...] = a
        b_ref[...] = jnp.sqrt(1.0 - a * a) * ii * xc


def _inproj_ab(xu, mods, norm1, w_pad, conv_qkv, conv_x, alog_vec, dt_vec, wbd, bbd, lam, nb, nt, nc):
    rows = xu.shape[0]
    ntiles = rows // TM
    hb = TM // HALO
    nhalo = rows // HALO

    def mrow(i):
        return jnp.where(i % nt < nc, nb, i // nt)

    full = lambda shape: pl.BlockSpec(shape, lambda i: (0,) * len(shape))
    row_spec = lambda w: pl.BlockSpec((TM, w), lambda i: (i, 0))
    sds = lambda w: jax.ShapeDtypeStruct((rows, w), F32)
    return pl.pallas_call(
        functools.partial(_inproj_ab_kernel, nt, nc),
        grid=(ntiles,),
        in_specs=[
            row_spec(D_MODEL),
            pl.BlockSpec((HALO, D_MODEL), lambda i: (jnp.maximum(i * hb - 1, 0), 0)),
            pl.BlockSpec((HALO, D_MODEL), lambda i: (jnp.minimum((i + 1) * hb, nhalo - 1), 0)),
            pl.BlockSpec((None, 6, D_MODEL), lambda i: (mrow(i), 0, 0)),
            full((1, D_MODEL)),
            full((D_MODEL, ZC_TOTAL)),
            full((CONV_W, 3 * A_WIDTH)),
            full((CONV_W, B_WIDTH)),
            full((1, LANES)),
            full((1, LANES)),
            full((B_WIDTH, 4 * B_WIDTH)),
            full((1, 4 * B_WIDTH)),
            full((1, 2 * B_WIDTH)),
        ],
        out_specs=[row_spec(A_WIDTH)] * 5 + [row_spec(LANES)] + [row_spec(B_WIDTH)] * 4,
        out_shape=[sds(A_WIDTH)] * 3 + [jax.ShapeDtypeStruct((rows, A_WIDTH), BF16)] * 2
        + [sds(LANES)] + [sds(B_WIDTH)] * 4,
        scratch_shapes=[pltpu.VMEM((TM + 2 * HALO, ZC_TOTAL), F32), pltpu.VMEM((TM, B_WIDTH), F32)],
        compiler_params=_cparams(("arbitrary",)),
        name="inproj_ab",
    )(xu, xu, xu, mods, norm1, w_pad, conv_qkv, conv_x, alog_vec, dt_vec, wbd, bbd, lam)


PAIR = 2 * CHUNK
N_STREAMS = 4
PREP_CHUNKS = 4
PREP_PROBLEMS = PREP_CHUNKS * N_STREAMS
SCAN_BATCH_MAX = 8


def _delta_prep_kernel(q_ref, k_ref, v_ref, g_ref, u_ref, wq_ref, at_ref, kt_ref, aux_ref,
                       gcum_s, dec_s, kq_s, rhs_s, n_s, pwf_s, l_s):
    ii = lax.broadcasted_iota(jnp.int32, (PAIR, PAIR), 0)
    jj = lax.broadcasted_iota(jnp.int32, (PAIR, PAIR), 1)
    same = (ii < CHUNK) == (jj < CHUNK)
    ci = lax.broadcasted_iota(jnp.int32, (CHUNK, CHUNK), 0)
    cj = lax.broadcasted_iota(jnp.int32, (CHUNK, CHUNK), 1)
    aux_ref[...] = jnp.zeros_like(aux_ref)
    problems = [(c, d, p) for c in range(PREP_CHUNKS) for d in range(2) for p in range(2)]

    for c in range(PREP_CHUNKS):
        gates = g_ref[c * CHUNK:(c + 1) * CHUNK, :]
        for d in range(2):
            tri = (ci >= cj) if d == 0 else (ci <= cj)
            gcum_s[2 * c + d] = jnp.dot(tri.astype(F32), gates, precision=HIGHEST, preferred_element_type=F32)

    for n, (c, d, p) in enumerate(problems):
        x = 2 * d + p
        rows = slice(c * CHUNK, (c + 1) * CHUNK)
        heads = (2 * p, 2 * p + 1)
        lanes = [d * 2 * A_HEADS + h for h in heads]
        last = CHUNK - 1 if d == 0 else 0
        lower = jnp.logical_and(same, (ii >= jj) if d == 0 else (ii <= jj))
        stack = lambda ref: jnp.concatenate([ref[rows, h * A_DK:(h + 1) * A_DK] for h in heads], axis=0)
        gcum = gcum_s[2 * c + d]
        gates = g_ref[rows, :]
        gc_col = jnp.concatenate([gcum[:, l:l + 1] for l in lanes], axis=0)
        beta = jnp.concatenate([gates[:, l + A_HEADS:l + A_HEADS + 1] for l in lanes], axis=0)
        g_last = [gcum[last:last + 1, l:l + 1] for l in lanes]
        gl_col = jnp.concatenate([jnp.broadcast_to(g, (CHUNK, 1)) for g in g_last], axis=0)
        gc_mat = jnp.broadcast_to(gc_col, (PAIR, PAIR))
        dec_s[n] = jnp.where(lower, jnp.exp(jnp.where(lower, gc_mat - gc_mat.T, 0.0)), 0.0)
        eg = jnp.exp(gc_col)
        q = stack(q_ref)
        k = stack(k_ref)
        kb = k * beta
        kq_s[n] = lax.dot_general(jnp.concatenate([kb, q], axis=0).astype(BF16), k.astype(BF16),
                                  (((1,), (1,)), ((), ())), preferred_element_type=F32)
        rhs_s[n] = jnp.concatenate([stack(v_ref) * beta, kb * eg], axis=-1)
        wq_ref[x, (2 * c + 1) * PAIR:(2 * c + 2) * PAIR, :] = (q * eg).astype(BF16)
        kt_ref[x, c * PAIR:(c + 1) * PAIR, :] = (k * jnp.exp(gl_col - gc_col)).T.astype(BF16)
        aux_ref[c * SUBLANES + x:c * SUBLANES + x + 1, :] = jnp.concatenate(
            [jnp.broadcast_to(jnp.exp(g), (1, A_DK)) for g in g_last], axis=-1)

    blk = lambda s: jnp.right_shift(ii, s) == jnp.right_shift(jj, s)
    for n, (c, d, p) in enumerate(problems):
        strict = jnp.logical_and(same, (ii > jj) if d == 0 else (ii < jj))
        dec = dec_s[n]
        lmat = jnp.where(strict, kq_s[n, :PAIR] * dec, 0.0)
        l_s[n] = lmat
        n_s[n] = jnp.where(blk(1), -lmat, 0.0)
        at_ref[2 * d + p, c * PAIR:(c + 1) * PAIR, :] = (kq_s[n, PAIR:] * dec).astype(BF16)

    for s in range(1, int(math.log2(CHUNK))):
        off = jnp.logical_and(blk(s + 1), jnp.logical_not(blk(s)))
        for n in range(PREP_PROBLEMS):
            l_off = jnp.where(off, l_s[n], 0.0)
            pwf_s[n] = l_off + jnp.dot(l_off.astype(BF16), n_s[n].astype(BF16), preferred_element_type=F32)
        for n in range(PREP_PROBLEMS):
            nm = n_s[n]
            xm = pwf_s[n]
            n_s[n] = nm - xm - jnp.dot(nm.astype(BF16), xm.astype(BF16), preferred_element_type=F32)

    for n, (c, d, p) in enumerate(problems):
        x = 2 * d + p
        rhs = rhs_s[n]
        sol = rhs + jnp.dot(n_s[n].astype(BF16), rhs.astype(BF16), preferred_element_type=F32)
        u_ref[x, c * PAIR:(c + 1) * PAIR, :] = sol[:, :A_DK]
        wq_ref[x, 2 * c * PAIR:(2 * c + 1) * PAIR, :] = sol[:, A_DK:].astype(BF16)


def _delta_scan_kernel(scan_batch, *refs):
    ins = refs[:4 * N_STREAMS]
    auxf_ref, auxb_ref, of_ref, ob_ref, s_ref, r_s, vbd_s = refs[4 * N_STREAMS:]
    insts = [(bb, x) for bb in range(scan_batch) for x in range(N_STREAMS)]

    @pl.when(pl.program_id(1) == 0)
    def _():
        s_ref[...] = jnp.zeros_like(s_ref)

    zeros = jnp.zeros((CHUNK, A_DK), F32)
    for n, (bb, x) in enumerate(insts):
        wq_ref = ins[4 * x + 1]
        r_s[n] = jnp.dot(wq_ref[bb], s_ref[n].astype(BF16), preferred_element_type=F32)
    for n, (bb, x) in enumerate(insts):
        d, p = divmod(x, 2)
        u_ref, _, at_ref, _ = ins[4 * x:4 * x + 4]
        o_ref = of_ref if d == 0 else ob_ref
        vn_a = u_ref[bb, :CHUNK, :] - r_s[n, :CHUNK, :A_DK]
        vn_b = u_ref[bb, CHUNK:, :] - r_s[n, CHUNK:PAIR, A_DK:]
        av = jnp.dot(at_ref[bb], jnp.concatenate([vn_a, vn_b], axis=0).astype(BF16),
                     preferred_element_type=F32)
        o_ref[bb, :, 2 * p * A_DK:(2 * p + 1) * A_DK] = (r_s[n, PAIR:PAIR + CHUNK, :A_DK] + av[:CHUNK]).astype(BF16)
        o_ref[bb, :, (2 * p + 1) * A_DK:(2 * p + 2) * A_DK] = (r_s[n, PAIR + CHUNK:, A_DK:] + av[CHUNK:]).astype(BF16)
        vbd_s[n] = jnp.concatenate([jnp.concatenate([vn_a, zeros], axis=1),
                                    jnp.concatenate([zeros, vn_b], axis=1)], axis=0).astype(BF16)
    for n, (bb, x) in enumerate(insts):
        aux_ref = auxf_ref if x < 2 else auxb_ref
        kt_ref = ins[4 * x + 3]
        s_ref[n] = s_ref[n] * aux_ref[bb, x:x + 1, :] + jnp.dot(kt_ref[bb], vbd_s[n],
                                                               preferred_element_type=F32)


def _bwd_order(s, n_ctx, n_all):
    return jnp.where(s < n_ctx, n_ctx - 1 - s, n_all - 1 - (s - n_ctx))


def _delta(q, k, v, g, nb, p_rows, c_rows):
    rows = q.shape[0]
    nchunks = rows // CHUNK
    pr = PREP_CHUNKS * CHUNK
    row_spec = lambda w: pl.BlockSpec((pr, w), lambda i: (i, 0))
    stream_spec = lambda m: pl.BlockSpec((N_STREAMS, PREP_CHUNKS * m, A_DK), lambda i: (0, i, 0))
    stream_sds = lambda m, dt: jax.ShapeDtypeStruct((N_STREAMS, nchunks * m, A_DK), dt)
    u, wq, at, kt, aux = pl.pallas_call(
        _delta_prep_kernel,
        grid=(nchunks // PREP_CHUNKS,),
        in_specs=[row_spec(A_WIDTH)] * 3 + [row_spec(LANES)],
        out_specs=[stream_spec(PAIR), stream_spec(2 * PAIR), stream_spec(PAIR), stream_spec(PAIR),
                   pl.BlockSpec((PREP_CHUNKS * SUBLANES, 2 * A_DK), lambda i: (i, 0))],
        out_shape=[stream_sds(PAIR, F32), stream_sds(2 * PAIR, BF16), stream_sds(PAIR, BF16),
                   stream_sds(PAIR, BF16), jax.ShapeDtypeStruct((nchunks * SUBLANES, 2 * A_DK), F32)],
        scratch_shapes=[
            pltpu.VMEM((2 * PREP_CHUNKS, CHUNK, LANES), F32),
            pltpu.VMEM((PREP_PROBLEMS, PAIR, PAIR), F32),
            pltpu.VMEM((PREP_PROBLEMS, 2 * PAIR, PAIR), F32),
            pltpu.VMEM((PREP_PROBLEMS, PAIR, 2 * A_DK), F32),
            pltpu.VMEM((PREP_PROBLEMS, PAIR, PAIR), F32),
            pltpu.VMEM((PREP_PROBLEMS, PAIR, PAIR), F32),
            pltpu.VMEM((PREP_PROBLEMS, PAIR, PAIR), F32),
        ],
        compiler_params=_cparams(("arbitrary",)),
        name="delta_prep",
    )(q, k, v, g)

    sb = math.gcd(nb, SCAN_BATCH_MAX)
    ng = nb // sb
    n_all = p_rows // CHUNK
    n_ctx = c_rows // CHUNK
    order = (lambda s: s, lambda s: _bwd_order(s, n_ctx, n_all))
    view = lambda arr, m: arr.reshape(N_STREAMS, ng, sb, n_all * m, A_DK)
    in_specs, args = [], []
    for x in range(N_STREAMS):
        pos = order[x // 2]
        for arr, m in ((u, PAIR), (wq, 2 * PAIR), (at, PAIR), (kt, PAIR)):
            in_specs.append(pl.BlockSpec((None, None, sb, m, A_DK),
                                         lambda b, s, x=x, pos=pos: (x, b, 0, pos(s), 0)))
            args.append(view(arr, m))
    for pos in order:
        in_specs.append(pl.BlockSpec((None, sb, SUBLANES, 2 * A_DK), lambda b, s, pos=pos: (b, 0, pos(s), 0)))
        args.append(aux.reshape(ng, sb, n_all * SUBLANES, 2 * A_DK))
    out_spec = lambda pos: pl.BlockSpec((None, sb, CHUNK, A_WIDTH), lambda b, s: (b, 0, pos(s), 0))
    n_inst = sb * N_STREAMS
    o_f, o_b = pl.pallas_call(
        functools.partial(_delta_scan_kernel, sb),
        grid=(ng, n_all),
        in_specs=in_specs,
        out_specs=[out_spec(order[0]), out_spec(order[1])],
        out_shape=[jax.ShapeDtypeStruct((ng, sb, p_rows, A_WIDTH), BF16)] * 2,
        scratch_shapes=[
            pltpu.VMEM((n_inst, A_DK, 2 * A_DK), F32),
            pltpu.VMEM((n_inst, 2 * PAIR, 2 * A_DK), F32),
            pltpu.VMEM((n_inst, PAIR, 2 * A_DK), BF16),
        ],
        compiler_params=_cparams(("arbitrary", "arbitrary")),
        name="delta_scan",
    )(*args)
    return o_f.reshape(rows, A_WIDTH), o_b.reshape(rows, A_WIDTH)


def _lru_kernel(a0_ref, b0_ref, a1_ref, b1_ref, hf_ref, hb_ref, carry_ref):
    @pl.when(pl.program_id(1) == 0)
    def _():
        carry_ref[...] = jnp.zeros_like(carry_ref)

    row = lax.broadcasted_iota(jnp.int32, (SUBLANES, B_WIDTH), 0)
    ngroups = TM // SUBLANES

    def scan_group(a_ref, b_ref, h_ref, r0, h_in, reverse):
        a = a_ref[pl.ds(r0, SUBLANES), :]
        b = b_ref[pl.ds(r0, SUBLANES), :]
        for sft in (1, 2, 4):
            shift = SUBLANES - sft if reverse else sft
            keep = (row < SUBLANES - sft) if reverse else (row >= sft)
            a_sh = pltpu.roll(a, shift, axis=0)
            b_sh = pltpu.roll(b, shift, axis=0)
            b = jnp.where(keep, a * b_sh + b, b)
            a = jnp.where(keep, a * a_sh, a)
        hrows = a * h_in + b
        h_ref[pl.ds(r0, SUBLANES), :] = hrows
        return hrows[0:1, :] if reverse else hrows[SUBLANES - 1:SUBLANES, :]

    def fwd_body(t, h_in):
        r0 = pl.multiple_of(t * SUBLANES, SUBLANES)
        return scan_group(a0_ref, b0_ref, hf_ref, r0, h_in, False)

    def bwd_body(t, h_in):
        r0 = pl.multiple_of((ngroups - 1 - t) * SUBLANES, SUBLANES)
        return scan_group(a1_ref, b1_ref, hb_ref, r0, h_in, True)

    carry_ref[0:1, :] = lax.fori_loop(0, ngroups, fwd_body, carry_ref[0:1, :])
    carry_ref[1:2, :] = lax.fori_loop(0, ngroups, bwd_body, carry_ref[1:2, :])


def _lru(a0, b0, a1, b1, nb, nt, nc):
    rows = a0.shape[0]
    fwd = pl.BlockSpec((TM, B_WIDTH), lambda b, s: (b * nt + s, 0))
    bwd = pl.BlockSpec((TM, B_WIDTH), lambda b, s: (b * nt + _bwd_order(s, nc, nt), 0))
    return pl.pallas_call(
        _lru_kernel,
        grid=(nb, nt),
        in_specs=[fwd, fwd, bwd, bwd],
        out_specs=[fwd, bwd],
        out_shape=[jax.ShapeDtypeStruct((rows, B_WIDTH), F32)] * 2,
        scratch_shapes=[pltpu.VMEM((SUBLANES, B_WIDTH), F32)],
        compiler_params=_cparams(("arbitrary", "arbitrary")),
        name="rg_lru_scan",
    )(a0, b0, a1, b1)


def _merge_ab_kernel(of_ref, ob_ref, hf_ref, hb_ref, go_ref, yb_ref, x_ref, mod_ref, on_ref, w_ref, xo_ref):
    parts = []
    for h in range(A_HEADS):
        sl = slice(h * A_DK, (h + 1) * A_DK)
        o = of_ref[:, sl].astype(F32) + ob_ref[:, sl].astype(F32)
        n = o * lax.rsqrt(jnp.mean(o * o, axis=-1, keepdims=True) + EPS) * on_ref[...]
        parts.append((n * _silu(go_ref[:, sl].astype(F32))).astype(BF16))
    parts.append(((hf_ref[...] + hb_ref[...]) * _gelu_tanh(yb_ref[...].astype(F32))).astype(BF16))
    cat = jnp.concatenate(parts, axis=-1)
    y = jnp.dot(cat, w_ref[...], preferred_element_type=F32)
    xo_ref[...] = x_ref[...] + mod_ref[2:3, :] * y


def _merge_ab(o_f, o_b, h_f, h_b, gout, yb, xu, mods, onorm, w_out, nb, nt, nc):
    rows = xu.shape[0]
    half = pl.BlockSpec((TM, A_WIDTH), lambda i: (i, 0))
    wide = pl.BlockSpec((TM, D_MODEL), lambda i: (i, 0))
    mrow = lambda i: jnp.where(i % nt < nc, nb, i // nt)
    return pl.pallas_call(
        _merge_ab_kernel,
        grid=(rows // TM,),
        in_specs=[half] * 6 + [
            wide,
            pl.BlockSpec((None, 6, D_MODEL), lambda i: (mrow(i), 0, 0)),
            pl.BlockSpec((1, A_DK), lambda i: (0, 0)),
            pl.BlockSpec((D_MODEL, D_MODEL), lambda i: (0, 0)),
        ],
        out_specs=wide,
        out_shape=jax.ShapeDtypeStruct((rows, D_MODEL), F32),
        compiler_params=_cparams(("arbitrary",)),
        name="merge_ab",
    )(o_f, o_b, h_f, h_b, gout, yb, xu, mods, onorm, w_out)


def _inproj_attn_kernel(x_ref, mod_ref, n1_ref, w_ref, qn_ref, kn_ref, cos_ref, sin_ref, q_ref, k_ref, v_ref):
    h = _modulate(x_ref[...], n1_ref[...], mod_ref[0:1, :], mod_ref[1:2, :]).astype(BF16)
    z = jnp.dot(h, w_ref[...], preferred_element_type=F32)
    cos = cos_ref[...]
    sin = sin_ref[...]
    for hd in range(C_HEADS + C_KV_HEADS):
        xh = z[:, hd * C_HD:(hd + 1) * C_HD]
        gain = qn_ref[...] if hd < C_HEADS else kn_ref[...]
        n = xh * lax.rsqrt(jnp.mean(xh * xh, axis=-1, keepdims=True) + EPS) * gain
        rot = n * cos + pltpu.roll(n, C_HD // 2, axis=1) * sin
        if hd < C_HEADS:
            q_ref[:, hd * C_HD:(hd + 1) * C_HD] = (rot * (C_HD ** -0.5 * LOG2_E)).astype(BF16)
        else:
            kh = hd - C_HEADS
            k_ref[:, kh * C_HD:(kh + 1) * C_HD] = rot.astype(BF16)
    v_ref[...] = z[:, (C_HEADS + C_KV_HEADS) * C_HD:].astype(BF16)


def _inproj_attn(xu, mods, norm1, w_perm, qn, kn, cos_tab, sin_tab, nb, nt, nc):
    rows = xu.shape[0]
    nqk = (C_HEADS + 2 * C_KV_HEADS) * C_HD
    mrow = lambda i: jnp.where(i % nt < nc, nb, i // nt)
    full = lambda shape: pl.BlockSpec(shape, lambda i: (0,) * len(shape))
    tab = pl.BlockSpec((TM, C_HD), lambda i: (i % nt, 0))
    return pl.pallas_call(
        _inproj_attn_kernel,
        grid=(rows // TM,),
        in_specs=[
            pl.BlockSpec((TM, D_MODEL), lambda i: (i, 0)),
            pl.BlockSpec((None, 6, D_MODEL), lambda i: (mrow(i), 0, 0)),
            full((1, D_MODEL)),
            full((D_MODEL, nqk)),
            full((1, C_HD)),
            full((1, C_HD)),
            tab, tab,
        ],
        out_specs=[
            pl.BlockSpec((TM, C_HEADS * C_HD), lambda i: (i, 0)),
            pl.BlockSpec((TM, C_KV_HEADS * C_HD), lambda i: (i, 0)),
            pl.BlockSpec((TM, C_KV_HEADS * C_HD), lambda i: (i, 0)),
        ],
        out_shape=[
            jax.ShapeDtypeStruct((rows, C_HEADS * C_HD), BF16),
            jax.ShapeDtypeStruct((rows, C_KV_HEADS * C_HD), BF16),
            jax.ShapeDtypeStruct((rows, C_KV_HEADS * C_HD), BF16),
        ],
        compiler_params=_cparams(("arbitrary",)),
        name="inproj_attn",
    )(xu, mods, norm1, w_perm, qn, kn, cos_tab, sin_tab)


def _attn_kernel(q_ref, k_ref, v_ref, o_ref):
    k = k_ref[...]
    v_ext = jnp.concatenate([v_ref[...], jnp.ones(v_ref.shape, v_ref.dtype)], axis=-1)
    for g in range(C_GRP):
        sl = slice(g * C_HD, (g + 1) * C_HD)
        s = lax.dot_general(q_ref[:, sl], k, (((1,), (1,)), ((), ())), preferred_element_type=F32)
        p = jnp.exp2(s - jnp.max(s, axis=-1, keepdims=True))
        o_ext = jnp.dot(p.astype(BF16), v_ext, preferred_element_type=F32)
        o_ref[:, sl] = (o_ext[:, :C_HD] / o_ext[:, C_HD:]).astype(BF16)


def _attention(q, k, v, nb, nt, nc, p_rows):
    nq = nt - nc
    gw = C_GRP * C_HD
    return pl.pallas_call(
        _attn_kernel,
        grid=(nb, C_KV_HEADS, nq),
        in_specs=[
            pl.BlockSpec((TM, gw), lambda b, h, t: (b * nt + nc + t, h)),
            pl.BlockSpec((p_rows, C_HD), lambda b, h, t: (b, h)),
            pl.BlockSpec((p_rows, C_HD), lambda b, h, t: (b, h)),
        ],
        out_specs=pl.BlockSpec((TM, gw), lambda b, h, t: (b * nq + t, h)),
        out_shape=jax.ShapeDtypeStruct((nb * nq * TM, C_HEADS * C_HD), BF16),
        compiler_params=_cparams(("arbitrary", "arbitrary", "arbitrary")),
        name="gqa_attention",
    )(q, k, v)


def _outproj_kernel(a_ref, x_ref, mod_ref, w_ref, xo_ref):
    y = jnp.dot(a_ref[...], w_ref[...], preferred_element_type=F32)
    xo_ref[...] = x_ref[...] + mod_ref[2:3, :] * y


def _outproj_lat(a, xu, mods, w_out, nt, nc):
    rows = a.shape[0]
    nq = nt - nc
    return pl.pallas_call(
        _outproj_kernel,
        grid=(rows // TM,),
        in_specs=[
            pl.BlockSpec((TM, D_MODEL), lambda j: (j, 0)),
            pl.BlockSpec((TM, D_MODEL), lambda j: ((j // nq) * nt + nc + j % nq, 0)),
            pl.BlockSpec((None, 6, D_MODEL), lambda j: (j // nq, 0, 0)),
            pl.BlockSpec((D_MODEL, D_MODEL), lambda j: (0, 0)),
        ],
        out_specs=pl.BlockSpec((TM, D_MODEL), lambda j: (j, 0)),
        out_shape=jax.ShapeDtypeStruct((rows, D_MODEL), F32),
        compiler_params=_cparams(("arbitrary",)),
        name="outproj_attn",
    )(a, xu, mods, w_out)


TE = 256
GATHER_DEPTH = 4
ROW_TILES = D_MODEL // LANES


def _store_token_tiles(ref, val):
    for s in range(ROW_TILES):
        ref[pl.ds(s, val.shape[0], stride=ROW_TILES), :] = val[:, s * LANES:(s + 1) * LANES]


def _load_token_tiles(ref, s, rows):
    return ref[pl.ds(s, rows, stride=ROW_TILES), :]


def _token_tile(ref, row):
    return ref.at[pl.ds(pl.multiple_of(row * ROW_TILES, ROW_TILES), ROW_TILES), :]
SEL_E1, SEL_E2, SEL_W1, SEL_W2, SEL_R1, SEL_R2 = 0, 1, 2, 3, 4, 5


def _route_kernel(x_ref, mod_ref, n2_ref, wr_ref, br_ref, h_ref, sel_ref, cnt_ref, run_s):
    @pl.when(pl.program_id(0) == 0)
    def _():
        run_s[...] = jnp.zeros_like(run_s)

    h = _modulate(x_ref[...], n2_ref[...], mod_ref[3:4, :], mod_ref[4:5, :])
    _store_token_tiles(h_ref, h)
    lg = _dot_split(h, wr_ref[...]) + br_ref[...]
    lane = lax.broadcasted_iota(jnp.int32, lg.shape, 1)
    neg = jnp.float32(-jnp.inf)
    big = jnp.int32(1 << 20)
    is_grp = jnp.logical_and(lane >= N_EXPERTS, lane < N_EXPERTS + N_GROUPS)
    gl = jnp.where(is_grp, lg, neg)
    gmax = jnp.max(gl, axis=-1, keepdims=True)
    gidx = jnp.min(jnp.where(gl == gmax, lane - N_EXPERTS, big), axis=-1, keepdims=True)
    g_w = 1.0 / jnp.sum(jnp.where(is_grp, jnp.exp(gl - gmax), 0.0), axis=-1, keepdims=True)
    in_grp = jnp.logical_and(lane < N_EXPERTS, jnp.right_shift(lane, 3) == gidx)
    e1 = jnp.where(in_grp, lg, neg)
    m1 = jnp.max(e1, axis=-1, keepdims=True)
    i1 = jnp.min(jnp.where(e1 == m1, lane, big), axis=-1, keepdims=True)
    e2 = jnp.where(lane == i1, neg, e1)
    m2 = jnp.max(e2, axis=-1, keepdims=True)
    i2 = jnp.min(jnp.where(e2 == m2, lane, big), axis=-1, keepdims=True)
    t = jnp.exp(m2 - m1)
    w1 = g_w / (1.0 + t)
    w2 = g_w * t / (1.0 + t)

    hit1 = lane == i1
    hit2 = lane == i2
    onehot = jnp.where(hit1, 1.0, 0.0) + jnp.where(hit2, 1.0, 0.0)
    ri = lax.broadcasted_iota(jnp.int32, (TM, TM), 0)
    rj = lax.broadcasted_iota(jnp.int32, (TM, TM), 1)
    earlier = jnp.dot(jnp.where(rj < ri, 1.0, 0.0).astype(BF16), onehot.astype(BF16),
                      preferred_element_type=F32)
    base = run_s[0:1, :] + earlier
    r1 = jnp.sum(jnp.where(hit1, base, 0.0), axis=-1, keepdims=True)
    r2 = jnp.sum(jnp.where(hit2, base, 0.0), axis=-1, keepdims=True)
    total = run_s[0:1, :] + jnp.sum(onehot, axis=0, keepdims=True)
    run_s[0:1, :] = total
    cnt_ref[...] = jnp.broadcast_to(total, cnt_ref.shape)

    rec = jnp.zeros(lg.shape, F32)
    for ln, val in ((SEL_E1, i1.astype(F32)), (SEL_E2, i2.astype(F32)), (SEL_W1, w1), (SEL_W2, w2),
                    (SEL_R1, r1), (SEL_R2, r2)):
        rec = jnp.where(lane == ln, val, rec)
    sel_ref[...] = rec


def _route_call(x, mods, norm2, w_route, b_route, mrow):
    rows = x.shape[0]
    return pl.pallas_call(
        _route_kernel,
        grid=(rows // TM,),
        in_specs=[
            pl.BlockSpec((TM, D_MODEL), lambda i: (i, 0)),
            pl.BlockSpec((None, 6, D_MODEL), lambda i: (mrow(i), 0, 0)),
            pl.BlockSpec((1, D_MODEL), lambda i: (0, 0)),
            pl.BlockSpec((D_MODEL, LANES), lambda i: (0, 0)),
            pl.BlockSpec((1, LANES), lambda i: (0, 0)),
        ],
        out_specs=[
            pl.BlockSpec((TM * ROW_TILES, LANES), lambda i: (i, 0)),
            pl.BlockSpec((TM, LANES), lambda i: (i, 0)),
            pl.BlockSpec((SUBLANES, LANES), lambda i: (0, 0)),
        ],
        out_shape=[
            jax.ShapeDtypeStruct((rows * ROW_TILES, LANES), F32),
            jax.ShapeDtypeStruct((rows, LANES), F32),
            jax.ShapeDtypeStruct((SUBLANES, LANES), F32),
        ],
        scratch_shapes=[pltpu.VMEM((SUBLANES, LANES), F32)],
        compiler_params=_cparams(("arbitrary",)),
        name="moe_route",
    )(x, mods, norm2, w_route, b_route)


def _moe_plan(sel, cnt, n_tok, e_base):
    n_asg = 2 * n_tok
    max_tiles = n_asg // TE + N_EXPERTS
    eids = jnp.arange(N_EXPERTS, dtype=jnp.int32)
    counts = cnt[0, :N_EXPERTS].astype(jnp.int32)
    tiles_e = (counts + TE - 1) // TE
    cum_tiles = jnp.cumsum(tiles_e)
    n_tiles = cum_tiles[-1]
    off = (cum_tiles - tiles_e) * TE
    start = jnp.cumsum(counts) - counts
    e = sel[:, SEL_E1:SEL_E2 + 1].astype(jnp.int32)
    rank = sel[:, SEL_R1:SEL_R2 + 1].astype(jnp.int32)
    off_e = jnp.sum(jnp.where(e[:, :, None] == eids, off, 0), axis=-1)
    pos = (off_e + rank).reshape(n_asg)
    tok = jnp.arange(n_asg, dtype=jnp.int32) // 2
    _, tok_sorted = lax.sort_key_val(pos, tok)
    last_e = jnp.max(jnp.where(tiles_e > 0, eids, 0))
    t_idx = jnp.arange(max_tiles, dtype=jnp.int32)
    tile_e = jnp.minimum(jnp.sum((cum_tiles[None, :] <= t_idx[:, None]).astype(jnp.int32), axis=1), last_e)
    r_d = (t_idx * TE - off[tile_e])[:, None] + jnp.arange(TE, dtype=jnp.int32)[None, :]
    valid = jnp.logical_and(r_d < counts[tile_e][:, None], (t_idx < n_tiles)[:, None])
    idx = jnp.clip(start[tile_e][:, None] + r_d, 0, n_asg - 1)
    filler = (t_idx * TE)[:, None] % n_tok + jnp.arange(TE, dtype=jnp.int32)[None, :]
    src = jnp.where(valid, tok_sorted[idx], filler).reshape(max_tiles * TE).astype(jnp.int32)
    prev_e = jnp.concatenate([tile_e[:1] - 1, tile_e[:-1]])
    first = jnp.logical_and(tile_e != prev_e, t_idx < n_tiles).astype(jnp.int32)
    run = jnp.cumsum(first) - 1
    later = jnp.logical_and(eids[None, :] > eids[:, None], (tiles_e > 0)[None, :])
    nxt_e = jnp.min(jnp.where(later, eids[None, :], N_EXPERTS), axis=1)
    nxt = jnp.where(nxt_e == N_EXPERTS, eids, nxt_e)[tile_e]
    meta = ((tile_e + e_base).astype(jnp.int32), n_tiles.reshape(1).astype(jnp.int32), src,
            first, run.astype(jnp.int32), (nxt + e_base).astype(jnp.int32))
    return meta, pos


def _moe_experts_kernel(te_ref, nt_ref, src_ref, first_ref, run_ref, nxt_ref,
                        h_hbm, wg_hbm, wu_hbm, wd_hbm, y_ref,
                        buf, xb, wg32, wu32, wd32, wg16, wu16, wd16, sem, wsem):
    t = pl.program_id(0)
    n_tiles = nt_ref[0]

    def row_copy(tok, slot, r):
        return pltpu.make_async_copy(_token_tile(h_hbm, tok), _token_tile(buf.at[slot], r), sem.at[slot])

    def weight_copies(e, slot):
        return [pltpu.make_async_copy(src.at[e], dst.at[slot], wsem.at[slot])
                for src, dst in ((wg_hbm, wg32), (wu_hbm, wu32), (wd_hbm, wd32))]

    @pl.when(t == 0)
    def _():
        for cp in weight_copies(te_ref[0], 0):
            cp.start()

    for k in range(GATHER_DEPTH - 1):
        @pl.when(jnp.logical_and(t == 0, k < n_tiles))
        def _(k=k):
            def body(r, carry):
                row_copy(src_ref[k * TE + r], k, r).start()
                return carry

            lax.fori_loop(0, TE, body, 0, unroll=8)

    run_starts = first_ref[t] != 0
    wslot = lax.rem(run_ref[t], 2)

    @pl.when(jnp.logical_and(run_starts, nxt_ref[t] != te_ref[t]))
    def _():
        for cp in weight_copies(nxt_ref[t], 1 - wslot):
            cp.start()

    @pl.when(run_starts)
    def _():
        for cp in weight_copies(te_ref[t], wslot):
            cp.wait()
        wg16[...] = wg32[wslot].astype(BF16)
        wu16[...] = wu32[wslot].astype(BF16)
        wd16[...] = wd32[wslot].astype(BF16)

    ahead = GATHER_DEPTH - 1

    def run_tile(prefetch):
        slot = lax.rem(t, GATHER_DEPTH)
        pltpu.make_async_copy(h_hbm.at[pl.ds(0, TE * ROW_TILES), :], buf.at[slot], sem.at[slot]).wait()
        for s in range(ROW_TILES):
            xb[:, s * LANES:(s + 1) * LANES] = _load_token_tiles(buf.at[slot], s, TE).astype(BF16)
        if prefetch:
            base = (t + ahead) * TE
            nslot = lax.rem(t + ahead, GATHER_DEPTH)
            for r in range(TE):
                row_copy(src_ref[base + r], nslot, r).start()
        x = xb[...]
        hg = jnp.dot(x, wg16[...], preferred_element_type=F32)
        hu = jnp.dot(x, wu16[...], preferred_element_type=F32)
        act = (_silu(hg) * hu).astype(BF16)
        _store_token_tiles(y_ref, jnp.dot(act, wd16[...], preferred_element_type=F32))

    @pl.when(t + ahead < n_tiles)
    def _():
        run_tile(True)

    @pl.when(jnp.logical_and(t < n_tiles, t + ahead >= n_tiles))
    def _():
        run_tile(False)

    @pl.when(t >= n_tiles)
    def _():
        y_ref[...] = jnp.zeros_like(y_ref)


def _moe_experts(h2, meta, w_gate, w_up, w_down):
    max_tiles = meta[0].shape[0]
    any_spec = pl.BlockSpec(memory_space=pl.ANY)
    return pl.pallas_call(
        _moe_experts_kernel,
        grid_spec=pltpu.PrefetchScalarGridSpec(
            num_scalar_prefetch=len(meta),
            grid=(max_tiles,),
            in_specs=[any_spec] * 4,
            out_specs=pl.BlockSpec((TE * ROW_TILES, LANES), lambda t, *_: (t, 0)),
            scratch_shapes=[
                pltpu.VMEM((GATHER_DEPTH, TE * ROW_TILES, LANES), F32),
                pltpu.VMEM((TE, D_MODEL), BF16),
                pltpu.VMEM((2, D_MODEL, D_EXPERT), F32),
                pltpu.VMEM((2, D_MODEL, D_EXPERT), F32),
                pltpu.VMEM((2, D_EXPERT, D_MODEL), F32),
                pltpu.VMEM((D_MODEL, D_EXPERT), BF16),
                pltpu.VMEM((D_MODEL, D_EXPERT), BF16),
                pltpu.VMEM((D_EXPERT, D_MODEL), BF16),
                pltpu.SemaphoreType.DMA((GATHER_DEPTH,)),
                pltpu.SemaphoreType.DMA((2,)),
            ],
        ),
        out_shape=jax.ShapeDtypeStruct((max_tiles * TE * ROW_TILES, LANES), F32),
        compiler_params=_cparams(("arbitrary",)),
        name="moe_experts",
    )(*meta, h2, w_gate, w_up, w_down)


def _moe_combine_kernel(final, pos_ref, y_hbm, x_ref, sel_ref, mod_ref, fn_ref, o_ref, buf, sem):
    i = pl.program_id(0)
    n = pl.num_programs(0)

    def row_copy(p, slot, k, r):
        return pltpu.make_async_copy(_token_tile(y_hbm, p), _token_tile(buf.at[slot, k], r), sem.at[slot])

    @pl.when(i == 0)
    def _():
        def body(r, carry):
            row_copy(pos_ref[2 * r], 0, 0, r).start()
            row_copy(pos_ref[2 * r + 1], 0, 1, r).start()
            return carry

        lax.fori_loop(0, TM, body, 0, unroll=8)

    slot = i % 2

    @pl.when(i + 1 < n)
    def _():
        base = (i + 1) * (2 * TM)
        for r in range(TM):
            row_copy(pos_ref[base + 2 * r], 1 - slot, 0, r).start(priority=0)
            row_copy(pos_ref[base + 2 * r + 1], 1 - slot, 1, r).start(priority=1)

    for k in range(2):
        pltpu.make_async_copy(y_hbm.at[pl.ds(0, TM * ROW_TILES), :], buf.at[slot, k], sem.at[slot]).wait()
    sel = sel_ref[...]
    w1 = sel[:, SEL_W1:SEL_W1 + 1]
    w2 = sel[:, SEL_W2:SEL_W2 + 1]
    parts = []
    for s in range(ROW_TILES):
        sl = slice(s * LANES, (s + 1) * LANES)
        y = w1 * _load_token_tiles(buf.at[slot, 0], s, TM) + w2 * _load_token_tiles(buf.at[slot, 1], s, TM)
        parts.append(x_ref[:, sl] + mod_ref[5:6, sl] * y)
    x = jnp.concatenate(parts, axis=-1)
    if final:
        x = x * lax.rsqrt(jnp.mean(x * x, axis=-1, keepdims=True) + EPS) * fn_ref[...]
    o_ref[...] = x


def _moe_combine(y_sorted, pos, x, sel, mods, final_norm, mrow, final):
    rows = x.shape[0]
    wide = pl.BlockSpec((TM, D_MODEL), lambda i, ps: (i, 0))
    return pl.pallas_call(
        functools.partial(_moe_combine_kernel, final),
        grid_spec=pltpu.PrefetchScalarGridSpec(
            num_scalar_prefetch=1,
            grid=(rows // TM,),
            in_specs=[
                pl.BlockSpec(memory_space=pl.ANY),
                wide,
                pl.BlockSpec((TM, LANES), lambda i, ps: (i, 0)),
                pl.BlockSpec((None, 6, D_MODEL), lambda i, ps: (mrow(i), 0, 0)),
                pl.BlockSpec((1, D_MODEL), lambda i, ps: (0, 0)),
            ],
            out_specs=wide,
            scratch_shapes=[pltpu.VMEM((2, 2, TM * ROW_TILES, LANES), F32), pltpu.SemaphoreType.DMA((2,))],
        ),
        out_shape=jax.ShapeDtypeStruct((rows, D_MODEL), F32),
        compiler_params=_cparams(("arbitrary",)),
        name="moe_combine",
    )(pos, y_sorted, x, sel, mods, final_norm)


def _moe_block(x, mods, norm2, w_grp, b_grp, w_exp, b_exp, w_gate, w_up, w_down, layer, final_norm, mrow, final):
    pad = LANES - N_EXPERTS - N_GROUPS
    w_route = jnp.concatenate([w_exp, w_grp, jnp.zeros((D_MODEL, pad), F32)], axis=1)
    b_route = jnp.concatenate([b_exp, b_grp, jnp.zeros((pad,), F32)]).reshape(1, LANES)
    h2, sel, cnt = _route_call(x, mods, norm2, w_route, b_route, mrow)
    meta, pos = _moe_plan(sel, cnt, x.shape[0], layer * N_EXPERTS)
    y_sorted = _moe_experts(h2, meta, w_gate, w_up, w_down)
    return _moe_combine(y_sorted, pos, x, sel, mods, final_norm, mrow, final)


def _ab_params(w_in, conv_qkv, a_log, dt_bias, conv_x, rg_wr, rg_br, rg_wi, rg_bi, rg_lam):
    o1 = 3 * A_WIDTH
    o2 = 4 * A_WIDTH
    o3 = o2 + 4 * A_HEADS
    o4 = o3 + B_WIDTH
    gate_cols = jnp.concatenate([w_in[:, o2:o3], jnp.zeros((D_MODEL, LANES - 4 * A_HEADS), F32)], axis=1)
    w_pad = jnp.concatenate([w_in[:, :o1], w_in[:, o3:o4], w_in[:, o1:o2], w_in[:, o4:], gate_cols],
                            axis=1).astype(BF16)
    z4 = jnp.zeros((A_HEADS,), F32)
    ztail = jnp.zeros((LANES - 4 * A_HEADS,), F32)
    alog_vec = jnp.concatenate([a_log[0], z4, a_log[1], z4, ztail]).reshape(1, LANES)
    dt_vec = jnp.concatenate([dt_bias[0], z4, dt_bias[1], z4, ztail]).reshape(1, LANES)
    eye = jnp.eye(B_BLOCKS, dtype=F32)
    bdiag = lambda w: jnp.einsum('gkj,gh->gkhj', w, eye).reshape(B_WIDTH, B_WIDTH)
    wbd = jnp.concatenate([bdiag(rg_wr[0]), bdiag(rg_wr[1]), bdiag(rg_wi[0]), bdiag(rg_wi[1])], axis=1)
    bbd = jnp.concatenate([rg_br[0].reshape(-1), rg_br[1].reshape(-1),
                           rg_bi[0].reshape(-1), rg_bi[1].reshape(-1)]).reshape(1, 4 * B_WIDTH)
    lam = rg_lam.reshape(1, 2 * B_WIDTH)
    return w_pad, alog_vec, dt_vec, wbd.astype(BF16), bbd, lam


def _attn_params(w_qkv, q_norm, k_norm, t_lat, c_rows):
    half = C_HD // 2
    perm = jnp.concatenate([jnp.arange(half) * 2, jnp.arange(half) * 2 + 1])
    nrot = (C_HEADS + C_KV_HEADS) * C_HD
    cols = (jnp.arange(C_HEADS + C_KV_HEADS)[:, None] * C_HD + perm[None, :]).reshape(-1)
    cols = jnp.concatenate([cols, jnp.arange(nrot, w_qkv.shape[1])])
    w_perm = w_qkv[:, cols].astype(BF16)
    qn = q_norm[perm].reshape(1, C_HD)
    kn = k_norm[perm].reshape(1, C_HD)
    return w_perm, qn, kn


def _rope_tables(t_lat, c_rows, grid_w):
    rows = t_lat // grid_w
    row = np.repeat(np.arange(rows, dtype=np.float64), grid_w)
    col = np.tile(np.arange(grid_w, dtype=np.float64), rows)
    n_freq = C_HD // 4
    inv = ROPE_THETA ** (-np.arange(n_freq, dtype=np.float64) / n_freq)
    ang = np.concatenate([row[:, None] * inv, col[:, None] * inv], axis=-1)
    cos = np.cos(ang).astype(np.float32)
    sin = np.sin(ang).astype(np.float32)
    cos_tab = np.concatenate([np.ones((c_rows, C_HD), np.float32), np.concatenate([cos, cos], axis=-1)], axis=0)
    sin_tab = np.concatenate([np.zeros((c_rows, C_HD), np.float32), np.concatenate([-sin, sin], axis=-1)], axis=0)
    return jnp.asarray(cos_tab), jnp.asarray(sin_tab)


GRID_W = 64


def kernel(x, c, ctx, c_ctx, ada_w, ada_b, norm1, norm2, final_norm, ab_w_in, ab_conv_qkv, ab_a_log, ab_dt_bias, ab_onorm, ab_conv_x, ab_rg_wr, ab_rg_br, ab_rg_wi, ab_rg_bi, ab_rg_lam, ab_w_out, at_w_qkv, at_q_norm, at_k_norm, at_w_out, moe_w_grp, moe_b_grp, moe_w_exp, moe_b_exp, moe_w_gate, moe_w_up, moe_w_down):
    nb, t_lat, _ = x.shape
    c_rows = ctx.shape[1]
    p_rows = c_rows + t_lat
    nt = p_rows // TM
    nc = c_rows // TM
    depth = ada_w.shape[0]
    assert depth == 2 and nb < 16 and c_rows % TM == 0 and t_lat % TM == 0

    cond = jnp.concatenate([c, c_ctx[None], jnp.zeros((16 - nb - 1, D_MODEL), F32)], axis=0)
    mods = _ada_mod(cond, ada_w, ada_b)
    xu = jnp.concatenate([ctx, x], axis=1).reshape(nb * p_rows, D_MODEL)
    uni_mrow = lambda i: jnp.where(i % nt < nc, nb, i // nt)
    fnorm = final_norm.reshape(1, D_MODEL)
    w_gate = moe_w_gate.reshape(depth * N_EXPERTS, D_MODEL, D_EXPERT)
    w_up = moe_w_up.reshape(depth * N_EXPERTS, D_MODEL, D_EXPERT)
    w_down = moe_w_down.reshape(depth * N_EXPERTS, D_EXPERT, D_MODEL)

    w_pad, alog_vec, dt_vec, wbd, bbd, lam = _ab_params(
        ab_w_in[0], ab_conv_qkv[0], ab_a_log[0], ab_dt_bias[0], ab_conv_x[0],
        ab_rg_wr[0], ab_rg_br[0], ab_rg_wi[0], ab_rg_bi[0], ab_rg_lam[0])
    q, k, v, gout, yb, gates, a0, b0, a1, b1 = _inproj_ab(
        xu, mods[0], norm1[0].reshape(1, D_MODEL), w_pad, ab_conv_qkv[0], ab_conv_x[0],
        alog_vec, dt_vec, wbd, bbd, lam, nb, nt, nc)
    o_f, o_b = _delta(q, k, v, gates, nb, p_rows, c_rows)
    h_f, h_b = _lru(a0, b0, a1, b1, nb, nt, nc)
    xu = _merge_ab(o_f, o_b, h_f, h_b, gout, yb, xu, mods[0], ab_onorm[0].reshape(1, A_DK),
                   ab_w_out[0].astype(BF16), nb, nt, nc)
    xu = _moe_block(xu, mods[0], norm2[0].reshape(1, D_MODEL), moe_w_grp[0], moe_b_grp[0], moe_w_exp[0],
                    moe_b_exp[0], w_gate, w_up, w_down, 0, fnorm, uni_mrow, False)

    w_perm, qn, kn = _attn_params(at_w_qkv[0], at_q_norm[0], at_k_norm[0], t_lat, c_rows)
    cos_tab, sin_tab = _rope_tables(t_lat, c_rows, GRID_W)
    q, k, v = _inproj_attn(xu, mods[1], norm1[1].reshape(1, D_MODEL), w_perm, qn, kn, cos_tab, sin_tab,
                           nb, nt, nc)
    att = _attention(q, k, v, nb, nt, nc, p_rows)
    xl = _outproj_lat(att, xu, mods[1], at_w_out[0].astype(BF16), nt, nc)
    nq = nt - nc
    xl = _moe_block(xl, mods[1], norm2[1].reshape(1, D_MODEL), moe_w_grp[1], moe_b_grp[1], moe_w_exp[1],
                    moe_b_exp[1], w_gate, w_up, w_down, 1, fnorm, lambda j: j // nq, True)
    return xl.reshape(nb, t_lat, D_MODEL)
```
